```python
import math
import jax, jax.numpy as jnp
from jax import lax
import numpy as np

D_MODEL = 1024
BATCH = 8
SEQ = 2048
DEPTH = 4
DEC_BATCH = 128
DEC_SEQ = 1
PAST_LEN = 2048
PAGE_SIZE = 128

HEAD_DIM = 64
N_BRANCH = 4
BRANCH_W = D_MODEL // N_BRANCH
S5_CH_PER_GROUP = 16
S5_GROUPS = BRANCH_W // S5_CH_PER_GROUP
S5_STATE = 64
S5_DT_MIN = 0.001
S5_DT_MAX = 0.1
RWKV_HEADS = BRANCH_W // HEAD_DIM
RWKV_DECAY_LORA = 64
RWKV_AAA_LORA = 64
RWKV_GATE_LORA = 128
RWKV_IN_W = 3 * BRANCH_W + RWKV_DECAY_LORA + RWKV_AAA_LORA + RWKV_GATE_LORA
RWKV_LN_EPS = 64e-5
RET_HEADS = BRANCH_W // HEAD_DIM
RET_CHUNK = 128
RET_IN_W = 4 * BRANCH_W
ROPE_BASE = 10000.0
MOBA_HEADS = BRANCH_W // HEAD_DIM
MOBA_BLOCK = 256
MOBA_TOPK = 3
MOBA_Q_BLOCK = 128
MOBA_IN_W = 3 * BRANCH_W
GATE_W = N_BRANCH * D_MODEL
IN_W = BRANCH_W + RWKV_IN_W + RET_IN_W + MOBA_IN_W + GATE_W
IN_SPLITS = [BRANCH_W, BRANCH_W + RWKV_IN_W, BRANCH_W + RWKV_IN_W + RET_IN_W, BRANCH_W + RWKV_IN_W + RET_IN_W + MOBA_IN_W]
RWKV_SPLITS = [BRANCH_W, 2 * BRANCH_W, 3 * BRANCH_W, 3 * BRANCH_W + RWKV_DECAY_LORA, 3 * BRANCH_W + RWKV_DECAY_LORA + RWKV_AAA_LORA]
D_FF = 256 * ((8 * D_MODEL // 3 + 255) // 256)
N_EXPERTS = 8
TOP_K = 2
D_FF_EXPERT = D_FF // TOP_K
N_DENSE = (DEPTH + 1) // 2
N_MOE = DEPTH // 2
RMS_EPS = 1e-6

kernel_name = 'gated_parallel_hybrid_decoder_step'


def _rmsnorm(x, g):
    xf = x.astype(jnp.float32)
    y = xf * lax.rsqrt(jnp.mean(xf * xf, axis=-1, keepdims=True) + RMS_EPS)
    return (y * g.astype(jnp.float32)).astype(x.dtype)


def _cmul(ar, ai, br, bi):
    return ar * br - ai * bi, ar * bi + ai * br


def _s5_combine(e1, e2):
    a1r, a1i, b1r, b1i = e1
    a2r, a2i, b2r, b2i = e2
    ar, ai = _cmul(a2r, a2i, a1r, a1i)
    br, bi = _cmul(a2r, a2i, b1r, b1i)
    return ar, ai, br + b2r, bi + b2i


def s5_branch(u, x0_re, x0_im, lam_re, lam_im, log_dt, b_re, b_im, c_re, c_im, d_skip, w_glu):
    f32 = jnp.float32
    bsz, t_len, _ = u.shape
    uf = u.astype(f32)
    lam_re = lam_re.astype(f32)
    lam_im = lam_im.astype(f32)
    dt = jnp.exp(log_dt.astype(f32))[:, None]
    mag = jnp.exp(lam_re * dt)
    lb_re = mag * jnp.cos(lam_im * dt)
    lb_im = mag * jnp.sin(lam_im * dt)
    den = lam_re * lam_re + lam_im * lam_im
    q_re = ((lb_re - 1.0) * lam_re + lb_im * lam_im) / den
    q_im = (lb_im * lam_re - (lb_re - 1.0) * lam_im) / den
    bb_re, bb_im = _cmul(q_re[..., None], q_im[..., None], b_re.astype(f32), b_im.astype(f32))
    ug = uf.reshape(bsz, t_len, S5_GROUPS, S5_CH_PER_GROUP)
    bu_re = jnp.einsum('btgc,gnc->btgn', ug, bb_re)
    bu_im = jnp.einsum('btgc,gnc->btgn', ug, bb_im)
    i0_re, i0_im = _cmul(lb_re, lb_im, x0_re.astype(f32), x0_im.astype(f32))
    bu_re = bu_re.at[:, 0].add(i0_re)
    bu_im = bu_im.at[:, 0].add(i0_im)
    a_re = jnp.broadcast_to(lb_re, bu_re.shape)
    a_im = jnp.broadcast_to(lb_im, bu_im.shape)
    _, _, xs_re, xs_im = lax.associative_scan(_s5_combine, (a_re, a_im, bu_re, bu_im), axis=1)
    y = jnp.einsum('gcn,btgn->btgc', c_re.astype(f32), xs_re) - jnp.einsum('gcn,btgn->btgc', c_im.astype(f32), xs_im)
    y = y.reshape(bsz, t_len, BRANCH_W) + d_skip.astype(f32) * uf
    z = jax.nn.gelu(y)
    out = z * jax.nn.sigmoid(z @ w_glu.astype(f32))
    return out.astype(u.dtype), xs_re[:, -1], xs_im[:, -1]


def rwkv_branch(p, shift0, s0, mu, w0, w2, a0, a2, g2, k_k, k_a, r_k, ln_g, ln_b):
    f32 = jnp.float32
    bsz, t_len, _ = p.shape
    p_prev = jnp.concatenate([shift0[:, None, :].astype(p.dtype), p[:, :-1]], axis=1)
    pm = p + (p_prev - p) * mu
    r, k, v, xw, xa, xg = jnp.split(pm, RWKV_SPLITS, axis=-1)
    w_raw = (w0 + jnp.tanh(xw) @ w2).astype(f32)
    w_log = -jax.nn.softplus(-w_raw) - 0.5
    decay = jnp.exp(-jnp.exp(w_log))
    a = jax.nn.sigmoid((a0 + xa @ a2).astype(f32))
    g = jax.nn.sigmoid(xg) @ g2

    def heads(t):
        return t.astype(f32).reshape(bsz, t_len, RWKV_HEADS, HEAD_DIM)

    kk = heads(k * k_k)
    kk = kk * lax.rsqrt(jnp.sum(kk * kk, axis=-1, keepdims=True) + 1e-12)
    k_mod = k.astype(f32) * (1.0 + (a - 1.0) * k_a.astype(f32))
    rh, kh, vh, wh, ah = heads(r), heads(k_mod), heads(v), heads(decay), heads(a)
    aa = -kk
    bb = kk * ah

    def step(s, inp):
        r_t, k_t, v_t, w_t, a_t, b_t = inp
        sa = jnp.einsum('bhvk,bhk->bhv', s, a_t)
        s = s * w_t[:, :, None, :] + sa[..., None] * b_t[:, :, None, :] + v_t[..., None] * k_t[:, :, None, :]
        return s, jnp.einsum('bhvk,bhk->bhv', s, r_t)

    tl = lambda t: jnp.swapaxes(t, 0, 1)
    s_fin, ys = lax.scan(step, s0.astype(f32), (tl(rh), tl(kh), tl(vh), tl(wh), tl(aa), tl(bb)))
    y = tl(ys)
    mean = jnp.mean(y, axis=-1, keepdims=True)
    var = jnp.mean(jnp.square(y - mean), axis=-1, keepdims=True)
    y = ((y - mean) * lax.rsqrt(var + RWKV_LN_EPS)).reshape(bsz, t_len, BRANCH_W) * ln_g.astype(f32) + ln_b.astype(f32)
    bonus = jnp.sum(rh * kh * r_k.astype(f32), axis=-1, keepdims=True) * vh
    out = (y + bonus.reshape(bsz, t_len, BRANCH_W)) * g.astype(f32)
    return out.astype(p.dtype), s_fin, p[:, -1]


def _rope(x, pos):
    half = HEAD_DIM // 2
    freqs = 1.0 / (ROPE_BASE ** jnp.linspace(0.0, 1.0, half, dtype=jnp.float32))
    ang = pos[:, None] * freqs[None, :]
    cos = jnp.cos(ang)[None, :, None, :]
    sin = jnp.sin(ang)[None, :, None, :]
    x1, x2 = x[..., :half], x[..., half:]
    return jnp.concatenate([x1 * cos - x2 * sin, x1 * sin + x2 * cos], axis=-1)


def retention_branch(p, s0, pos0):
    f32 = jnp.float32
    bsz, t_len, _ = p.shape
    q, k, v, g = jnp.split(p, 4, axis=-1)

    def heads(t):
        return t.astype(f32).reshape(bsz, t_len, RET_HEADS, HEAD_DIM)

    pos = jnp.arange(t_len, dtype=f32) + pos0
    qh = _rope(heads(q), pos)
    kh = _rope(heads(k), pos) * (HEAD_DIM ** -0.5)
    vh = heads(v)
    log_gamma = jnp.log(1.0 - jnp.exp2(-5.0 - jnp.arange(RET_HEADS, dtype=f32)))
    csz = math.gcd(t_len, RET_CHUNK)
    n_chunks = t_len // csz
    i = jnp.arange(csz, dtype=f32)
    diff = i[:, None] - i[None, :]
    dmask = jnp.where(diff >= 0, jnp.exp(log_gamma[:, None, None] * jnp.maximum(diff, 0.0)), 0.0)
    xi = jnp.exp(log_gamma[None, :] * (i[:, None] + 1.0))
    zeta = jnp.exp(log_gamma[None, :] * (csz - 1.0 - i[:, None]))
    g_chunk = jnp.exp(log_gamma * csz)

    def chunks(t):
        return jnp.swapaxes(t.reshape(bsz, n_chunks, csz, RET_HEADS, HEAD_DIM), 0, 1)

    def step(s, inp):
        qc, kc, vc = inp
        att = jnp.einsum('bihd,bjhd->bhij', qc, kc) * dmask[None]
        intra = jnp.einsum('bhij,bjhe->bihe', att, vc)
        inter = jnp.einsum('bihd,bhde->bihe', qc, s) * xi[None, :, :, None]
        s = s * g_chunk[None, :, None, None] + jnp.einsum('bjhd,bjhe->bhde', kc * zeta[None, :, :, None], vc)
        return s, intra + inter

    s_fin, o = lax.scan(step, s0.astype(f32), (chunks(qh), chunks(kh), chunks(vh)))
    o = jnp.swapaxes(o, 0, 1).reshape(bsz, t_len, RET_HEADS, HEAD_DIM)
    o = o * lax.rsqrt(jnp.mean(o * o, axis=-1, keepdims=True) + RMS_EPS)
    out = o.reshape(bsz, t_len, BRANCH_W) * jax.nn.silu(g.astype(f32))
    return out.astype(p.dtype), s_fin


def moba_branch(p, pos0, k_past, v_past):
    f32 = jnp.float32
    bsz, t_len, _ = p.shape
    q, k, v = jnp.split(p, 3, axis=-1)
    q = q.reshape(bsz, t_len, MOBA_HEADS, HEAD_DIM)
    k = k.reshape(bsz, t_len, MOBA_HEADS, HEAD_DIM)
    v = v.reshape(bsz, t_len, MOBA_HEADS, HEAD_DIM)
    if k_past is None:
        k_all, v_all = k, v
    else:
        k_all = jnp.concatenate([k_past.astype(k.dtype), k], axis=1)
        v_all = jnp.concatenate([v_past.astype(v.dtype), v], axis=1)
    total = k_all.shape[1]
    n_blk = -(-total // MOBA_BLOCK)
    pad = n_blk * MOBA_BLOCK - total
    if pad > 0:
        k_all = jnp.pad(k_all, ((0, 0), (0, pad), (0, 0), (0, 0)))
        v_all = jnp.pad(v_all, ((0, 0), (0, pad), (0, 0), (0, 0)))
    kb = k_all.reshape(bsz, n_blk, MOBA_BLOCK, MOBA_HEADS, HEAD_DIM).transpose(0, 3, 1, 2, 4)
    vb = v_all.reshape(bsz, n_blk, MOBA_BLOCK, MOBA_HEADS, HEAD_DIM).transpose(0, 3, 1, 2, 4)
    kmean = jnp.mean(kb.astype(f32), axis=3)
    n_sel = min(MOBA_TOPK, n_blk)
    scale = HEAD_DIM ** -0.5
    bi = jnp.arange(bsz)[:, None, None, None]
    hi = jnp.arange(MOBA_HEADS)[None, None, :, None]
    blk_ids = jnp.arange(n_blk)
    in_blk = jnp.arange(MOBA_BLOCK)

    def attend(qc, pc):
        tq = qc.shape[1]
        own = pc // MOBA_BLOCK
        qf = qc.astype(f32)
        gs = jnp.einsum('bqhd,bhnd->bqhn', qf, kmean)
        past_ok = blk_ids[None, :] < own[:, None]
        gs = jnp.where(past_ok[None, :, None, :], gs, -jnp.inf)
        _, idx = lax.top_k(gs, n_sel)
        valid = idx < own[None, :, None, None]
        k_sel = kb[bi, hi, idx].astype(f32)
        v_sel = vb[bi, hi, idx].astype(f32)
        k_own = kb[bi[..., 0], hi[..., 0], own[None, :, None]].astype(f32)
        v_own = vb[bi[..., 0], hi[..., 0], own[None, :, None]].astype(f32)
        s_sel = jnp.einsum('bqhd,bqhjkd->bqhjk', qf, k_sel) * scale
        s_sel = jnp.where(valid[..., None], s_sel, -jnp.inf)
        key_pos = own[:, None] * MOBA_BLOCK + in_blk[None, :]
        s_own = jnp.einsum('bqhd,bqhkd->bqhk', qf, k_own) * scale
        s_own = jnp.where((key_pos <= pc[:, None])[None, :, None, :], s_own, -jnp.inf)
        s = jnp.concatenate([s_sel.reshape(bsz, tq, MOBA_HEADS, n_sel * MOBA_BLOCK), s_own], axis=-1)
        pr = jax.nn.softmax(s, axis=-1)
        p_sel = pr[..., : n_sel * MOBA_BLOCK].reshape(bsz, tq, MOBA_HEADS, n_sel, MOBA_BLOCK)
        p_own = pr[..., n_sel * MOBA_BLOCK:]
        o = jnp.einsum('bqhjk,bqhjkd->bqhd', p_sel, v_sel) + jnp.einsum('bqhk,bqhkd->bqhd', p_own, v_own)
        return o.astype(qc.dtype)

    q_pos = jnp.arange(t_len, dtype=jnp.int32) + pos0
    if t_len > MOBA_Q_BLOCK and t_len % MOBA_Q_BLOCK == 0:
        nq = t_len // MOBA_Q_BLOCK
        qs = q.reshape(bsz, nq, MOBA_Q_BLOCK, MOBA_HEADS, HEAD_DIM).transpose(1, 0, 2, 3, 4)
        ps = q_pos.reshape(nq, MOBA_Q_BLOCK)
        o = lax.map(lambda a: attend(a[0], a[1]), (qs, ps))
        o = o.transpose(1, 0, 2, 3, 4).reshape(bsz, t_len, BRANCH_W)
    else:
        o = attend(q, q_pos).reshape(bsz, t_len, BRANCH_W)
    return o, k, v


def dense_ffn(h, w1, w3, w2):
    return (jax.nn.silu(h @ w1) * (h @ w3)) @ w2


def moe_ffn(h, router, w1, w3, w2):
    logits = (h @ router).astype(jnp.float32)
    top_v, top_i = lax.top_k(logits, TOP_K)
    wts = jax.nn.softmax(top_v, axis=-1)
    combine = jnp.sum(jax.nn.one_hot(top_i, N_EXPERTS, dtype=jnp.float32) * wts[..., None], axis=-2)
    out = jnp.zeros_like(h)
    for e in range(N_EXPERTS):
        y_e = (jax.nn.silu(h @ w1[e]) * (h @ w3[e])) @ w2[e]
        out = out + combine[..., e:e + 1].astype(h.dtype) * y_e
    return out


def layer(x, c, pos0, s5_re0, s5_im0, rwkv_s0, shift0, ret_s0, k_past, v_past, lp):
    bsz, t_len, _ = x.shape
    mod = jax.nn.silu(c) @ lp['w_ada'] + lp['b_ada']
    sh1, sc1, gt1, sh2, sc2, gt2 = jnp.split(mod[:, None, :], 6, axis=-1)
    h = _rmsnorm(x, lp['norm_mix']) * (1.0 + sc1) + sh1
    proj = h @ lp['w_in']
    u_s5, p_rwkv, p_ret, p_moba, gates = jnp.split(proj, IN_SPLITS, axis=-1)
    y_s5, s5_re_n, s5_im_n = s5_branch(u_s5, s5_re0, s5_im0, *lp['s5'])
    y_rwkv, rwkv_s_n, shift_n = rwkv_branch(p_rwkv, shift0, rwkv_s0, *lp['rwkv'])
    y_ret, ret_s_n = retention_branch(p_ret, ret_s0, pos0)
    y_moba, k_new, v_new = moba_branch(p_moba, pos0, k_past, v_past)
    br = jnp.stack([y_s5, y_rwkv, y_ret, y_moba], axis=2)
    up = jnp.einsum('btgc,gcd->btgd', br, lp['w_branch'])
    gate = jax.nn.sigmoid(gates.reshape(bsz, t_len, N_BRANCH, D_MODEL))
    mixed = jnp.sum(gate * up, axis=2) @ lp['w_out']
    x = x + gt1 * mixed
    h2 = _rmsnorm(x, lp['norm_ffn']) * (1.0 + sc2) + sh2
    kind, fw = lp['ffn']
    f = dense_ffn(h2, *fw) if kind == 'dense' else moe_ffn(h2, *fw)
    x = x + gt2 * f
    return x, (s5_re_n, s5_im_n, rwkv_s_n, shift_n, ret_s_n, k_new, v_new)


def setup_inputs(seed: int = 0) -> dict:
    key = jax.random.key(seed)
    ks = iter(jax.random.split(key, 64))
    f32 = jnp.float32

    def nrm(shape, scale):
        return jax.random.normal(next(ks), shape, f32) * scale

    n_pages = PAST_LEN // PAGE_SIZE
    n_pool = (DEC_BATCH * n_pages * 5) // 4
    d = D_MODEL
    x_prompt = nrm((BATCH, SEQ, d), 1.0)
    x_sample = nrm((DEC_BATCH, DEC_SEQ, d), 1.0)
    c_prompt = nrm((BATCH, d), 1.0)
    c_sample = nrm((DEC_BATCH, d), 1.0)
    state_s5_re = nrm((DEPTH, DEC_BATCH, S5_GROUPS, S5_STATE), 0.1)
    state_s5_im = nrm((DEPTH, DEC_BATCH, S5_GROUPS, S5_STATE), 0.1)
    state_rwkv = nrm((DEPTH, DEC_BATCH, RWKV_HEADS, HEAD_DIM, HEAD_DIM), 0.1)
    state_rwkv_shift = nrm((DEPTH, DEC_BATCH, RWKV_IN_W), 1.0)
    state_ret = nrm((DEPTH, DEC_BATCH, RET_HEADS, HEAD_DIM, HEAD_DIM), 0.3)
    cache_moba_k = nrm((DEPTH, n_pool, PAGE_SIZE, MOBA_HEADS, HEAD_DIM), 1.0)
    cache_moba_v = nrm((DEPTH, n_pool, PAGE_SIZE, MOBA_HEADS, HEAD_DIM), 1.0)
    page_table = jax.random.permutation(next(ks), n_pool)[: DEC_BATCH * n_pages].reshape(DEC_BATCH, n_pages).astype(jnp.int32)

    norm_mix_g = 1.0 + nrm((DEPTH, d), 0.01)
    norm_ffn_g = 1.0 + nrm((DEPTH, d), 0.01)
    norm_final_g = 1.0 + nrm((d,), 0.01)
    w_ada = nrm((DEPTH, d, 6 * d), 0.3 * d ** -0.5)
    b_ada = nrm((DEPTH, 6 * d), 0.01)
    w_in = nrm((DEPTH, d, IN_W), d ** -0.5)
    s5_lambda_re = -0.5 * jnp.exp(nrm((DEPTH, S5_GROUPS, S5_STATE), 0.05))
    s5_lambda_im = math.pi * jnp.arange(S5_STATE, dtype=f32)[None, None, :] + nrm((DEPTH, S5_GROUPS, S5_STATE), 0.05)
    s5_log_dt = jax.random.uniform(next(ks), (DEPTH, S5_GROUPS), f32, minval=math.log(S5_DT_MIN), maxval=math.log(S5_DT_MAX))
    s5_b_re = nrm((DEPTH, S5_GROUPS, S5_STATE, S5_CH_PER_GROUP), (2.0 * S5_CH_PER_GROUP) ** -0.5)
    s5_b_im = nrm((DEPTH, S5_GROUPS, S5_STATE, S5_CH_PER_GROUP), (2.0 * S5_CH_PER_GROUP) ** -0.5)
    s5_c_re = nrm((DEPTH, S5_GROUPS, S5_CH_PER_GROUP, S5_STATE), (2.0 * S5_STATE) ** -0.5)
    s5_c_im = nrm((DEPTH, S5_GROUPS, S5_CH_PER_GROUP, S5_STATE), (2.0 * S5_STATE) ** -0.5)
    s5_d = nrm((DEPTH, BRANCH_W), 1.0)
    s5_w_glu = nrm((DEPTH, BRANCH_W, BRANCH_W), BRANCH_W ** -0.5)
    rwkv_mu = jax.random.uniform(next(ks), (DEPTH, RWKV_IN_W), f32)
    ramp = (jnp.arange(BRANCH_W, dtype=f32) / (BRANCH_W - 1)) ** 0.85
    rwkv_w0 = (-7.0 + 5.0 * ramp + 0.5)[None, :] + nrm((DEPTH, BRANCH_W), 0.1)
    rwkv_w2 = nrm((DEPTH, RWKV_DECAY_LORA, BRANCH_W), 0.1 * RWKV_DECAY_LORA ** -0.5)
    rwkv_a0 = nrm((DEPTH, BRANCH_W), 0.1)
    rwkv_a2 = nrm((DEPTH, RWKV_AAA_LORA, BRANCH_W), 0.5 * RWKV_AAA_LORA ** -0.5)
    rwkv_g2 = nrm((DEPTH, RWKV_GATE_LORA, BRANCH_W), RWKV_GATE_LORA ** -0.5)
    rwkv_k_k = 0.85 + nrm((DEPTH, BRANCH_W), 0.05)
    rwkv_k_a = 1.0 + nrm((DEPTH, BRANCH_W), 0.05)
    rwkv_r_k = nrm((DEPTH, RWKV_HEADS, HEAD_DIM), 0.1)
    rwkv_ln_g = 1.0 + nrm((DEPTH, BRANCH_W), 0.01)
    rwkv_ln_b = nrm((DEPTH, BRANCH_W), 0.01)
    w_branch = nrm((DEPTH, N_BRANCH, BRANCH_W, d), BRANCH_W ** -0.5)
    w_out = nrm((DEPTH, d, d), d ** -0.5)
    ffn_w1 = nrm((N_DENSE, d, D_FF), d ** -0.5)
    ffn_w3 = nrm((N_DENSE, d, D_FF), d ** -0.5)
    ffn_w2 = nrm((N_DENSE, D_FF, d), D_FF ** -0.5)
    moe_router = nrm((N_MOE, d, N_EXPERTS), d ** -0.5)
    moe_w1 = nrm((N_MOE, N_EXPERTS, d, D_FF_EXPERT), d ** -0.5)
    moe_w3 = nrm((N_MOE, N_EXPERTS, d, D_FF_EXPERT), d ** -0.5)
    moe_w2 = nrm((N_MOE, N_EXPERTS, D_FF_EXPERT, d), D_FF_EXPERT ** -0.5)
    return {
        'x_prompt': x_prompt, 'x_sample': x_sample, 'c_prompt': c_prompt, 'c_sample': c_sample,
        'state_s5_re': state_s5_re, 'state_s5_im': state_s5_im, 'state_rwkv': state_rwkv,
        'state_rwkv_shift': state_rwkv_shift, 'state_ret': state_ret,
        'cache_moba_k': cache_moba_k, 'cache_moba_v': cache_moba_v, 'page_table': page_table,
        'norm_mix_g': norm_mix_g, 'norm_ffn_g': norm_ffn_g, 'norm_final_g': norm_final_g,
        'w_ada': w_ada, 'b_ada': b_ada, 'w_in': w_in,
        's5_lambda_re': s5_lambda_re, 's5_lambda_im': s5_lambda_im, 's5_log_dt': s5_log_dt,
        's5_b_re': s5_b_re, 's5_b_im': s5_b_im, 's5_c_re': s5_c_re, 's5_c_im': s5_c_im,
        's5_d': s5_d, 's5_w_glu': s5_w_glu,
        'rwkv_mu': rwkv_mu, 'rwkv_w0': rwkv_w0, 'rwkv_w2': rwkv_w2, 'rwkv_a0': rwkv_a0, 'rwkv_a2': rwkv_a2,
        'rwkv_g2': rwkv_g2, 'rwkv_k_k': rwkv_k_k, 'rwkv_k_a': rwkv_k_a, 'rwkv_r_k': rwkv_r_k,
        'rwkv_ln_g': rwkv_ln_g, 'rwkv_ln_b': rwkv_ln_b,
        'w_branch': w_branch, 'w_out': w_out,
        'ffn_w1': ffn_w1, 'ffn_w3': ffn_w3, 'ffn_w2': ffn_w2,
        'moe_router': moe_router, 'moe_w1': moe_w1, 'moe_w3': moe_w3, 'moe_w2': moe_w2,
    }


def reference(x_prompt, x_sample, c_prompt, c_sample, state_s5_re, state_s5_im, state_rwkv,
              state_rwkv_shift, state_ret, cache_moba_k, cache_moba_v, page_table,
              norm_mix_g, norm_ffn_g, norm_final_g, w_ada, b_ada, w_in,
              s5_lambda_re, s5_lambda_im, s5_log_dt, s5_b_re, s5_b_im, s5_c_re, s5_c_im, s5_d, s5_w_glu,
              rwkv_mu, rwkv_w0, rwkv_w2, rwkv_a0, rwkv_a2, rwkv_g2, rwkv_k_k, rwkv_k_a, rwkv_r_k,
              rwkv_ln_g, rwkv_ln_b, w_branch, w_out, ffn_w1, ffn_w3, ffn_w2,
              moe_router, moe_w1, moe_w3, moe_w2):
    f32 = jnp.float32
    bp = x_prompt.shape[0]
    bs = x_sample.shape[0]
    z_s5 = jnp.zeros((bp, S5_GROUPS, S5_STATE), f32)
    z_rwkv = jnp.zeros((bp, RWKV_HEADS, HEAD_DIM, HEAD_DIM), f32)
    z_shift = jnp.zeros((bp, RWKV_IN_W), x_prompt.dtype)
    z_ret = jnp.zeros((bp, RET_HEADS, HEAD_DIM, HEAD_DIM), f32)
    xp, xs = x_prompt, x_sample
    outs_p = [[] for _ in range(7)]
    outs_s = [[] for _ in range(7)]
    for l in range(DEPTH):
        if l % 2 == 0:
            ffn = ('dense', (ffn_w1[l // 2], ffn_w3[l // 2], ffn_w2[l // 2]))
        else:
            ffn = ('moe', (moe_router[l // 2], moe_w1[l // 2], moe_w3[l // 2], moe_w2[l // 2]))
        lp = {
            'norm_mix': norm_mix_g[l], 'norm_ffn': norm_ffn_g[l], 'w_ada': w_ada[l], 'b_ada': b_ada[l],
            'w_in': w_in[l],
            's5': (s5_lambda_re[l], s5_lambda_im[l], s5_log_dt[l], s5_b_re[l], s5_b_im[l],
                   s5_c_re[l], s5_c_im[l], s5_d[l], s5_w_glu[l]),
            'rwkv': (rwkv_mu[l], rwkv_w0[l], rwkv_w2[l], rwkv_a0[l], rwkv_a2[l], rwkv_g2[l],
                     rwkv_k_k[l], rwkv_k_a[l], rwkv_r_k[l], rwkv_ln_g[l], rwkv_ln_b[l]),
            'w_branch': w_branch[l], 'w_out': w_out[l], 'ffn': ffn,
        }
        xp, st_p = layer(xp, c_prompt, 0, z_s5, z_s5, z_rwkv, z_shift, z_ret, None, None, lp)
        k_past = cache_moba_k[l][page_table].reshape(bs, -1, MOBA_HEADS, HEAD_DIM)
        v_past = cache_moba_v[l][page_table].reshape(bs, -1, MOBA_HEADS, HEAD_DIM)
        xs, st_s = layer(xs, c_sample, PAST_LEN, state_s5_re[l], state_s5_im[l], state_rwkv[l],
                         state_rwkv_shift[l], state_ret[l], k_past, v_past, lp)
        for j in range(7):
            outs_p[j].append(st_p[j])
            outs_s[j].append(st_s[j])
    y_prompt = _rmsnorm(xp, norm_final_g)
    y_sample = _rmsnorm(xs, norm_final_g)
    p_s5_re, p_s5_im, p_rwkv, p_shift, p_ret, p_k, p_v = [jnp.stack(o, axis=0) for o in outs_p]
    s_s5_re, s_s5_im, s_rwkv, s_shift, s_ret, s_k, s_v = [jnp.stack(o, axis=0) for o in outs_s]
    return (y_prompt, y_sample, p_s5_re, p_s5_im, p_rwkv, p_shift, p_ret, p_k, p_v,
            s_s5_re, s_s5_im, s_rwkv, s_shift, s_ret, s_k, s_v)
```

```python
import functools
import math

import jax
import jax.numpy as jnp
from jax import lax
from jax.experimental import pallas as pl
from jax.experimental.pallas import tpu as pltpu

F32 = jnp.float32
BF16 = jnp.bfloat16
HIGHEST = lax.Precision.HIGHEST

D_MODEL = 1024
BRANCH_W = 256
HEAD_DIM = 64
N_HEADS = 4
N_BRANCH = 4
S5_GROUPS = 16
S5_STATE = 64
S5_CH = 16
S5_W = S5_GROUPS * S5_STATE
IN_W = 7168
RWKV_IN_W = 1024
RWKV_LN_EPS = 64e-5
RMS_EPS = 1e-6
ROPE_BASE = 10000.0
MOBA_BLOCK = 256
MOBA_TOPK = 3
N_EXPERTS = 8
ROUTER_PAD = 128
NEG_INF = float("-inf")

COL_S5 = 0
COL_RWKV = 1
COL_RET = 5
COL_MOBA = 9
COL_GATE = 12

VMEM_LIMIT = 48 * 1024 * 1024


def _params(sem):
    return pltpu.CompilerParams(dimension_semantics=sem, vmem_limit_bytes=VMEM_LIMIT)


def _dot(a, b):
    return jnp.dot(a.astype(BF16), b.astype(BF16), preferred_element_type=F32)


def _dot_hi(a, b):
    return jnp.dot(a, b, precision=HIGHEST, preferred_element_type=F32)


def _dot_nt(a, b):
    return lax.dot_general(a.astype(BF16), b.astype(BF16), (((1,), (1,)), ((), ())),
                           preferred_element_type=F32)


def _dot_nt_hi(a, b):
    return lax.dot_general(a, b, (((1,), (1,)), ((), ())), precision=HIGHEST,
                           preferred_element_type=F32)


def _dot_tn(a, b):
    return lax.dot_general(a.astype(BF16), b.astype(BF16), (((0,), (0,)), ((), ())),
                           preferred_element_type=F32)


def _head_masks(width=BRANCH_W):
    lane = lax.broadcasted_iota(jnp.int32, (1, width), 1)
    return [(lane // HEAD_DIM == h).astype(F32) for h in range(N_HEADS)]


def _head_ones():
    r = lax.broadcasted_iota(jnp.int32, (BRANCH_W, BRANCH_W), 0) // HEAD_DIM
    c = lax.broadcasted_iota(jnp.int32, (BRANCH_W, BRANCH_W), 1) // HEAD_DIM
    return (r == c).astype(F32)


def _head_sum(x, ones_bd):
    return _dot_hi(x, ones_bd)


def _ada_kernel(c_ref, w_ref, b_ref, o_ref):
    c = c_ref[...]
    h = c * jax.nn.sigmoid(c)
    o_ref[0] = _dot(h, w_ref[0]) + b_ref[0]


def _ada(c_all, w_ada, b_ada):
    depth, _, width = w_ada.shape
    rows = c_all.shape[0]
    tn = 1024
    return pl.pallas_call(
        _ada_kernel,
        grid=(depth, width // tn),
        in_specs=[pl.BlockSpec((rows, D_MODEL), lambda l, j: (0, 0)),
                  pl.BlockSpec((1, D_MODEL, tn), lambda l, j: (l, 0, j)),
                  pl.BlockSpec((1, 1, tn), lambda l, j: (l, 0, j))],
        out_specs=pl.BlockSpec((1, rows, tn), lambda l, j: (l, 0, j)),
        out_shape=jax.ShapeDtypeStruct((depth, rows, width), F32),
        compiler_params=_params(("arbitrary", "arbitrary")),
        name="ada_mod",
    )(c_all, w_ada, b_ada.reshape(depth, 1, width))


def _modulated_norm(x, g, sc, sh):
    y = x * lax.rsqrt(jnp.mean(x * x, axis=-1, keepdims=True) + RMS_EPS) * g
    return y * (1.0 + sc) + sh


def _inproj_kernel(x_ref, g_ref, sc_ref, sh_ref, w_ref, o_ref, *rest, emit_tb):
    h_scr = rest[-1]
    j = pl.program_id(1)

    @pl.when(j == 0)
    def _():
        h_scr[...] = _modulated_norm(x_ref[...], g_ref[...], sc_ref[0], sh_ref[0]).astype(BF16)

    acc = jnp.dot(h_scr[...], w_ref[...].astype(BF16), preferred_element_type=F32)
    o_ref[...] = acc
    if emit_tb:
        u_ref = rest[0]

        @pl.when(j == 0)
        def _():
            u_ref[...] = acc[:, :BRANCH_W]


def _inproj(x2d, g, mod3, w, *, tm, tiles_per_group, tb_shape=None):
    rows = x2d.shape[0]
    m = mod3.shape[1]
    tn = 512
    tpg = tiles_per_group
    in_specs = [pl.BlockSpec((tm, D_MODEL), lambda i, j: (i, 0)),
                pl.BlockSpec((1, D_MODEL), lambda i, j: (0, 0)),
                pl.BlockSpec((1, m, D_MODEL), lambda i, j: (i // tpg, 0, 1)),
                pl.BlockSpec((1, m, D_MODEL), lambda i, j: (i // tpg, 0, 0)),
                pl.BlockSpec((D_MODEL, tn), lambda i, j: (0, j))]
    out_specs = [pl.BlockSpec((tm, tn), lambda i, j: (i, j))]
    out_shape = [jax.ShapeDtypeStruct((rows, IN_W), F32)]
    if tb_shape is not None:
        out_specs.append(pl.BlockSpec((tm, BRANCH_W), lambda i, j: (i % tpg, i // tpg)))
        out_shape.append(jax.ShapeDtypeStruct(tb_shape, F32))
    return pl.pallas_call(
        functools.partial(_inproj_kernel, emit_tb=tb_shape is not None),
        grid=(rows // tm, IN_W // tn),
        in_specs=in_specs, out_specs=out_specs, out_shape=out_shape,
        scratch_shapes=[pltpu.VMEM((tm, D_MODEL), BF16)],
        compiler_params=_params(("arbitrary", "arbitrary")),
        name="inproj",
    )(x2d, g.reshape(1, D_MODEL), mod3, mod3, w)


def _s5_kernel(u_ref, x0r_ref, x0i_ref, lbr_ref, lbi_ref, br_ref, bi_ref, cr_ref, ci_ref, d_ref, wg_ref,
               y_ref, sr_ref, si_ref, bur, bui, xr, xi, *, nb, tc):
    c = pl.program_id(0)

    @pl.when(c == 0)
    def _():
        xr[...] = x0r_ref[...]
        xi[...] = x0i_ref[...]

    u = u_ref[...]
    ub = u.astype(BF16)
    bur[...] = jnp.dot(ub, br_ref[...].astype(BF16), preferred_element_type=F32)
    bui[...] = jnp.dot(ub, bi_ref[...].astype(BF16), preferred_element_type=F32)
    lbr = jnp.broadcast_to(lbr_ref[...], (nb, S5_W))
    lbi = jnp.broadcast_to(lbi_ref[...], (nb, S5_W))

    def body(t, carry):
        sr, si = carry
        rows = pl.ds(pl.multiple_of(t * nb, nb), nb)
        nr = lbr * sr - lbi * si + bur[rows, :]
        ni = lbr * si + lbi * sr + bui[rows, :]
        bur[rows, :] = nr
        bui[rows, :] = ni
        return nr, ni

    sr, si = lax.fori_loop(0, tc, body, (xr[...], xi[...]))
    xr[...] = sr
    xi[...] = si
    sr_ref[...] = sr
    si_ref[...] = si
    y = _dot(bur[...], cr_ref[...]) - _dot(bui[...], ci_ref[...]) + d_ref[...] * u
    z = jax.nn.gelu(y)
    y_ref[...] = z * jax.nn.sigmoid(_dot(z, wg_ref[...]))


def _s5_params(lam_re, lam_im, log_dt, b_re, b_im, c_re, c_im):
    dt = jnp.exp(log_dt)[:, None]
    mag = jnp.exp(lam_re * dt)
    lb_re = mag * jnp.cos(lam_im * dt)
    lb_im = mag * jnp.sin(lam_im * dt)
    den = lam_re * lam_re + lam_im * lam_im
    q_re = ((lb_re - 1.0) * lam_re + lb_im * lam_im) / den
    q_im = (lb_im * lam_re - (lb_re - 1.0) * lam_im) / den
    bb_re = q_re[..., None] * b_re - q_im[..., None] * b_im
    bb_im = q_re[..., None] * b_im + q_im[..., None] * b_re
    eye = jnp.eye(S5_GROUPS, dtype=F32)
    to_in = lambda bb: jnp.einsum('gnc,gh->gchn', bb, eye).reshape(BRANCH_W, S5_W)
    to_out = lambda cc: jnp.einsum('gcn,gh->gnhc', cc, eye).reshape(S5_W, BRANCH_W)
    return (lb_re.reshape(1, S5_W), lb_im.reshape(1, S5_W), to_in(bb_re), to_in(bb_im),
            to_out(c_re), to_out(c_im))


def _s5(u_tb, x0_re, x0_im, s5p, d_skip, w_glu, *, nb, t_len, tc):
    lb_re, lb_im, bin_re, bin_im, cout_re, cout_im = s5p
    rows = tc * nb
    const = lambda shape: pl.BlockSpec(shape, lambda c: (0,) * len(shape))
    return pl.pallas_call(
        functools.partial(_s5_kernel, nb=nb, tc=tc),
        grid=(t_len // tc,),
        in_specs=[pl.BlockSpec((rows, BRANCH_W), lambda c: (c, 0)),
                  const((nb, S5_W)), const((nb, S5_W)), const((1, S5_W)), const((1, S5_W)),
                  const((BRANCH_W, S5_W)), const((BRANCH_W, S5_W)),
                  const((S5_W, BRANCH_W)), const((S5_W, BRANCH_W)),
                  const((1, BRANCH_W)), const((BRANCH_W, BRANCH_W))],
        out_specs=[pl.BlockSpec((rows, BRANCH_W), lambda c: (c, 0)), const((nb, S5_W)), const((nb, S5_W))],
        out_shape=[jax.ShapeDtypeStruct((t_len * nb, BRANCH_W), F32),
                   jax.ShapeDtypeStruct((nb, S5_W), F32), jax.ShapeDtypeStruct((nb, S5_W), F32)],
        scratch_shapes=[pltpu.VMEM((rows, S5_W), F32), pltpu.VMEM((rows, S5_W), F32),
                        pltpu.VMEM((nb, S5_W), F32), pltpu.VMEM((nb, S5_W), F32)],
        compiler_params=_params(("arbitrary",)),
        name="s5",
    )(u_tb, x0_re, x0_im, lb_re, lb_im, bin_re, bin_im, cout_re, cout_im,
      d_skip.reshape(1, BRANCH_W), w_glu)


def _rwkv_features(pm_r, pm_k, pm_v, pm_l, prm, ones_bd):
    w0, w2p, a0, a2p, g2p, k_k, k_a = prm
    w_raw = w0 + _dot(jnp.tanh(pm_l), w2p)
    lw = -jax.nn.sigmoid(w_raw) * math.exp(-0.5)
    a_sig = jax.nn.sigmoid(a0 + _dot(pm_l, a2p))
    g = _dot(jax.nn.sigmoid(pm_l), g2p)
    kk = pm_k * k_k
    kk = kk * lax.rsqrt(_head_sum(kk * kk, ones_bd) + 1e-12)
    k_mod = pm_k * (1.0 + (a_sig - 1.0) * k_a)
    return pm_r, k_mod, pm_v, lw, -kk, kk * a_sig, g


def _rwkv_post(y, r, k_mod, v, g, r_k, ln_g, ln_b, ones_bd):
    inv = 1.0 / HEAD_DIM
    mean = _head_sum(y, ones_bd) * inv
    yc = y - mean
    var = _head_sum(yc * yc, ones_bd) * inv
    yn = yc * lax.rsqrt(var + RWKV_LN_EPS) * ln_g + ln_b
    bonus = _head_sum(r * k_mod * r_k, ones_bd) * v
    return (yn + bonus) * g


def _rwkv_kernel(r_ref, k_ref, v_ref, l_ref, shift_ref, s0_ref, mu_ref, w0_ref, w2_ref, a0_ref, a2_ref,
                 g2_ref, kk_ref, ka_ref, rk_ref, lng_ref, lnb_ref,
                 y_ref, s_ref, sh_ref, prev_scr, s_scr, *, chunk):
    c = pl.program_id(1)
    n = chunk

    @pl.when(c == 0)
    def _():
        prev_scr[...] = shift_ref[0]
        s_scr[...] = s0_ref[0]

    row = lax.broadcasted_iota(jnp.int32, (n, BRANCH_W), 0)
    pieces = []
    for idx, ref in enumerate((r_ref, k_ref, v_ref, l_ref)):
        lanes = slice(idx * BRANCH_W, (idx + 1) * BRANCH_W)
        x = ref[0]
        x_prev = jnp.where(row == 0, prev_scr[:, lanes], pltpu.roll(x, 1, 0))
        pieces.append(x + (x_prev - x) * mu_ref[:, lanes])
        prev_scr[:, lanes] = x[n - 1:n, :]
        sh_ref[0, :, lanes] = x[n - 1:n, :]

    ones_bd = _head_ones()
    prm = (w0_ref[...], w2_ref[...], a0_ref[...], a2_ref[...], g2_ref[...], kk_ref[...], ka_ref[...])
    r, k_mod, v, lw, aa, bb, g = _rwkv_features(*pieces, prm, ones_bd)

    ti = lax.broadcasted_iota(jnp.int32, (n, n), 0)
    tj = lax.broadcasted_iota(jnp.int32, (n, n), 1)
    low_incl = (ti >= tj).astype(F32)
    low_strict = (ti > tj).astype(F32)
    eye = (ti == tj).astype(F32)
    cum = _dot_hi(low_incl, lw)
    tot = cum[n - 1:n, :]
    a_t = aa * jnp.exp(cum - lw)
    r_t = r * jnp.exp(cum)
    e_neg = jnp.exp(-cum)
    b_t = bb * e_neg
    k_t = k_mod * e_neg
    e_rem = jnp.exp(tot - cum)
    b_h = bb * e_rem
    k_h = k_mod * e_rem
    s = s_scr[...]
    u_all = jnp.zeros((n, BRANCH_W), F32)
    y_all = jnp.zeros((n, BRANCH_W), F32)
    n_double = max(int(math.log2(n)) - 1, 0)
    for mh in _head_masks():
        a_m = a_t * mh
        r_m = r_t * mh
        nmat = _dot_nt_hi(a_m, b_t) * low_strict
        mmat = _dot_nt_hi(a_m, k_t) * low_strict
        lb = _dot_nt_hi(r_m, b_t) * low_incl
        lk = _dot_nt_hi(r_m, k_t) * low_incl
        tinv = eye + nmat
        pw = nmat
        for _ in range(n_double):
            pw = _dot_hi(pw, pw)
            tinv = tinv + _dot_hi(pw, tinv)
        rhs = _dot_nt_hi(a_m, s) + _dot_hi(mmat, v) * mh
        u_h = _dot_hi(tinv, rhs)
        y_h = _dot_nt_hi(r_m, s) + _dot_hi(lb, u_h) + _dot_hi(lk, v) * mh
        u_all = u_all + u_h
        y_all = y_all + y_h
    upd = lax.dot_general(u_all, b_h, (((0,), (0,)), ((), ())), precision=HIGHEST, preferred_element_type=F32)
    upd = upd + lax.dot_general(v, k_h, (((0,), (0,)), ((), ())), precision=HIGHEST, preferred_element_type=F32)
    s_new = s * jnp.exp(tot) + ones_bd * upd
    s_scr[...] = s_new
    s_ref[0] = s_new
    y_ref[0] = _rwkv_post(y_all, r, k_mod, v, g, rk_ref[...], lng_ref[...], lnb_ref[...], ones_bd)


def _rwkv_params(mu, w0, w2, a0, a2, g2, k_k, k_a, r_k, ln_g, ln_b):
    row = lambda t: t.reshape(1, -1)
    pad = lambda w, lo: jnp.zeros((BRANCH_W, BRANCH_W), F32).at[lo:lo + w.shape[0]].set(w)
    return (row(mu), row(w0), pad(w2, 0), row(a0), pad(a2, 64), pad(g2, 128), row(k_k), row(k_a),
            row(r_k), row(ln_g), row(ln_b))


def _block_diag_state(s):
    bsz = s.shape[0]
    eye = jnp.eye(N_HEADS, dtype=F32)
    return jnp.einsum('bhij,hg->bhigj', s, eye).reshape(bsz, BRANCH_W, BRANCH_W)


def _diag_blocks(s_bd):
    bsz = s_bd.shape[0]
    s5 = s_bd.reshape(bsz, N_HEADS, HEAD_DIM, N_HEADS, HEAD_DIM)
    return jnp.stack([s5[:, h, :, h, :] for h in range(N_HEADS)], axis=1)


def _rwkv(proj3, shift0, s0, rp, *, chunk):
    bsz, t_len, _ = proj3.shape
    col = lambda k: pl.BlockSpec((1, chunk, BRANCH_W), lambda b, c: (b, c, COL_RWKV + k))
    const = lambda shape: pl.BlockSpec(shape, lambda b, c: (0,) * len(shape))
    vec = const((1, BRANCH_W))
    mat = const((BRANCH_W, BRANCH_W))
    y, s_bd, shift_n = pl.pallas_call(
        functools.partial(_rwkv_kernel, chunk=chunk),
        grid=(bsz, t_len // chunk),
        in_specs=[col(0), col(1), col(2), col(3),
                  pl.BlockSpec((1, 1, RWKV_IN_W), lambda b, c: (b, 0, 0)),
                  pl.BlockSpec((1, BRANCH_W, BRANCH_W), lambda b, c: (b, 0, 0)),
                  const((1, RWKV_IN_W)), vec, mat, vec, mat, mat, vec, vec, vec, vec, vec],
        out_specs=[pl.BlockSpec((1, chunk, BRANCH_W), lambda b, c: (b, c, 0)),
                   pl.BlockSpec((1, BRANCH_W, BRANCH_W), lambda b, c: (b, 0, 0)),
                   pl.BlockSpec((1, 1, RWKV_IN_W), lambda b, c: (b, 0, 0))],
        out_shape=[jax.ShapeDtypeStruct((bsz, t_len, BRANCH_W), F32),
                   jax.ShapeDtypeStruct((bsz, BRANCH_W, BRANCH_W), F32),
                   jax.ShapeDtypeStruct((bsz, 1, RWKV_IN_W), F32)],
        scratch_shapes=[pltpu.VMEM((1, RWKV_IN_W), F32), pltpu.VMEM((BRANCH_W, BRANCH_W), F32)],
        compiler_params=_params(("arbitrary", "arbitrary")),
        name="rwkv",
    )(proj3, proj3, proj3, proj3, shift0.reshape(bsz, 1, RWKV_IN_W), _block_diag_state(s0), *rp)
    return y, _diag_blocks(s_bd), shift_n.reshape(bsz, RWKV_IN_W)


def _rope_tables(t_len, pos0):
    half = HEAD_DIM // 2
    freqs = 1.0 / (ROPE_BASE ** jnp.linspace(0.0, 1.0, half, dtype=F32))
    pos = jnp.arange(t_len, dtype=F32) + pos0
    ang = pos[:, None] * freqs[None, :]
    cos = jnp.cos(ang)
    sin = jnp.sin(ang)
    cos_t = jnp.tile(jnp.concatenate([cos, cos], axis=-1), (1, N_HEADS))
    sin_t = jnp.tile(jnp.concatenate([-sin, sin], axis=-1), (1, N_HEADS))
    return cos_t, sin_t


def _rope(x, cos_t, sin_t):
    lane = lax.broadcasted_iota(jnp.int32, x.shape, 1)
    first = (lane % HEAD_DIM) < (HEAD_DIM // 2)
    swapped = jnp.where(first, pltpu.roll(x, BRANCH_W - HEAD_DIM // 2, 1), pltpu.roll(x, HEAD_DIM // 2, 1))
    return x * cos_t + swapped * sin_t


def _ret_tables(chunk):
    log_gamma = jnp.log(1.0 - jnp.exp2(-5.0 - jnp.arange(N_HEADS, dtype=F32)))
    i = jnp.arange(chunk, dtype=F32)
    diff = i[:, None] - i[None, :]
    dmask = jnp.where(diff >= 0, jnp.exp(log_gamma[:, None, None] * jnp.maximum(diff, 0.0)), 0.0)
    lanes = lambda t: jnp.repeat(t, HEAD_DIM, axis=-1)
    xi = lanes(jnp.exp(log_gamma[None, :] * (i[:, None] + 1.0)))
    zeta = lanes(jnp.exp(log_gamma[None, :] * (chunk - 1.0 - i[:, None])))
    g_chunk = lanes(jnp.exp(log_gamma * chunk)[None, :])
    return dmask, xi, zeta, g_chunk


def _ret_kernel(q_ref, k_ref, v_ref, g_ref, s0_ref, cos_ref, sin_ref, dm_ref, xi_ref, zeta_ref, gch_ref,
                y_ref, s_ref, s_scr):
    c = pl.program_id(1)

    @pl.when(c == 0)
    def _():
        s_scr[...] = s0_ref[0]

    cos_t = cos_ref[...]
    sin_t = sin_ref[...]
    q = _rope(q_ref[0], cos_t, sin_t)
    k = _rope(k_ref[0], cos_t, sin_t) * (HEAD_DIM ** -0.5)
    v = v_ref[0]
    s = s_scr[...]
    ones_bd = _head_ones()
    o = _dot(q, s) * xi_ref[...]
    for h, mh in enumerate(_head_masks()):
        att = _dot_nt(q * mh, k) * dm_ref[h]
        o = o + _dot(att, v) * mh
    s_new = s * gch_ref[...] + ones_bd * _dot_tn(k * zeta_ref[...], v)
    s_scr[...] = s_new
    s_ref[0] = s_new
    o = o * lax.rsqrt(_head_sum(o * o, ones_bd) * (1.0 / HEAD_DIM) + RMS_EPS)
    g = g_ref[0]
    y_ref[0] = o * (g * jax.nn.sigmoid(g))


def _ret(proj3, s0, pos0, *, chunk):
    bsz, t_len, _ = proj3.shape
    cos_t, sin_t = _rope_tables(t_len, pos0)
    dmask, xi, zeta, g_chunk = _ret_tables(chunk)
    col = lambda k: pl.BlockSpec((1, chunk, BRANCH_W), lambda b, c: (b, c, COL_RET + k))
    const = lambda shape: pl.BlockSpec(shape, lambda b, c: (0,) * len(shape))
    tab = pl.BlockSpec((chunk, BRANCH_W), lambda b, c: (c, 0))
    y, s_bd = pl.pallas_call(
        _ret_kernel,
        grid=(bsz, t_len // chunk),
        in_specs=[col(0), col(1), col(2), col(3),
                  pl.BlockSpec((1, BRANCH_W, BRANCH_W), lambda b, c: (b, 0, 0)),
                  tab, tab, const((N_HEADS, chunk, chunk)), const((chunk, BRANCH_W)),
                  const((chunk, BRANCH_W)), const((1, BRANCH_W))],
        out_specs=[pl.BlockSpec((1, chunk, BRANCH_W), lambda b, c: (b, c, 0)),
                   pl.BlockSpec((1, BRANCH_W, BRANCH_W), lambda b, c: (b, 0, 0))],
        out_shape=[jax.ShapeDtypeStruct((bsz, t_len, BRANCH_W), F32),
                   jax.ShapeDtypeStruct((bsz, BRANCH_W, BRANCH_W), F32)],
        scratch_shapes=[pltpu.VMEM((BRANCH_W, BRANCH_W), F32)],
        compiler_params=_params(("arbitrary", "arbitrary")),
        name="retention",
    )(proj3, proj3, proj3, proj3, _block_diag_state(s0), cos_t, sin_t, dmask, xi, zeta, g_chunk)
    return y, _diag_blocks(s_bd)


def _topk_select(gs, n_valid, blk):
    nblk = gs.shape[1]
    gsm = jnp.where(blk < n_valid, gs, NEG_INF)
    cnt = jnp.zeros(gs.shape, F32)
    for m in range(nblk):
        col = gsm[:, m:m + 1]
        beats = (col > gsm) | ((col == gsm) & (m < blk))
        cnt = cnt + beats.astype(F32)
    return ((blk < n_valid) & (cnt < MOBA_TOPK)).astype(F32)


def _moba_kernel(q_ref, k_ref, v_ref, o_ref, km_scr, *, nblk):
    qi = pl.program_id(1)
    bs = MOBA_BLOCK

    @pl.when(qi == 0)
    def _():
        for n in range(nblk):
            km_scr[n:n + 1, :] = jnp.mean(k_ref[0, n * bs:(n + 1) * bs, :], axis=0, keepdims=True)

    q = q_ref[0]
    km = km_scr[...]
    scale = HEAD_DIM ** -0.5
    blk = lax.broadcasted_iota(jnp.int32, (1, nblk), 1)
    ri = lax.broadcasted_iota(jnp.int32, (bs, bs), 0)
    ci = lax.broadcasted_iota(jnp.int32, (bs, bs), 1)
    own = pl.ds(pl.multiple_of(qi * bs, bs), bs)
    k_own = k_ref[0, own, :]
    v_own = v_ref[0, own, :]
    out = jnp.zeros((bs, BRANCH_W), F32)
    for mh in _head_masks():
        qh = q * mh
        sel = _topk_select(_dot_nt_hi(qh, km), qi, blk)
        s = jnp.where(ci <= ri, _dot_nt(qh, k_own) * scale, NEG_INF)
        m0 = jnp.max(s, axis=1, keepdims=True)
        p = jnp.exp(s - m0)
        l0 = jnp.sum(p, axis=1, keepdims=True)
        acc0 = _dot(p, v_own)

        def body(n, carry, qh=qh, sel=sel):
            m, l, acc = carry
            rows = pl.ds(pl.multiple_of(n * bs, bs), bs)
            seln = jnp.sum(jnp.where(blk == n, sel, 0.0), axis=1, keepdims=True)
            s = jnp.where(seln > 0.0, _dot_nt(qh, k_ref[0, rows, :]) * scale, NEG_INF)
            m_new = jnp.maximum(m, jnp.max(s, axis=1, keepdims=True))
            alpha = jnp.exp(m - m_new)
            p = jnp.exp(s - m_new)
            l = alpha * l + jnp.sum(p, axis=1, keepdims=True)
            acc = alpha * acc + _dot(p, v_ref[0, rows, :])
            return m_new, l, acc

        _, l, acc = lax.fori_loop(0, qi, body, (m0, l0, acc0))
        out = out + mh * (acc / l)
    o_ref[0] = out


def _moba_prompt(proj3):
    bsz, t_len, _ = proj3.shape
    nblk = t_len // MOBA_BLOCK
    full = lambda k: pl.BlockSpec((1, t_len, BRANCH_W), lambda b, i: (b, 0, COL_MOBA + k))
    return pl.pallas_call(
        functools.partial(_moba_kernel, nblk=nblk),
        grid=(bsz, nblk),
        in_specs=[pl.BlockSpec((1, MOBA_BLOCK, BRANCH_W), lambda b, i: (b, i, COL_MOBA)), full(1), full(2)],
        out_specs=pl.BlockSpec((1, MOBA_BLOCK, BRANCH_W), lambda b, i: (b, i, 0)),
        out_shape=jax.ShapeDtypeStruct((bsz, t_len, BRANCH_W), F32),
        scratch_shapes=[pltpu.VMEM((nblk, BRANCH_W), F32)],
        compiler_params=_params(("arbitrary", "arbitrary")),
        name="moba_prompt",
    )(proj3, proj3, proj3)


def _merge_kernel(x_ref, y0_ref, y1_ref, y2_ref, y3_ref, g0_ref, g1_ref, g2_ref, g3_ref, gt_ref,
                  wb_ref, wo_ref, o_ref, wb_scr, wo_scr):
    @pl.when(pl.program_id(0) == 0)
    def _():
        wb_scr[...] = wb_ref[...].astype(BF16)
        wo_scr[...] = wo_ref[...].astype(BF16)

    mixed = None
    for g, (y_ref, g_ref) in enumerate(((y0_ref, g0_ref), (y1_ref, g1_ref), (y2_ref, g2_ref), (y3_ref, g3_ref))):
        up = jnp.dot(y_ref[...].astype(BF16), wb_scr[g], preferred_element_type=F32)
        term = jax.nn.sigmoid(g_ref[...]) * up
        mixed = term if mixed is None else mixed + term
    o_ref[...] = x_ref[...] + gt_ref[0] * jnp.dot(mixed.astype(BF16), wo_scr[...], preferred_element_type=F32)


def _merge(x2d, y_s5, s5_spec, y_rwkv, y_ret, y_moba, proj, mod3, w_branch, w_out, *, tm, tiles_per_group):
    rows = x2d.shape[0]
    m = mod3.shape[1]
    tpg = tiles_per_group
    ysp = pl.BlockSpec((tm, BRANCH_W), lambda i: (i, 0))
    gate = lambda g: pl.BlockSpec((tm, D_MODEL), lambda i: (i, COL_GATE // N_BRANCH + g))
    return pl.pallas_call(
        _merge_kernel,
        grid=(rows // tm,),
        in_specs=[pl.BlockSpec((tm, D_MODEL), lambda i: (i, 0)), s5_spec, ysp, ysp, ysp,
                  gate(0), gate(1), gate(2), gate(3),
                  pl.BlockSpec((1, m, D_MODEL), lambda i: (i // tpg, 0, 2)),
                  pl.BlockSpec((N_BRANCH, BRANCH_W, D_MODEL), lambda i: (0, 0, 0)),
                  pl.BlockSpec((D_MODEL, D_MODEL), lambda i: (0, 0))],
        out_specs=pl.BlockSpec((tm, D_MODEL), lambda i: (i, 0)),
        out_shape=jax.ShapeDtypeStruct((rows, D_MODEL), F32),
        scratch_shapes=[pltpu.VMEM((N_BRANCH, BRANCH_W, D_MODEL), BF16), pltpu.VMEM((D_MODEL, D_MODEL), BF16)],
        compiler_params=_params(("arbitrary",)),
        name="merge",
    )(x2d, y_s5, y_rwkv, y_ret, y_moba, proj, proj, proj, proj, mod3, w_branch, w_out)


def _router_combine(h, router):
    logits = _dot_hi(h, router)
    lane = lax.broadcasted_iota(jnp.int32, logits.shape, 1)
    logits = jnp.where(lane < N_EXPERTS, logits, NEG_INF)
    m1 = jnp.max(logits, axis=1, keepdims=True)
    i1 = jnp.min(jnp.where(logits == m1, lane, ROUTER_PAD), axis=1, keepdims=True)
    rest = jnp.where(lane == i1, NEG_INF, logits)
    m2 = jnp.max(rest, axis=1, keepdims=True)
    i2 = jnp.min(jnp.where(rest == m2, lane, ROUTER_PAD), axis=1, keepdims=True)
    e2 = jnp.exp(m2 - m1)
    den = 1.0 + e2
    return jnp.where(lane == i1, 1.0 / den, 0.0) + jnp.where(lane == i2, e2 / den, 0.0)


def _ffn_kernel(*refs, moe, n_e, n_j):
    if moe:
        x_ref, g_ref, sc_ref, sh_ref, gt_ref, rt_ref, w1_ref, w3_ref, w2_ref, o_ref, h_scr, acc_scr, comb_scr = refs
    else:
        x_ref, g_ref, sc_ref, sh_ref, gt_ref, w1_ref, w3_ref, w2_ref, o_ref, h_scr, acc_scr = refs
    e = pl.program_id(1)
    j = pl.program_id(2)

    @pl.when((e == 0) & (j == 0))
    def _():
        h = _modulated_norm(x_ref[...], g_ref[...], sc_ref[0], sh_ref[0])
        h_scr[...] = h.astype(BF16)
        acc_scr[...] = jnp.zeros(acc_scr.shape, F32)
        if moe:
            comb_scr[...] = _router_combine(h, rt_ref[...])

    hb = h_scr[...]
    a = jnp.dot(hb, w1_ref[0].astype(BF16), preferred_element_type=F32)
    b = jnp.dot(hb, w3_ref[0].astype(BF16), preferred_element_type=F32)
    act = (a * jax.nn.sigmoid(a)) * b
    part = jnp.dot(act.astype(BF16), w2_ref[0].astype(BF16), preferred_element_type=F32)
    if moe:
        comb = comb_scr[...]
        lane = lax.broadcasted_iota(jnp.int32, comb.shape, 1)
        part = jnp.sum(jnp.where(lane == e, comb, 0.0), axis=1, keepdims=True) * part
    acc_scr[...] += part

    @pl.when((e == n_e - 1) & (j == n_j - 1))
    def _():
        o_ref[...] = x_ref[...] + gt_ref[0] * acc_scr[...]


def _ffn(x2d, g, mod3, weights, *, tm, tiles_per_group, tf):
    rows = x2d.shape[0]
    m = mod3.shape[1]
    tpg = tiles_per_group
    moe = len(weights) == 4
    if moe:
        router, w1, w3, w2 = weights
        router = jnp.zeros((D_MODEL, ROUTER_PAD), F32).at[:, :N_EXPERTS].set(router)
    else:
        w1, w3, w2 = (w[None] for w in weights)
    n_e, _, d_ff = w1.shape
    n_j = d_ff // tf
    modspec = lambda k: pl.BlockSpec((1, m, D_MODEL), lambda i, e, j: (i // tpg, 0, k))
    in_specs = [pl.BlockSpec((tm, D_MODEL), lambda i, e, j: (i, 0)),
                pl.BlockSpec((1, D_MODEL), lambda i, e, j: (0, 0)),
                modspec(4), modspec(3), modspec(5)]
    args = [x2d, g.reshape(1, D_MODEL), mod3, mod3, mod3]
    scratch = [pltpu.VMEM((tm, D_MODEL), BF16), pltpu.VMEM((tm, D_MODEL), F32)]
    if moe:
        in_specs.append(pl.BlockSpec((D_MODEL, ROUTER_PAD), lambda i, e, j: (0, 0)))
        args.append(router)
        scratch.append(pltpu.VMEM((tm, ROUTER_PAD), F32))
    in_specs += [pl.BlockSpec((1, D_MODEL, tf), lambda i, e, j: (e, 0, j)),
                 pl.BlockSpec((1, D_MODEL, tf), lambda i, e, j: (e, 0, j)),
                 pl.BlockSpec((1, tf, D_MODEL), lambda i, e, j: (e, j, 0))]
    args += [w1, w3, w2]
    return pl.pallas_call(
        functools.partial(_ffn_kernel, moe=moe, n_e=n_e, n_j=n_j),
        grid=(rows // tm, n_e, n_j),
        in_specs=in_specs,
        out_specs=pl.BlockSpec((tm, D_MODEL), lambda i, e, j: (i, 0)),
        out_shape=jax.ShapeDtypeStruct((rows, D_MODEL), F32),
        scratch_shapes=scratch,
        compiler_params=pltpu.CompilerParams(dimension_semantics=("arbitrary",) * 3,
                                             vmem_limit_bytes=56 * 1024 * 1024),
        name="moe_ffn" if moe else "dense_ffn",
    )(*args)


N_PRE = 9


def _dec_pre_kernel(r_ref, k_ref, v_ref, l_ref, q_ref, kr_ref, shift_ref, mu_ref, w0_ref, w2_ref, a0_ref,
                    a2_ref, g2_ref, kk_ref, ka_ref, cos_ref, sin_ref, o_ref):
    pieces = []
    for idx, ref in enumerate((r_ref, k_ref, v_ref, l_ref)):
        lanes = slice(idx * BRANCH_W, (idx + 1) * BRANCH_W)
        x = ref[...]
        pieces.append(x + (shift_ref[:, lanes] - x) * mu_ref[:, lanes])
    ones_bd = _head_ones()
    prm = (w0_ref[...], w2_ref[...], a0_ref[...], a2_ref[...], g2_ref[...], kk_ref[...], ka_ref[...])
    r, k_mod, v, lw, aa, bb, g = _rwkv_features(*pieces, prm, ones_bd)
    q_r = _rope(q_ref[...], cos_ref[...], sin_ref[...])
    k_r = _rope(kr_ref[...], cos_ref[...], sin_ref[...]) * (HEAD_DIM ** -0.5)
    for idx, val in enumerate((r, k_mod, v, jnp.exp(lw), aa, bb, g, q_r, k_r)):
        o_ref[:, idx * BRANCH_W:(idx + 1) * BRANCH_W] = val


def _dec_pre(proj, shift0, rp, pos0):
    rows = proj.shape[0]
    cos_t, sin_t = _rope_tables(1, pos0)
    col = lambda k: pl.BlockSpec((rows, BRANCH_W), lambda i: (0, k))
    const = lambda shape: pl.BlockSpec(shape, lambda i: (0,) * len(shape))
    vec = const((1, BRANCH_W))
    mat = const((BRANCH_W, BRANCH_W))
    mu, w0, w2p, a0, a2p, g2p, k_k, k_a = rp[:8]
    return pl.pallas_call(
        _dec_pre_kernel,
        grid=(1,),
        in_specs=[col(COL_RWKV), col(COL_RWKV + 1), col(COL_RWKV + 2), col(COL_RWKV + 3),
                  col(COL_RET), col(COL_RET + 1), const((rows, RWKV_IN_W)), const((1, RWKV_IN_W)),
                  vec, mat, vec, mat, mat, vec, vec, vec, vec],
        out_specs=const((rows, N_PRE * BRANCH_W)),
        out_shape=jax.ShapeDtypeStruct((rows, N_PRE * BRANCH_W), F32),
        compiler_params=_params(("arbitrary",)),
        name="dec_pre",
    )(proj, proj, proj, proj, proj, proj, shift0, mu, w0, w2p, a0, a2p, g2p, k_k, k_a, cos_t, sin_t)


def _dec_state_kernel(sw_ref, aa_ref, w_ref, bb_ref, km_ref, r_ref, vv_ref,
                      sr_ref, q_ref, kc_ref, vr_ref, gm_ref,
                      sw_out, y_out, sr_out, o_out):
    s = sw_ref[...]
    sa = jnp.sum(s * aa_ref[...], axis=-1, keepdims=True)
    s = s * w_ref[...] + sa * bb_ref[...] + vv_ref[...] * km_ref[...]
    sw_out[...] = s
    y_out[...] = jnp.sum(s * r_ref[...], axis=-1, keepdims=True)
    t = sr_ref[...]
    q = q_ref[...]
    kc = kc_ref[...]
    vr = vr_ref[...]
    gm = gm_ref[...]
    inter = jnp.sum(q * t, axis=1, keepdims=True) * gm
    att = jnp.sum(q * kc, axis=1, keepdims=True)
    o_out[...] = att * vr + inter
    sr_out[...] = t * gm + kc * vr


def _dec_state(s_rwkv, s_ret, pre, v_ret):
    bsz = s_rwkv.shape[0]
    nbh = bsz * N_HEADS
    tb = 64
    piece = lambda k: pre[:, k * BRANCH_W:(k + 1) * BRANCH_W]
    as_row = lambda t: t.reshape(nbh, 1, HEAD_DIM)
    as_col = lambda t: t.reshape(nbh, HEAD_DIM, 1)
    log_gamma = jnp.log(1.0 - jnp.exp2(-5.0 - jnp.arange(N_HEADS, dtype=F32)))
    gamma = jnp.broadcast_to(jnp.exp(log_gamma)[None, :, None, None], (bsz, N_HEADS, 1, HEAD_DIM))
    mat = pl.BlockSpec((tb, HEAD_DIM, HEAD_DIM), lambda i: (i, 0, 0))
    row = pl.BlockSpec((tb, 1, HEAD_DIM), lambda i: (i, 0, 0))
    colv = pl.BlockSpec((tb, HEAD_DIM, 1), lambda i: (i, 0, 0))
    sw, y, sr, o = pl.pallas_call(
        _dec_state_kernel,
        grid=(nbh // tb,),
        in_specs=[mat, row, row, row, row, row, colv, mat, colv, colv, row, row],
        out_specs=[mat, colv, mat, row],
        out_shape=[jax.ShapeDtypeStruct((nbh, HEAD_DIM, HEAD_DIM), F32),
                   jax.ShapeDtypeStruct((nbh, HEAD_DIM, 1), F32),
                   jax.ShapeDtypeStruct((nbh, HEAD_DIM, HEAD_DIM), F32),
                   jax.ShapeDtypeStruct((nbh, 1, HEAD_DIM), F32)],
        compiler_params=_params(("arbitrary",)),
        name="dec_state",
    )(s_rwkv.reshape(nbh, HEAD_DIM, HEAD_DIM), as_row(piece(4)), as_row(piece(3)), as_row(piece(5)),
      as_row(piece(1)), as_row(piece(0)), as_col(piece(2)),
      s_ret.reshape(nbh, HEAD_DIM, HEAD_DIM), as_col(piece(7)), as_col(piece(8)), as_row(v_ret),
      gamma.reshape(nbh, 1, HEAD_DIM))
    shape4 = (bsz, N_HEADS, HEAD_DIM, HEAD_DIM)
    return sw.reshape(shape4), y.reshape(bsz, BRANCH_W), sr.reshape(shape4), o.reshape(bsz, BRANCH_W)


def _dec_post_kernel(y_ref, r_ref, km_ref, v_ref, g_ref, o_ref, gr_ref, rk_ref, lng_ref, lnb_ref,
                     yw_out, yr_out):
    ones_bd = _head_ones()
    yw_out[...] = _rwkv_post(y_ref[...], r_ref[...], km_ref[...], v_ref[...], g_ref[...],
                             rk_ref[...], lng_ref[...], lnb_ref[...], ones_bd)
    o = o_ref[...]
    o = o * lax.rsqrt(_head_sum(o * o, ones_bd) * (1.0 / HEAD_DIM) + RMS_EPS)
    g = gr_ref[...]
    yr_out[...] = o * (g * jax.nn.sigmoid(g))


def _dec_post(y_rwkv, pre, o_ret, proj, rp):
    rows = proj.shape[0]
    blk = pl.BlockSpec((rows, BRANCH_W), lambda i: (0, 0))
    col = lambda k: pl.BlockSpec((rows, BRANCH_W), lambda i: (0, k))
    vec = pl.BlockSpec((1, BRANCH_W), lambda i: (0, 0))
    r_k, ln_g, ln_b = rp[8:]
    return pl.pallas_call(
        _dec_post_kernel,
        grid=(1,),
        in_specs=[blk, col(0), col(1), col(2), col(6), blk, col(COL_RET + 3), vec, vec, vec],
        out_specs=[blk, blk],
        out_shape=[jax.ShapeDtypeStruct((rows, BRANCH_W), F32)] * 2,
        compiler_params=_params(("arbitrary",)),
        name="dec_post",
    )(y_rwkv, pre, pre, pre, pre, o_ret, proj, r_k, ln_g, ln_b)


def _moba_dec_kernel(pt_ref, q_ref, kn_ref, vn_ref, *refs, n_pages):
    del pt_ref
    k_refs = refs[:n_pages]
    v_refs = refs[n_pages:2 * n_pages]
    o_ref, sc_scr, km_scr = refs[2 * n_pages:]
    page = k_refs[0].shape[2]
    nblk = n_pages * page // MOBA_BLOCK
    per = MOBA_BLOCK // page
    scale = HEAD_DIM ** -0.5
    ones = jnp.ones((HEAD_DIM, 128), F32)
    ri = lax.broadcasted_iota(jnp.int32, (nblk, nblk), 0)
    ci = lax.broadcasted_iota(jnp.int32, (nblk, nblk), 1)
    eye = (ri == ci).astype(F32)
    for h in range(N_HEADS):
        q_h = q_ref[0, h:h + 1, :]
        for n in range(nblk):
            ksum = jnp.zeros((1, HEAD_DIM), F32)
            for pg in range(n * per, (n + 1) * per):
                k_h = k_refs[pg][0, 0, :, h, :]
                ksum = ksum + jnp.sum(k_h, axis=0, keepdims=True)
                sc_scr[pg * page:(pg + 1) * page, :] = _dot(k_h * q_h, ones) * scale
            km_scr[n:n + 1, :] = ksum * (1.0 / MOBA_BLOCK)
        g_col = jnp.sum(km_scr[...] * q_h, axis=1, keepdims=True)
        g_self = jnp.broadcast_to(g_col, (nblk, nblk))
        g_other = _dot_hi(jnp.ones((nblk, nblk), F32), eye * g_col)
        beats = (g_other > g_self) | ((g_other == g_self) & (ci < ri))
        sel = (jnp.sum(beats.astype(F32), axis=1, keepdims=True) < MOBA_TOPK).astype(F32)
        s_own = jnp.sum(q_h * kn_ref[0, h:h + 1, :], axis=1, keepdims=True) * scale
        m = jnp.broadcast_to(s_own, (1, 128))
        for n in range(nblk):
            bmax = jnp.max(sc_scr[n * MOBA_BLOCK:(n + 1) * MOBA_BLOCK, :], axis=0, keepdims=True)
            m = jnp.maximum(m, jnp.where(sel[n:n + 1, :] > 0.0, bmax, NEG_INF))
        l = jnp.exp(s_own - m)
        o = l[:, :HEAD_DIM] * vn_ref[0, h:h + 1, :]
        for n in range(nblk):
            seln = sel[n:n + 1, :] > 0.0
            for pg in range(n * per, (n + 1) * per):
                p = jnp.where(seln, jnp.exp(sc_scr[pg * page:(pg + 1) * page, :] - m), 0.0)
                l = l + jnp.sum(p, axis=0, keepdims=True)
                o = o + jnp.sum(p[:, :HEAD_DIM] * v_refs[pg][0, 0, :, h, :], axis=0, keepdims=True)
        o_ref[0, h:h + 1, :] = o / l[:, :HEAD_DIM]


def _moba_dec(q, k_new, v_new, cache_k, cache_v, page_table, layer):
    bsz, n_pages = page_table.shape
    page = cache_k.shape[2]
    heads = lambda t: t.reshape(bsz, N_HEADS, HEAD_DIM)
    qspec = pl.BlockSpec((1, N_HEADS, HEAD_DIM), lambda b, pt: (b, 0, 0))
    pspec = lambda pg: pl.BlockSpec((1, 1, page, N_HEADS, HEAD_DIM), lambda b, pt: (layer, pt[b, pg], 0, 0, 0))
    out = pl.pallas_call(
        functools.partial(_moba_dec_kernel, n_pages=n_pages),
        grid_spec=pltpu.PrefetchScalarGridSpec(
            num_scalar_prefetch=1,
            grid=(bsz,),
            in_specs=[qspec, qspec, qspec] + [pspec(pg) for pg in range(n_pages)] * 2,
            out_specs=qspec,
            scratch_shapes=[pltpu.VMEM((n_pages * page, 128), F32),
                            pltpu.VMEM((n_pages * page // MOBA_BLOCK, HEAD_DIM), F32)]),
        out_shape=jax.ShapeDtypeStruct((bsz, N_HEADS, HEAD_DIM), F32),
        compiler_params=_params(("arbitrary",)),
        name="moba_decode",
    )(page_table, heads(q), heads(k_new), heads(v_new), *([cache_k] * n_pages), *([cache_v] * n_pages))
    return out.reshape(bsz, BRANCH_W)


def _final_norm_kernel(x_ref, g_ref, o_ref):
    x = x_ref[...]
    o_ref[...] = x * lax.rsqrt(jnp.mean(x * x, axis=-1, keepdims=True) + RMS_EPS) * g_ref[...]


def _final_norm(x2d, g, *, tm):
    rows = x2d.shape[0]
    return pl.pallas_call(
        _final_norm_kernel,
        grid=(rows // tm,),
        in_specs=[pl.BlockSpec((tm, D_MODEL), lambda i: (i, 0)), pl.BlockSpec((1, D_MODEL), lambda i: (0, 0))],
        out_specs=pl.BlockSpec((tm, D_MODEL), lambda i: (i, 0)),
        out_shape=jax.ShapeDtypeStruct((rows, D_MODEL), F32),
        compiler_params=_params(("arbitrary",)),
        name="final_norm",
    )(x2d, g.reshape(1, D_MODEL))


RWKV_CHUNK = 64
RET_CHUNK = 256
S5_TIME_CHUNK = 128


def _ffn_any(x2d, g, mod3, ffn, *, tm, tiles_per_group):
    if len(ffn) == 3:
        return _ffn(x2d, g, mod3, ffn, tm=tm, tiles_per_group=tiles_per_group, tf=256)
    return _ffn(x2d, g, mod3, ffn, tm=min(tm, 512), tiles_per_group=tiles_per_group * max(tm // 512, 1),
                tf=ffn[1].shape[-1])


def _prompt_layer(x2d, bsz, t_len, mod_l, lp):
    mod3 = mod_l.reshape(bsz, 1, -1)
    tm = min(1024, t_len)
    tpg = t_len // tm
    proj, u_tb = _inproj(x2d, lp['norm_mix'], mod3, lp['w_in'], tm=tm, tiles_per_group=tpg,
                         tb_shape=(t_len, bsz * BRANCH_W))
    proj3 = proj.reshape(bsz, t_len, IN_W)
    z_state = jnp.zeros((bsz, S5_W), F32)
    y_s5, s5_re, s5_im = _s5(u_tb.reshape(t_len * bsz, BRANCH_W), z_state, z_state, lp['s5p'], lp['s5_d'],
                             lp['s5_w_glu'], nb=bsz, t_len=t_len, tc=min(S5_TIME_CHUNK, t_len))
    z_mat = jnp.zeros((bsz, N_HEADS, HEAD_DIM, HEAD_DIM), F32)
    y_rwkv, s_rwkv, shift_n = _rwkv(proj3, jnp.zeros((bsz, RWKV_IN_W), F32), z_mat, lp['rwkv'],
                                    chunk=min(RWKV_CHUNK, t_len))
    y_ret, s_ret = _ret(proj3, z_mat, 0, chunk=min(RET_CHUNK, t_len))
    y_moba = _moba_prompt(proj3)
    kv = lambda k: proj3[:, :, (COL_MOBA + k) * BRANCH_W:(COL_MOBA + k + 1) * BRANCH_W].reshape(
        bsz, t_len, N_HEADS, HEAD_DIM)
    tmm = min(256, t_len)
    tpm = t_len // tmm
    s5_spec = pl.BlockSpec((tmm, BRANCH_W), lambda i: (i % tpm, i // tpm))
    flat = lambda y: y.reshape(bsz * t_len, BRANCH_W)
    x2d = _merge(x2d, y_s5.reshape(t_len, bsz * BRANCH_W), s5_spec, flat(y_rwkv), flat(y_ret), flat(y_moba),
                 proj, mod3, lp['w_branch'], lp['w_out'], tm=tmm, tiles_per_group=tpm)
    x2d = _ffn_any(x2d, lp['norm_ffn'], mod3, lp['ffn'], tm=tm, tiles_per_group=tpg)
    g16 = (bsz, S5_GROUPS, S5_STATE)
    return x2d, (s5_re.reshape(g16), s5_im.reshape(g16), s_rwkv, shift_n, s_ret, kv(1), kv(2))


def _decode_layer(x2d, mod_l, lp, layer, pos0, s5_re0, s5_im0, rwkv_s0, shift0, ret_s0, cache_k, cache_v,
                  page_table):
    bsz = x2d.shape[0]
    mod3 = mod_l.reshape(1, bsz, -1)
    proj = _inproj(x2d, lp['norm_mix'], mod3, lp['w_in'], tm=bsz, tiles_per_group=1)[0]
    piece = lambda k: proj[:, k * BRANCH_W:(k + 1) * BRANCH_W]
    y_s5, s5_re, s5_im = _s5(piece(COL_S5), s5_re0.reshape(bsz, S5_W), s5_im0.reshape(bsz, S5_W), lp['s5p'],
                             lp['s5_d'], lp['s5_w_glu'], nb=bsz, t_len=1, tc=1)
    pre = _dec_pre(proj, shift0, lp['rwkv'], pos0)
    s_rwkv, y_raw, s_ret, o_raw = _dec_state(rwkv_s0, ret_s0, pre, piece(COL_RET + 2))
    y_rwkv, y_ret = _dec_post(y_raw, pre, o_raw, proj, lp['rwkv'])
    k_new = piece(COL_MOBA + 1)
    v_new = piece(COL_MOBA + 2)
    y_moba = _moba_dec(piece(COL_MOBA), k_new, v_new, cache_k, cache_v, page_table, layer)
    s5_spec = pl.BlockSpec((bsz, BRANCH_W), lambda i: (i, 0))
    x2d = _merge(x2d, y_s5, s5_spec, y_rwkv, y_ret, y_moba, proj, mod3, lp['w_branch'], lp['w_out'],
                 tm=bsz, tiles_per_group=1)
    x2d = _ffn_any(x2d, lp['norm_ffn'], mod3, lp['ffn'], tm=bsz, tiles_per_group=1)
    g16 = (bsz, S5_GROUPS, S5_STATE)
    kv4 = lambda t: t.reshape(bsz, 1, N_HEADS, HEAD_DIM)
    shift_n = proj[:, COL_RWKV * BRANCH_W:COL_RWKV * BRANCH_W + RWKV_IN_W]
    return x2d, (s5_re.reshape(g16), s5_im.reshape(g16), s_rwkv, shift_n, s_ret, kv4(k_new), kv4(v_new))


def kernel(x_prompt, x_sample, c_prompt, c_sample, state_s5_re, state_s5_im, state_rwkv, state_rwkv_shift, state_ret, cache_moba_k, cache_moba_v, page_table, norm_mix_g, norm_ffn_g, norm_final_g, w_ada, b_ada, w_in, s5_lambda_re, s5_lambda_im, s5_log_dt, s5_b_re, s5_b_im, s5_c_re, s5_c_im, s5_d, s5_w_glu, rwkv_mu, rwkv_w0, rwkv_w2, rwkv_a0, rwkv_a2, rwkv_g2, rwkv_k_k, rwkv_k_a, rwkv_r_k, rwkv_ln_g, rwkv_ln_b, w_branch, w_out, ffn_w1, ffn_w3, ffn_w2, moe_router, moe_w1, moe_w3, moe_w2):
    bp, t_len, _ = x_prompt.shape
    bs = x_sample.shape[0]
    depth = w_in.shape[0]
    past_len = page_table.shape[1] * cache_moba_k.shape[2]
    mod_all = _ada(jnp.concatenate([c_prompt, c_sample], axis=0), w_ada, b_ada)
    xp = x_prompt.reshape(bp * t_len, D_MODEL)
    xs = x_sample.reshape(bs, D_MODEL)
    outs_p = [[] for _ in range(7)]
    outs_s = [[] for _ in range(7)]
    for l in range(depth):
        if l % 2 == 0:
            ffn = (ffn_w1[l // 2], ffn_w3[l // 2], ffn_w2[l // 2])
        else:
            ffn = (moe_router[l // 2], moe_w1[l // 2], moe_w3[l // 2], moe_w2[l // 2])
        lp = {
            'norm_mix': norm_mix_g[l], 'norm_ffn': norm_ffn_g[l], 'w_in': w_in[l],
            's5p': _s5_params(s5_lambda_re[l], s5_lambda_im[l], s5_log_dt[l], s5_b_re[l], s5_b_im[l],
                              s5_c_re[l], s5_c_im[l]),
            's5_d': s5_d[l], 's5_w_glu': s5_w_glu[l],
            'rwkv': _rwkv_params(rwkv_mu[l], rwkv_w0[l], rwkv_w2[l], rwkv_a0[l], rwkv_a2[l], rwkv_g2[l],
                                 rwkv_k_k[l], rwkv_k_a[l], rwkv_r_k[l], rwkv_ln_g[l], rwkv_ln_b[l]),
            'w_branch': w_branch[l], 'w_out': w_out[l], 'ffn': ffn,
        }
        xp, st_p = _prompt_layer(xp, bp, t_len, mod_all[l, :bp], lp)
        xs, st_s = _decode_layer(xs, mod_all[l, bp:], lp, l, past_len, state_s5_re[l], state_s5_im[l],
                                 state_rwkv[l], state_rwkv_shift[l], state_ret[l], cache_moba_k, cache_moba_v,
                                 page_table)
        for j in range(7):
            outs_p[j].append(st_p[j])
            outs_s[j].append(st_s[j])
    y_prompt = _final_norm(xp, norm_final_g, tm=1024).reshape(bp, t_len, D_MODEL)
    y_sample = _final_norm(xs, norm_final_g, tm=bs).reshape(bs, 1, D_MODEL)
    stack = lambda outs: [jnp.stack(o, axis=0) for o in outs]
    return (y_prompt, y_sample, *stack(outs_p), *stack(outs_s))
```

```python
import functools
import math

import jax
import jax.numpy as jnp
from jax import lax
from jax.experimental import pallas as pl
from jax.experimental.pallas import tpu as pltpu

F32 = jnp.float32
BF16 = jnp.bfloat16
HIGHEST = lax.Precision.HIGHEST

D_MODEL = 1024
BRANCH_W = 256
HEAD_DIM = 64
N_HEADS = 4
N_BRANCH = 4
S5_GROUPS = 16
S5_STATE = 64
S5_CH = 16
S5_W = S5_GROUPS * S5_STATE
IN_W = 7168
RWKV_IN_W = 1024
RWKV_LN_EPS = 64e-5
RMS_EPS = 1e-6
ROPE_BASE = 10000.0
MOBA_BLOCK = 256
MOBA_TOPK = 3
N_EXPERTS = 8
ROUTER_PAD = 128
NEG_INF = float("-inf")

COL_S5 = 0
COL_RWKV = 1
COL_RET = 5
COL_MOBA = 9
COL_GATE = 12

VMEM_LIMIT = 48 * 1024 * 1024


def _params(sem):
    return pltpu.CompilerParams(dimension_semantics=sem, vmem_limit_bytes=VMEM_LIMIT)


def _dot(a, b):
    return jnp.dot(a.astype(BF16), b.astype(BF16), preferred_element_type=F32)


def _dot_hi(a, b):
    return jnp.dot(a, b, precision=HIGHEST, preferred_element_type=F32)


def _dot_nt(a, b):
    return lax.dot_general(a.astype(BF16), b.astype(BF16), (((1,), (1,)), ((), ())),
                           preferred_element_type=F32)


def _dot_nt_hi(a, b):
    return lax.dot_general(a, b, (((1,), (1,)), ((), ())), precision=HIGHEST,
                           preferred_element_type=F32)


def _dot_tn(a, b):
    return lax.dot_general(a.astype(BF16), b.astype(BF16), (((0,), (0,)), ((), ())),
                           preferred_element_type=F32)


def _head_masks(width=BRANCH_W):
    lane = lax.broadcasted_iota(jnp.int32, (1, width), 1)
    return [(lane // HEAD_DIM == h).astype(F32) for h in range(N_HEADS)]


def _head_ones():
    r = lax.broadcasted_iota(jnp.int32, (BRANCH_W, BRANCH_W), 0) // HEAD_DIM
    c = lax.broadcasted_iota(jnp.int32, (BRANCH_W, BRANCH_W), 1) // HEAD_DIM
    return (r == c).astype(F32)


def _split3(x):
    hi = x.astype(BF16)
    rest = x - hi.astype(F32)
    mid = rest.astype(BF16)
    return hi, mid, (rest - mid.astype(F32)).astype(BF16)


def _head_sum(x, ones_bd):
    w = ones_bd.astype(BF16)
    return sum(jnp.dot(p, w, preferred_element_type=F32) for p in _split3(x))


def _ada_kernel(c_ref, w_ref, b_ref, o_ref):
    c = c_ref[...]
    h = c * jax.nn.sigmoid(c)
    o_ref[0] = _dot(h, w_ref[0]) + b_ref[0]


def _ada(c_all, w_ada, b_ada):
    depth, _, width = w_ada.shape
    rows = c_all.shape[0]
    tn = 1024
    return pl.pallas_call(
        _ada_kernel,
        grid=(depth, width // tn),
        in_specs=[pl.BlockSpec((rows, D_MODEL), lambda l, j: (0, 0)),
                  pl.BlockSpec((1, D_MODEL, tn), lambda l, j: (l, 0, j)),
                  pl.BlockSpec((1, 1, tn), lambda l, j: (l, 0, j))],
        out_specs=pl.BlockSpec((1, rows, tn), lambda l, j: (l, 0, j)),
        out_shape=jax.ShapeDtypeStruct((depth, rows, width), F32),
        compiler_params=_params(("arbitrary", "arbitrary")),
        name="ada_mod",
    )(c_all, w_ada, b_ada.reshape(depth, 1, width))


def _modulated_norm(x, g, sc, sh):
    y = x * lax.rsqrt(jnp.mean(x * x, axis=-1, keepdims=True) + RMS_EPS) * g
    return y * (1.0 + sc) + sh


def _inproj_kernel(x_ref, g_ref, sc_ref, sh_ref, w_ref, o_ref, *rest, emit_tb):
    h_scr = rest[-1]
    j = pl.program_id(1)

    @pl.when(j == 0)
    def _():
        h_scr[...] = _modulated_norm(x_ref[...], g_ref[...], sc_ref[0], sh_ref[0]).astype(BF16)

    acc = jnp.dot(h_scr[...], w_ref[...].astype(BF16), preferred_element_type=F32)
    o_ref[...] = acc
    if emit_tb:
        u_ref = rest[0]

        @pl.when(j == 0)
        def _():
            u_ref[...] = acc[:, :BRANCH_W]


def _inproj(x2d, g, mod3, w, *, tm, tiles_per_group, tb_shape=None):
    rows = x2d.shape[0]
    m = mod3.shape[1]
    tn = 512
    tpg = tiles_per_group
    in_specs = [pl.BlockSpec((tm, D_MODEL), lambda i, j: (i, 0)),
                pl.BlockSpec((1, D_MODEL), lambda i, j: (0, 0)),
                pl.BlockSpec((1, m, D_MODEL), lambda i, j: (i // tpg, 0, 1)),
                pl.BlockSpec((1, m, D_MODEL), lambda i, j: (i // tpg, 0, 0)),
                pl.BlockSpec((D_MODEL, tn), lambda i, j: (0, j))]
    out_specs = [pl.BlockSpec((tm, tn), lambda i, j: (i, j))]
    out_shape = [jax.ShapeDtypeStruct((rows, IN_W), F32)]
    if tb_shape is not None:
        out_specs.append(pl.BlockSpec((tm, BRANCH_W), lambda i, j: (i % tpg, i // tpg)))
        out_shape.append(jax.ShapeDtypeStruct(tb_shape, F32))
    return pl.pallas_call(
        functools.partial(_inproj_kernel, emit_tb=tb_shape is not None),
        grid=(rows // tm, IN_W // tn),
        in_specs=in_specs, out_specs=out_specs, out_shape=out_shape,
        scratch_shapes=[pltpu.VMEM((tm, D_MODEL), BF16)],
        compiler_params=_params(("arbitrary", "arbitrary")),
        name="inproj",
    )(x2d, g.reshape(1, D_MODEL), mod3, mod3, w)


def _s5_kernel(u_ref, x0r_ref, x0i_ref, lbr_ref, lbi_ref, br_ref, bi_ref, cr_ref, ci_ref, d_ref, wg_ref,
               y_ref, sr_ref, si_ref, bur, bui, xr, xi, *, nb, tc):
    c = pl.program_id(0)

    @pl.when(c == 0)
    def _():
        xr[...] = x0r_ref[...]
        xi[...] = x0i_ref[...]

    u = u_ref[...]
    ub = u.astype(BF16)
    bur[...] = jnp.dot(ub, br_ref[...].astype(BF16), preferred_element_type=F32)
    bui[...] = jnp.dot(ub, bi_ref[...].astype(BF16), preferred_element_type=F32)
    lbr = jnp.broadcast_to(lbr_ref[...], (nb, S5_W))
    lbi = jnp.broadcast_to(lbi_ref[...], (nb, S5_W))

    def body(t, carry):
        sr, si = carry
        rows = pl.ds(pl.multiple_of(t * nb, nb), nb)
        nr = lbr * sr - lbi * si + bur[rows, :]
        ni = lbr * si + lbi * sr + bui[rows, :]
        bur[rows, :] = nr
        bui[rows, :] = ni
        return nr, ni

    sr, si = lax.fori_loop(0, tc, body, (xr[...], xi[...]))
    xr[...] = sr
    xi[...] = si
    sr_ref[...] = sr
    si_ref[...] = si
    y = _dot(bur[...], cr_ref[...]) - _dot(bui[...], ci_ref[...]) + d_ref[...] * u
    z = jax.nn.gelu(y)
    y_ref[...] = z * jax.nn.sigmoid(_dot(z, wg_ref[...]))


def _s5_params(lam_re, lam_im, log_dt, b_re, b_im, c_re, c_im):
    dt = jnp.exp(log_dt)[:, None]
    mag = jnp.exp(lam_re * dt)
    lb_re = mag * jnp.cos(lam_im * dt)
    lb_im = mag * jnp.sin(lam_im * dt)
    den = lam_re * lam_re + lam_im * lam_im
    q_re = ((lb_re - 1.0) * lam_re + lb_im * lam_im) / den
    q_im = (lb_im * lam_re - (lb_re - 1.0) * lam_im) / den
    bb_re = q_re[..., None] * b_re - q_im[..., None] * b_im
    bb_im = q_re[..., None] * b_im + q_im[..., None] * b_re
    eye = jnp.eye(S5_GROUPS, dtype=F32)
    to_in = lambda bb: jnp.einsum('gnc,gh->gchn', bb, eye).reshape(BRANCH_W, S5_W)
    to_out = lambda cc: jnp.einsum('gcn,gh->gnhc', cc, eye).reshape(S5_W, BRANCH_W)
    return (lb_re.reshape(1, S5_W), lb_im.reshape(1, S5_W), to_in(bb_re), to_in(bb_im),
            to_out(c_re), to_out(c_im))


def _s5(u_tb, x0_re, x0_im, s5p, d_skip, w_glu, *, nb, t_len, tc):
    lb_re, lb_im, bin_re, bin_im, cout_re, cout_im = s5p
    rows = tc * nb
    const = lambda shape: pl.BlockSpec(shape, lambda c: (0,) * len(shape))
    return pl.pallas_call(
        functools.partial(_s5_kernel, nb=nb, tc=tc),
        grid=(t_len // tc,),
        in_specs=[pl.BlockSpec((rows, BRANCH_W), lambda c: (c, 0)),
                  const((nb, S5_W)), const((nb, S5_W)), const((1, S5_W)), const((1, S5_W)),
                  const((BRANCH_W, S5_W)), const((BRANCH_W, S5_W)),
                  const((S5_W, BRANCH_W)), const((S5_W, BRANCH_W)),
                  const((1, BRANCH_W)), const((BRANCH_W, BRANCH_W))],
        out_specs=[pl.BlockSpec((rows, BRANCH_W), lambda c: (c, 0)), const((nb, S5_W)), const((nb, S5_W))],
        out_shape=[jax.ShapeDtypeStruct((t_len * nb, BRANCH_W), F32),
                   jax.ShapeDtypeStruct((nb, S5_W), F32), jax.ShapeDtypeStruct((nb, S5_W), F32)],
        scratch_shapes=[pltpu.VMEM((rows, S5_W), F32), pltpu.VMEM((rows, S5_W), F32),
                        pltpu.VMEM((nb, S5_W), F32), pltpu.VMEM((nb, S5_W), F32)],
        compiler_params=_params(("arbitrary",)),
        name="s5",
    )(u_tb, x0_re, x0_im, lb_re, lb_im, bin_re, bin_im, cout_re, cout_im,
      d_skip.reshape(1, BRANCH_W), w_glu)


def _rwkv_features(pm_r, pm_k, pm_v, pm_l, prm, ones_bd):
    w0, w2p, a0, a2p, g2p, k_k, k_a = prm
    w_raw = w0 + _dot(jnp.tanh(pm_l), w2p)
    lw = -jax.nn.sigmoid(w_raw) * math.exp(-0.5)
    a_sig = jax.nn.sigmoid(a0 + _dot(pm_l, a2p))
    g = _dot(jax.nn.sigmoid(pm_l), g2p)
    kk = pm_k * k_k
    kk = kk * lax.rsqrt(_head_sum(kk * kk, ones_bd) + 1e-12)
    k_mod = pm_k * (1.0 + (a_sig - 1.0) * k_a)
    return pm_r, k_mod, pm_v, lw, -kk, kk * a_sig, g


def _rwkv_post(y, r, k_mod, v, g, r_k, ln_g, ln_b, ones_bd):
    inv = 1.0 / HEAD_DIM
    mean = _head_sum(y, ones_bd) * inv
    yc = y - mean
    var = _head_sum(yc * yc, ones_bd) * inv
    yn = yc * lax.rsqrt(var + RWKV_LN_EPS) * ln_g + ln_b
    bonus = _head_sum(r * k_mod * r_k, ones_bd) * v
    return (yn + bonus) * g


RWKV_CHUNK = 64


def _cumsum_rows(tril, x):
    return sum(jnp.dot(tril, p, preferred_element_type=F32) for p in _split3(x))


def _per_head(x, ones_bd):
    return jnp.concatenate([x] * N_HEADS, axis=0) * ones_bd


def _rwkv_chunk(r, k_mod, v, lw, aa, bb, s, ones_bd):
    n = RWKV_CHUNK
    t_idx = lax.broadcasted_iota(jnp.int32, (n, BRANCH_W), 0)
    i_idx = lax.broadcasted_iota(jnp.int32, (n, BRANCH_W), 1) % n
    strict = (t_idx > i_idx).astype(F32)
    incl = (t_idx >= i_idx).astype(F32)
    eye_c = (t_idx == i_idx).astype(F32)
    tril = (lax.broadcasted_iota(jnp.int32, (n, n), 0) >= lax.broadcasted_iota(jnp.int32, (n, n), 1)).astype(BF16)
    cum = _cumsum_rows(tril, lw)
    tot = cum[n - 1:n, :]
    e_neg = jnp.exp(-cum)
    e_rem = jnp.exp(tot - cum)
    ar = jnp.concatenate([aa * jnp.exp(cum - lw), r * jnp.exp(cum)], axis=0)
    xb = _dot_nt(ar, _per_head(bb * e_neg, ones_bd))
    xk = _dot_nt(ar, _per_head(k_mod * e_neg, ones_bd))
    nmat, lb = xb[:n] * strict, xb[n:] * incl
    mmat, lk = xk[:n] * strict, xk[n:] * incl
    tinv = eye_c + nmat
    pw = nmat
    for _ in range(int(math.log2(n)) - 1):
        pw = _dot(pw, _per_head(pw, ones_bd))
        tinv = tinv + _dot(pw, _per_head(tinv, ones_bd))
    ss = _dot_nt(ar, s)
    v_bd = _per_head(v, ones_bd)
    u = _dot(tinv, _per_head(ss[:n] + _dot(mmat, v_bd), ones_bd))
    y = ss[n:] + _dot(lb, _per_head(u, ones_bd)) + _dot(lk, v_bd)
    upd = _dot_tn(jnp.concatenate([u, v], axis=0), jnp.concatenate([bb * e_rem, k_mod * e_rem], axis=0))
    return y, s * jnp.exp(tot) + ones_bd * upd


def _rwkv_kernel(r_ref, k_ref, v_ref, l_ref, shift_ref, s0_ref, mu_ref, w0_ref, w2_ref, a0_ref, a2_ref,
                 g2_ref, kk_ref, ka_ref, rk_ref, lng_ref, lnb_ref,
                 y_ref, s_ref, sh_ref, prev_scr, s_scr, *, nb):
    c = pl.program_id(1)
    n = RWKV_CHUNK

    @pl.when(c == 0)
    def _():
        prev_scr[...] = shift_ref[...]
        s_scr[...] = s0_ref[...]

    row = lax.broadcasted_iota(jnp.int32, (n, BRANCH_W), 0)
    ones_bd = _head_ones()
    prm = (w0_ref[...], w2_ref[...], a0_ref[...], a2_ref[...], g2_ref[...], kk_ref[...], ka_ref[...])
    for b in range(nb):
        pieces = []
        for idx, ref in enumerate((r_ref, k_ref, v_ref, l_ref)):
            lanes = slice(idx * BRANCH_W, (idx + 1) * BRANCH_W)
            x = ref[b]
            x_prev = jnp.where(row == 0, prev_scr[b, :, lanes], pltpu.roll(x, 1, 0))
            pieces.append(x + (x_prev - x) * mu_ref[:, lanes])
            prev_scr[b, :, lanes] = x[n - 1:n, :]
            sh_ref[b, :, lanes] = x[n - 1:n, :]
        r, k_mod, v, lw, aa, bb, g = _rwkv_features(*pieces, prm, ones_bd)
        y, s_new = _rwkv_chunk(r, k_mod, v, lw, aa, bb, s_scr[b], ones_bd)
        s_scr[b] = s_new
        s_ref[b] = s_new
        y_ref[b] = _rwkv_post(y, r, k_mod, v, g, rk_ref[...], lng_ref[...], lnb_ref[...], ones_bd)


def _rwkv_params(mu, w0, w2, a0, a2, g2, k_k, k_a, r_k, ln_g, ln_b):
    row = lambda t: t.reshape(1, -1)
    pad = lambda w, lo: jnp.zeros((BRANCH_W, BRANCH_W), F32).at[lo:lo + w.shape[0]].set(w)
    return (row(mu), row(w0), pad(w2, 0), row(a0), pad(a2, 64), pad(g2, 128), row(k_k), row(k_a),
            row(r_k), row(ln_g), row(ln_b))


def _block_diag_state(s):
    bsz = s.shape[0]
    eye = jnp.eye(N_HEADS, dtype=F32)
    return jnp.einsum('bhij,hg->bhigj', s, eye).reshape(bsz, BRANCH_W, BRANCH_W)


def _diag_blocks(s_bd):
    bsz = s_bd.shape[0]
    s5 = s_bd.reshape(bsz, N_HEADS, HEAD_DIM, N_HEADS, HEAD_DIM)
    return jnp.stack([s5[:, h, :, h, :] for h in range(N_HEADS)], axis=1)


def _rwkv(proj3, shift0, s0, rp, *, nb):
    bsz, t_len, _ = proj3.shape
    chunk = RWKV_CHUNK
    col = lambda k: pl.BlockSpec((nb, chunk, BRANCH_W), lambda b, c: (b, c, COL_RWKV + k))
    const = lambda shape: pl.BlockSpec(shape, lambda b, c: (0,) * len(shape))
    vec = const((1, BRANCH_W))
    mat = const((BRANCH_W, BRANCH_W))
    y, s_bd, shift_n = pl.pallas_call(
        functools.partial(_rwkv_kernel, nb=nb),
        grid=(bsz // nb, t_len // chunk),
        in_specs=[col(0), col(1), col(2), col(3),
                  pl.BlockSpec((nb, 1, RWKV_IN_W), lambda b, c: (b, 0, 0)),
                  pl.BlockSpec((nb, BRANCH_W, BRANCH_W), lambda b, c: (b, 0, 0)),
                  const((1, RWKV_IN_W)), vec, mat, vec, mat, mat, vec, vec, vec, vec, vec],
        out_specs=[pl.BlockSpec((nb, chunk, BRANCH_W), lambda b, c: (b, c, 0)),
                   pl.BlockSpec((nb, BRANCH_W, BRANCH_W), lambda b, c: (b, 0, 0)),
                   pl.BlockSpec((nb, 1, RWKV_IN_W), lambda b, c: (b, 0, 0))],
        out_shape=[jax.ShapeDtypeStruct((bsz, t_len, BRANCH_W), F32),
                   jax.ShapeDtypeStruct((bsz, BRANCH_W, BRANCH_W), F32),
                   jax.ShapeDtypeStruct((bsz, 1, RWKV_IN_W), F32)],
        scratch_shapes=[pltpu.VMEM((nb, 1, RWKV_IN_W), F32), pltpu.VMEM((nb, BRANCH_W, BRANCH_W), F32)],
        compiler_params=_params(("arbitrary", "arbitrary")),
        name="rwkv",
    )(proj3, proj3, proj3, proj3, shift0.reshape(bsz, 1, RWKV_IN_W), _block_diag_state(s0), *rp)
    return y, _diag_blocks(s_bd), shift_n.reshape(bsz, RWKV_IN_W)


def _rope_tables(t_len, pos0):
    half = HEAD_DIM // 2
    freqs = 1.0 / (ROPE_BASE ** jnp.linspace(0.0, 1.0, half, dtype=F32))
    pos = jnp.arange(t_len, dtype=F32) + pos0
    ang = pos[:, None] * freqs[None, :]
    cos = jnp.cos(ang)
    sin = jnp.sin(ang)
    cos_t = jnp.tile(jnp.concatenate([cos, cos], axis=-1), (1, N_HEADS))
    sin_t = jnp.tile(jnp.concatenate([-sin, sin], axis=-1), (1, N_HEADS))
    return cos_t, sin_t


def _rope(x, cos_t, sin_t):
    lane = lax.broadcasted_iota(jnp.int32, x.shape, 1)
    first = (lane % HEAD_DIM) < (HEAD_DIM // 2)
    swapped = jnp.where(first, pltpu.roll(x, BRANCH_W - HEAD_DIM // 2, 1), pltpu.roll(x, HEAD_DIM // 2, 1))
    return x * cos_t + swapped * sin_t


def _ret_tables(chunk):
    log_gamma = jnp.log(1.0 - jnp.exp2(-5.0 - jnp.arange(N_HEADS, dtype=F32)))
    i = jnp.arange(chunk, dtype=F32)
    diff = i[:, None] - i[None, :]
    dmask = jnp.where(diff >= 0, jnp.exp(log_gamma[:, None, None] * jnp.maximum(diff, 0.0)), 0.0)
    lanes = lambda t: jnp.repeat(t, HEAD_DIM, axis=-1)
    xi = lanes(jnp.exp(log_gamma[None, :] * (i[:, None] + 1.0)))
    zeta = lanes(jnp.exp(log_gamma[None, :] * (chunk - 1.0 - i[:, None])))
    g_chunk = lanes(jnp.exp(log_gamma * chunk)[None, :])
    return dmask, xi, zeta, g_chunk


def _ret_kernel(q_ref, k_ref, v_ref, g_ref, s0_ref, cos_ref, sin_ref, dm_ref, xi_ref, zeta_ref, gch_ref,
                y_ref, s_ref, s_scr):
    c = pl.program_id(1)

    @pl.when(c == 0)
    def _():
        s_scr[...] = s0_ref[0]

    cos_t = cos_ref[...]
    sin_t = sin_ref[...]
    q = _rope(q_ref[0], cos_t, sin_t)
    k = _rope(k_ref[0], cos_t, sin_t) * (HEAD_DIM ** -0.5)
    v = v_ref[0]
    s = s_scr[...]
    ones_bd = _head_ones()
    o = _dot(q, s) * xi_ref[...]
    for h, mh in enumerate(_head_masks()):
        att = _dot_nt(q * mh, k) * dm_ref[h]
        o = o + _dot(att, v) * mh
    s_new = s * gch_ref[...] + ones_bd * _dot_tn(k * zeta_ref[...], v)
    s_scr[...] = s_new
    s_ref[0] = s_new
    o = o * lax.rsqrt(_head_sum(o * o, ones_bd) * (1.0 / HEAD_DIM) + RMS_EPS)
    g = g_ref[0]
    y_ref[0] = o * (g * jax.nn.sigmoid(g))


def _ret(proj3, s0, pos0, *, chunk):
    bsz, t_len, _ = proj3.shape
    cos_t, sin_t = _rope_tables(t_len, pos0)
    dmask, xi, zeta, g_chunk = _ret_tables(chunk)
    col = lambda k: pl.BlockSpec((1, chunk, BRANCH_W), lambda b, c: (b, c, COL_RET + k))
    const = lambda shape: pl.BlockSpec(shape, lambda b, c: (0,) * len(shape))
    tab = pl.BlockSpec((chunk, BRANCH_W), lambda b, c: (c, 0))
    y, s_bd = pl.pallas_call(
        _ret_kernel,
        grid=(bsz, t_len // chunk),
        in_specs=[col(0), col(1), col(2), col(3),
                  pl.BlockSpec((1, BRANCH_W, BRANCH_W), lambda b, c: (b, 0, 0)),
                  tab, tab, const((N_HEADS, chunk, chunk)), const((chunk, BRANCH_W)),
                  const((chunk, BRANCH_W)), const((1, BRANCH_W))],
        out_specs=[pl.BlockSpec((1, chunk, BRANCH_W), lambda b, c: (b, c, 0)),
                   pl.BlockSpec((1, BRANCH_W, BRANCH_W), lambda b, c: (b, 0, 0))],
        out_shape=[jax.ShapeDtypeStruct((bsz, t_len, BRANCH_W), F32),
                   jax.ShapeDtypeStruct((bsz, BRANCH_W, BRANCH_W), F32)],
        scratch_shapes=[pltpu.VMEM((BRANCH_W, BRANCH_W), F32)],
        compiler_params=_params(("arbitrary", "arbitrary")),
        name="retention",
    )(proj3, proj3, proj3, proj3, _block_diag_state(s0), cos_t, sin_t, dmask, xi, zeta, g_chunk)
    return y, _diag_blocks(s_bd)


def _topk_rows(gs_t, n_valid):
    nblk = gs_t.shape[0]
    blk = lax.broadcasted_iota(jnp.int32, gs_t.shape, 0)
    valid = blk < n_valid
    gsm = jnp.where(valid, gs_t, NEG_INF)
    cnt = jnp.zeros(gs_t.shape, F32)
    for m in range(nblk):
        row = gsm[m:m + 1, :]
        cnt = cnt + jnp.where(row > gsm, 1.0, jnp.where((row == gsm) & (blk > m), 1.0, 0.0))
    return jnp.where(valid & (cnt < MOBA_TOPK), 1.0, 0.0)


def _dot_nt3(a, b):
    ah = a.astype(BF16)
    al = (a - ah.astype(F32)).astype(BF16)
    bh = b.astype(BF16)
    bl = (b - bh.astype(F32)).astype(BF16)
    d = lambda x, y: lax.dot_general(x, y, (((1,), (1,)), ((), ())), preferred_element_type=F32)
    return d(ah, bh) + d(ah, bl) + d(al, bh)


def _moba_kernel(q_ref, k_ref, v_ref, o_ref, km_scr, kb_scr, vb_scr, *, nblk):
    qi = pl.program_id(1)
    bs = MOBA_BLOCK
    masks = _head_masks()

    @pl.when(qi == 0)
    def _():
        for n in range(nblk):
            rows = slice(n * bs, (n + 1) * bs)
            kblk = k_ref[0, rows, :]
            kb_scr[rows, :] = kblk.astype(BF16)
            vb_scr[rows, :] = v_ref[0, rows, :].astype(BF16)
            km = jnp.mean(kblk, axis=0, keepdims=True)
            for h, mh in enumerate(masks):
                km_scr[h * nblk + n:h * nblk + n + 1, :] = km * mh

    q = q_ref[0]
    scale = HEAD_DIM ** -0.5
    ri = lax.broadcasted_iota(jnp.int32, (bs, bs), 0)
    ci = lax.broadcasted_iota(jnp.int32, (bs, bs), 1)
    eye_b = (lax.broadcasted_iota(jnp.int32, (nblk, 128), 0)
             == lax.broadcasted_iota(jnp.int32, (nblk, 128), 1)).astype(F32)
    blk = lax.broadcasted_iota(jnp.int32, (1, 128), 1)
    own = pl.ds(pl.multiple_of(qi * bs, bs), bs)
    k_own = kb_scr[own, :]
    v_own = vb_scr[own, :]
    gs_all = _dot_nt3(km_scr[...], q)
    out = jnp.zeros((bs, BRANCH_W), F32)
    for h, mh in enumerate(masks):
        qh = (q * mh).astype(BF16)
        sel_t = _topk_rows(gs_all[h * nblk:(h + 1) * nblk], qi)
        sel = lax.dot_general(sel_t, eye_b, (((0,), (0,)), ((), ())),
                              preferred_element_type=F32)
        nt = lambda x, y: lax.dot_general(x, y, (((1,), (1,)), ((), ())), preferred_element_type=F32)
        s = jnp.where(ci <= ri, nt(qh, k_own) * scale, NEG_INF)
        m0 = jnp.max(s, axis=1, keepdims=True)
        p = jnp.exp(s - m0)
        l0 = jnp.sum(p, axis=1, keepdims=True)
        acc0 = jnp.dot(p.astype(BF16), v_own, preferred_element_type=F32)

        def body(n, carry, qh=qh, sel=sel, nt=nt):
            m, l, acc = carry
            rows = pl.ds(pl.multiple_of(n * bs, bs), bs)
            seln = jnp.sum(jnp.where(blk == n, sel, 0.0), axis=1, keepdims=True)
            s = jnp.where(seln > 0.0, nt(qh, kb_scr[rows, :]) * scale, NEG_INF)
            m_new = jnp.maximum(m, jnp.max(s, axis=1, keepdims=True))
            alpha = jnp.exp(m - m_new)
            p = jnp.exp(s - m_new)
            l = alpha * l + jnp.sum(p, axis=1, keepdims=True)
            acc = alpha * acc + jnp.dot(p.astype(BF16), vb_scr[rows, :], preferred_element_type=F32)
            return m_new, l, acc

        _, l, acc = lax.fori_loop(0, qi, body, (m0, l0, acc0))
        out = out + mh * (acc / l)
    o_ref[0] = out


def _moba_prompt(proj3):
    bsz, t_len, _ = proj3.shape
    nblk = t_len // MOBA_BLOCK
    full = lambda k: pl.BlockSpec((1, t_len, BRANCH_W), lambda b, i: (b, 0, COL_MOBA + k))
    return pl.pallas_call(
        functools.partial(_moba_kernel, nblk=nblk),
        grid=(bsz, nblk),
        in_specs=[pl.BlockSpec((1, MOBA_BLOCK, BRANCH_W), lambda b, i: (b, i, COL_MOBA)), full(1), full(2)],
        out_specs=pl.BlockSpec((1, MOBA_BLOCK, BRANCH_W), lambda b, i: (b, i, 0)),
        out_shape=jax.ShapeDtypeStruct((bsz, t_len, BRANCH_W), F32),
        scratch_shapes=[pltpu.VMEM((N_HEADS * nblk, BRANCH_W), F32), pltpu.VMEM((t_len, BRANCH_W), BF16),
                        pltpu.VMEM((t_len, BRANCH_W), BF16)],
        compiler_params=_params(("arbitrary", "arbitrary")),
        name="moba_prompt",
    )(proj3, proj3, proj3)


def _merge_kernel(x_ref, y0_ref, y1_ref, y2_ref, y3_ref, g0_ref, g1_ref, g2_ref, g3_ref, gt_ref,
                  wb_ref, wo_ref, o_ref, wb_scr, wo_scr):
    @pl.when(pl.program_id(0) == 0)
    def _():
        wb_scr[...] = wb_ref[...].astype(BF16)
        wo_scr[...] = wo_ref[...].astype(BF16)

    mixed = None
    for g, (y_ref, g_ref) in enumerate(((y0_ref, g0_ref), (y1_ref, g1_ref), (y2_ref, g2_ref), (y3_ref, g3_ref))):
        up = jnp.dot(y_ref[...].astype(BF16), wb_scr[g], preferred_element_type=F32)
        term = jax.nn.sigmoid(g_ref[...]) * up
        mixed = term if mixed is None else mixed + term
    o_ref[...] = x_ref[...] + gt_ref[0] * jnp.dot(mixed.astype(BF16), wo_scr[...], preferred_element_type=F32)


def _merge(x2d, y_s5, s5_spec, y_rwkv, y_ret, y_moba, proj, mod3, w_branch, w_out, *, tm, tiles_per_group):
    rows = x2d.shape[0]
    m = mod3.shape[1]
    tpg = tiles_per_group
    ysp = pl.BlockSpec((tm, BRANCH_W), lambda i: (i, 0))
    gate = lambda g: pl.BlockSpec((tm, D_MODEL), lambda i: (i, COL_GATE // N_BRANCH + g))
    return pl.pallas_call(
        _merge_kernel,
        grid=(rows // tm,),
        in_specs=[pl.BlockSpec((tm, D_MODEL), lambda i: (i, 0)), s5_spec, ysp, ysp, ysp,
                  gate(0), gate(1), gate(2), gate(3),
                  pl.BlockSpec((1, m, D_MODEL), lambda i: (i // tpg, 0, 2)),
                  pl.BlockSpec((N_BRANCH, BRANCH_W, D_MODEL), lambda i: (0, 0, 0)),
                  pl.BlockSpec((D_MODEL, D_MODEL), lambda i: (0, 0))],
        out_specs=pl.BlockSpec((tm, D_MODEL), lambda i: (i, 0)),
        out_shape=jax.ShapeDtypeStruct((rows, D_MODEL), F32),
        scratch_shapes=[pltpu.VMEM((N_BRANCH, BRANCH_W, D_MODEL), BF16), pltpu.VMEM((D_MODEL, D_MODEL), BF16)],
        compiler_params=_params(("arbitrary",)),
        name="merge",
    )(x2d, y_s5, y_rwkv, y_ret, y_moba, proj, proj, proj, proj, mod3, w_branch, w_out)


def _router_combine(h, router):
    logits = _dot_hi(h, router)
    lane = lax.broadcasted_iota(jnp.int32, logits.shape, 1)
    logits = jnp.where(lane < N_EXPERTS, logits, NEG_INF)
    m1 = jnp.max(logits, axis=1, keepdims=True)
    i1 = jnp.min(jnp.where(logits == m1, lane, ROUTER_PAD), axis=1, keepdims=True)
    rest = jnp.where(lane == i1, NEG_INF, logits)
    m2 = jnp.max(rest, axis=1, keepdims=True)
    i2 = jnp.min(jnp.where(rest == m2, lane, ROUTER_PAD), axis=1, keepdims=True)
    e2 = jnp.exp(m2 - m1)
    den = 1.0 + e2
    comb = jnp.where(lane == i1, 1.0 / den, 0.0) + jnp.where(lane == i2, e2 / den, 0.0)
    return comb, jnp.where((lane == i1) | (lane == i2), 1.0, 0.0)


def _swiglu(hb, w1, w3, w2):
    a = jnp.dot(hb, w1, preferred_element_type=F32)
    b = jnp.dot(hb, w3, preferred_element_type=F32)
    act = (a * jax.nn.sigmoid(a)) * b
    return jnp.dot(act.astype(BF16), w2, preferred_element_type=F32)


def _ffn_kernel(x_ref, g_ref, sc_ref, sh_ref, gt_ref, w1_ref, w3_ref, w2_ref, o_ref, h_scr, acc_scr, *, n_j):
    j = pl.program_id(1)

    @pl.when(j == 0)
    def _():
        h_scr[...] = _modulated_norm(x_ref[...], g_ref[...], sc_ref[0], sh_ref[0]).astype(BF16)
        acc_scr[...] = jnp.zeros(acc_scr.shape, F32)

    acc_scr[...] += _swiglu(h_scr[...], w1_ref[...].astype(BF16), w3_ref[...].astype(BF16),
                            w2_ref[...].astype(BF16))

    @pl.when(j == n_j - 1)
    def _():
        o_ref[...] = x_ref[...] + gt_ref[0] * acc_scr[...]


def _ffn(x2d, g, mod3, weights, *, tm, tiles_per_group, tf):
    rows = x2d.shape[0]
    m = mod3.shape[1]
    tpg = tiles_per_group
    w1, w3, w2 = weights
    n_j = w1.shape[1] // tf
    modspec = lambda k: pl.BlockSpec((1, m, D_MODEL), lambda i, j: (i // tpg, 0, k))
    return pl.pallas_call(
        functools.partial(_ffn_kernel, n_j=n_j),
        grid=(rows // tm, n_j),
        in_specs=[pl.BlockSpec((tm, D_MODEL), lambda i, j: (i, 0)),
                  pl.BlockSpec((1, D_MODEL), lambda i, j: (0, 0)),
                  modspec(4), modspec(3), modspec(5),
                  pl.BlockSpec((D_MODEL, tf), lambda i, j: (0, j)),
                  pl.BlockSpec((D_MODEL, tf), lambda i, j: (0, j)),
                  pl.BlockSpec((tf, D_MODEL), lambda i, j: (j, 0))],
        out_specs=pl.BlockSpec((tm, D_MODEL), lambda i, j: (i, 0)),
        out_shape=jax.ShapeDtypeStruct((rows, D_MODEL), F32),
        scratch_shapes=[pltpu.VMEM((tm, D_MODEL), BF16), pltpu.VMEM((tm, D_MODEL), F32)],
        compiler_params=_params(("arbitrary", "arbitrary")),
        name="dense_ffn",
    )(x2d, g.reshape(1, D_MODEL), mod3, mod3, mod3, w1, w3, w2)


def _moe_kernel(x_ref, g_ref, sc_ref, sh_ref, gt_ref, rt_ref, w1_ref, w3_ref, w2_ref, o_ref,
                h_scr, acc_scr, comb_scr, asg_scr, rank_scr, *, n_e, cap):
    e = pl.program_id(1)
    tm = x_ref.shape[0]

    @pl.when(e == 0)
    def _():
        h = _modulated_norm(x_ref[...], g_ref[...], sc_ref[0], sh_ref[0])
        h_scr[...] = h.astype(BF16)
        acc_scr[...] = jnp.zeros(acc_scr.shape, F32)
        comb, asg = _router_combine(h, rt_ref[...])
        comb_scr[...] = comb
        asg_scr[...] = asg
        below = (lax.broadcasted_iota(jnp.int32, (tm, tm), 0) > lax.broadcasted_iota(jnp.int32, (tm, tm), 1))
        rank_scr[...] = jnp.dot(below.astype(BF16), asg.astype(BF16), preferred_element_type=F32)

    lane = lax.broadcasted_iota(jnp.int32, (tm, ROUTER_PAD), 1)
    pick = lambda ref: jnp.sum(jnp.where(lane == e, ref[...], 0.0), axis=1, keepdims=True)
    a_col = pick(asg_scr)
    r_col = pick(rank_scr)
    c_col = pick(comb_scr)
    count = jnp.sum(a_col).astype(jnp.int32)
    slot = lax.broadcasted_iota(jnp.int32, (1, cap), 1).astype(F32)
    w1 = w1_ref[0]
    w3 = w3_ref[0]
    w2 = w2_ref[0]

    def one_pass(pi, carry):
        base = (pi * cap).astype(F32)
        sel = jnp.where((r_col - base == slot) & (a_col > 0.0), 1.0, 0.0).astype(BF16)
        packed = lax.dot_general(sel, h_scr[...], (((0,), (0,)), ((), ())), preferred_element_type=F32)
        y = _swiglu(packed.astype(BF16), w1, w3, w2)
        y_hi = y.astype(BF16)
        y_lo = (y - y_hi.astype(F32)).astype(BF16)
        spread = (jnp.dot(sel, y_hi, preferred_element_type=F32)
                  + jnp.dot(sel, y_lo, preferred_element_type=F32))
        acc_scr[...] += c_col * spread
        return carry

    lax.fori_loop(0, (count + cap - 1) // cap, one_pass, 0)

    @pl.when(e == n_e - 1)
    def _():
        o_ref[...] = x_ref[...] + gt_ref[0] * acc_scr[...]


def _moe(x2d, g, mod3, weights, *, tm, tiles_per_group, cap):
    rows = x2d.shape[0]
    m = mod3.shape[1]
    tpg = tiles_per_group
    router, w1, w3, w2 = weights
    router = jnp.zeros((D_MODEL, ROUTER_PAD), F32).at[:, :N_EXPERTS].set(router)
    n_e, _, d_ff = w1.shape
    modspec = lambda k: pl.BlockSpec((1, m, D_MODEL), lambda i, e: (i // tpg, 0, k))
    return pl.pallas_call(
        functools.partial(_moe_kernel, n_e=n_e, cap=cap),
        grid=(rows // tm, n_e),
        in_specs=[pl.BlockSpec((tm, D_MODEL), lambda i, e: (i, 0)),
                  pl.BlockSpec((1, D_MODEL), lambda i, e: (0, 0)),
                  modspec(4), modspec(3), modspec(5),
                  pl.BlockSpec((D_MODEL, ROUTER_PAD), lambda i, e: (0, 0)),
                  pl.BlockSpec((1, D_MODEL, d_ff), lambda i, e: (e, 0, 0)),
                  pl.BlockSpec((1, D_MODEL, d_ff), lambda i, e: (e, 0, 0)),
                  pl.BlockSpec((1, d_ff, D_MODEL), lambda i, e: (e, 0, 0))],
        out_specs=pl.BlockSpec((tm, D_MODEL), lambda i, e: (i, 0)),
        out_shape=jax.ShapeDtypeStruct((rows, D_MODEL), F32),
        scratch_shapes=[pltpu.VMEM((tm, D_MODEL), BF16), pltpu.VMEM((tm, D_MODEL), F32),
                        pltpu.VMEM((tm, ROUTER_PAD), F32), pltpu.VMEM((tm, ROUTER_PAD), F32),
                        pltpu.VMEM((tm, ROUTER_PAD), F32)],
        compiler_params=_params(("arbitrary", "arbitrary")),
        name="moe_ffn",
    )(x2d, g.reshape(1, D_MODEL), mod3, mod3, mod3, router, w1, w3, w2)


N_PRE = 9


def _dec_pre_kernel(r_ref, k_ref, v_ref, l_ref, q_ref, kr_ref, shift_ref, mu_ref, w0_ref, w2_ref, a0_ref,
                    a2_ref, g2_ref, kk_ref, ka_ref, cos_ref, sin_ref, o_ref):
    pieces = []
    for idx, ref in enumerate((r_ref, k_ref, v_ref, l_ref)):
        lanes = slice(idx * BRANCH_W, (idx + 1) * BRANCH_W)
        x = ref[...]
        pieces.append(x + (shift_ref[:, lanes] - x) * mu_ref[:, lanes])
    ones_bd = _head_ones()
    prm = (w0_ref[...], w2_ref[...], a0_ref[...], a2_ref[...], g2_ref[...], kk_ref[...], ka_ref[...])
    r, k_mod, v, lw, aa, bb, g = _rwkv_features(*pieces, prm, ones_bd)
    q_r = _rope(q_ref[...], cos_ref[...], sin_ref[...])
    k_r = _rope(kr_ref[...], cos_ref[...], sin_ref[...]) * (HEAD_DIM ** -0.5)
    for idx, val in enumerate((r, k_mod, v, jnp.exp(lw), aa, bb, g, q_r, k_r)):
        o_ref[:, idx * BRANCH_W:(idx + 1) * BRANCH_W] = val


def _dec_pre(proj, shift0, rp, pos0):
    rows = proj.shape[0]
    cos_t, sin_t = _rope_tables(1, pos0)
    col = lambda k: pl.BlockSpec((rows, BRANCH_W), lambda i: (0, k))
    const = lambda shape: pl.BlockSpec(shape, lambda i: (0,) * len(shape))
    vec = const((1, BRANCH_W))
    mat = const((BRANCH_W, BRANCH_W))
    mu, w0, w2p, a0, a2p, g2p, k_k, k_a = rp[:8]
    return pl.pallas_call(
        _dec_pre_kernel,
        grid=(1,),
        in_specs=[col(COL_RWKV), col(COL_RWKV + 1), col(COL_RWKV + 2), col(COL_RWKV + 3),
                  col(COL_RET), col(COL_RET + 1), const((rows, RWKV_IN_W)), const((1, RWKV_IN_W)),
                  vec, mat, vec, mat, mat, vec, vec, vec, vec],
        out_specs=const((rows, N_PRE * BRANCH_W)),
        out_shape=jax.ShapeDtypeStruct((rows, N_PRE * BRANCH_W), F32),
        compiler_params=_params(("arbitrary",)),
        name="dec_pre",
    )(proj, proj, proj, proj, proj, proj, shift0, mu, w0, w2p, a0, a2p, g2p, k_k, k_a, cos_t, sin_t)


def _dec_state_kernel(sw_ref, aa_ref, w_ref, bb_ref, km_ref, r_ref, vv_ref,
                      sr_ref, q_ref, kc_ref, vr_ref, gm_ref,
                      sw_out, y_out, sr_out, o_out):
    s = sw_ref[...]
    sa = jnp.sum(s * aa_ref[...], axis=-1, keepdims=True)
    s = s * w_ref[...] + sa * bb_ref[...] + vv_ref[...] * km_ref[...]
    sw_out[...] = s
    y_out[...] = jnp.sum(s * r_ref[...], axis=-1, keepdims=True)
    t = sr_ref[...]
    q = q_ref[...]
    kc = kc_ref[...]
    vr = vr_ref[...]
    gm = gm_ref[...]
    inter = jnp.sum(q * t, axis=1, keepdims=True) * gm
    att = jnp.sum(q * kc, axis=1, keepdims=True)
    o_out[...] = att * vr + inter
    sr_out[...] = t * gm + kc * vr


def _dec_state(s_rwkv, s_ret, pre, v_ret):
    bsz = s_rwkv.shape[0]
    nbh = bsz * N_HEADS
    tb = 64
    piece = lambda k: pre[:, k * BRANCH_W:(k + 1) * BRANCH_W]
    as_row = lambda t: t.reshape(nbh, 1, HEAD_DIM)
    as_col = lambda t: t.reshape(nbh, HEAD_DIM, 1)
    log_gamma = jnp.log(1.0 - jnp.exp2(-5.0 - jnp.arange(N_HEADS, dtype=F32)))
    gamma = jnp.broadcast_to(jnp.exp(log_gamma)[None, :, None, None], (bsz, N_HEADS, 1, HEAD_DIM))
    mat = pl.BlockSpec((tb, HEAD_DIM, HEAD_DIM), lambda i: (i, 0, 0))
    row = pl.BlockSpec((tb, 1, HEAD_DIM), lambda i: (i, 0, 0))
    colv = pl.BlockSpec((tb, HEAD_DIM, 1), lambda i: (i, 0, 0))
    sw, y, sr, o = pl.pallas_call(
        _dec_state_kernel,
        grid=(nbh // tb,),
        in_specs=[mat, row, row, row, row, row, colv, mat, colv, colv, row, row],
        out_specs=[mat, colv, mat, row],
        out_shape=[jax.ShapeDtypeStruct((nbh, HEAD_DIM, HEAD_DIM), F32),
                   jax.ShapeDtypeStruct((nbh, HEAD_DIM, 1), F32),
                   jax.ShapeDtypeStruct((nbh, HEAD_DIM, HEAD_DIM), F32),
                   jax.ShapeDtypeStruct((nbh, 1, HEAD_DIM), F32)],
        compiler_params=_params(("arbitrary",)),
        name="dec_state",
    )(s_rwkv.reshape(nbh, HEAD_DIM, HEAD_DIM), as_row(piece(4)), as_row(piece(3)), as_row(piece(5)),
      as_row(piece(1)), as_row(piece(0)), as_col(piece(2)),
      s_ret.reshape(nbh, HEAD_DIM, HEAD_DIM), as_col(piece(7)), as_col(piece(8)), as_row(v_ret),
      gamma.reshape(nbh, 1, HEAD_DIM))
    shape4 = (bsz, N_HEADS, HEAD_DIM, HEAD_DIM)
    return sw.reshape(shape4), y.reshape(bsz, BRANCH_W), sr.reshape(shape4), o.reshape(bsz, BRANCH_W)


def _dec_post_kernel(y_ref, r_ref, km_ref, v_ref, g_ref, o_ref, gr_ref, rk_ref, lng_ref, lnb_ref,
                     yw_out, yr_out):
    ones_bd = _head_ones()
    yw_out[...] = _rwkv_post(y_ref[...], r_ref[...], km_ref[...], v_ref[...], g_ref[...],
                             rk_ref[...], lng_ref[...], lnb_ref[...], ones_bd)
    o = o_ref[...]
    o = o * lax.rsqrt(_head_sum(o * o, ones_bd) * (1.0 / HEAD_DIM) + RMS_EPS)
    g = gr_ref[...]
    yr_out[...] = o * (g * jax.nn.sigmoid(g))


def _dec_post(y_rwkv, pre, o_ret, proj, rp):
    rows = proj.shape[0]
    blk = pl.BlockSpec((rows, BRANCH_W), lambda i: (0, 0))
    col = lambda k: pl.BlockSpec((rows, BRANCH_W), lambda i: (0, k))
    vec = pl.BlockSpec((1, BRANCH_W), lambda i: (0, 0))
    r_k, ln_g, ln_b = rp[8:]
    return pl.pallas_call(
        _dec_post_kernel,
        grid=(1,),
        in_specs=[blk, col(0), col(1), col(2), col(6), blk, col(COL_RET + 3), vec, vec, vec],
        out_specs=[blk, blk],
        out_shape=[jax.ShapeDtypeStruct((rows, BRANCH_W), F32)] * 2,
        compiler_params=_params(("arbitrary",)),
        name="dec_post",
    )(y_rwkv, pre, pre, pre, pre, o_ret, proj, r_k, ln_g, ln_b)


def _moba_dec_kernel(pt_ref, q_ref, kn_ref, vn_ref, *refs, n_pages):
    del pt_ref
    k_refs = refs[:n_pages]
    v_refs = refs[n_pages:2 * n_pages]
    o_ref, sc_scr = refs[2 * n_pages:]
    page = k_refs[0].shape[-1]
    per = MOBA_BLOCK // page
    nblk = n_pages // per
    half = N_HEADS * nblk
    scale = HEAD_DIM ** -0.5
    for h in range(N_HEADS):
        q_c = q_ref[0, h]
        for pg in range(n_pages):
            row = (pg % per) * half + h * nblk + pg // per
            sc_scr[row:row + 1, :] = jnp.sum(k_refs[pg][0, 0, h] * q_c, axis=0, keepdims=True)
    raw = sc_scr[...]
    rs = jnp.sum(raw, axis=1, keepdims=True)
    gate = (rs[:half] + rs[half:]) * (1.0 / MOBA_BLOCK)
    ri = lax.broadcasted_iota(jnp.int32, (half, half), 0)
    ci = lax.broadcasted_iota(jnp.int32, (half, half), 1)
    g_self = jnp.broadcast_to(gate, (half, half))
    g_other = _dot_hi(jnp.ones((half, half), F32), jnp.where(ri == ci, g_self, 0.0))
    beats = jnp.where(g_other > g_self, 1.0, jnp.where((g_other == g_self) & (ci < ri), 1.0, 0.0))
    beats = jnp.where(ri // nblk == ci // nblk, beats, 0.0)
    sel = jnp.sum(beats, axis=1, keepdims=True) < MOBA_TOPK
    sel2 = jnp.concatenate([sel.astype(F32)] * per, axis=0) > 0.0
    masked = jnp.where(sel2, raw * scale, NEG_INF)
    for h in range(N_HEADS):
        q_c = q_ref[0, h]
        s_own = jnp.sum(q_c * kn_ref[0, h], axis=0, keepdims=True) * scale
        parts = [masked[par * half + h * nblk:par * half + (h + 1) * nblk] for par in range(per)]
        m = s_own
        for part in parts:
            m = jnp.maximum(m, jnp.max(jnp.max(part, axis=1, keepdims=True), axis=0, keepdims=True))
        p_own = jnp.exp(s_own - m)
        l = p_own
        acc = jnp.zeros((HEAD_DIM, page), F32)
        for par, part in enumerate(parts):
            p = jnp.exp(part - m)
            l = l + jnp.sum(jnp.sum(p, axis=1, keepdims=True), axis=0, keepdims=True)
            for n in range(nblk):
                acc = acc + p[n:n + 1, :] * v_refs[n * per + par][0, 0, h]
        o = jnp.sum(acc, axis=1, keepdims=True) + p_own * vn_ref[0, h]
        o_ref[0, h] = o / l


def _moba_dec(q, k_new, v_new, k_t, v_t, page_table, layer):
    bsz, n_pages = page_table.shape
    page = k_t.shape[-1]
    cols = lambda t: t.reshape(bsz, N_HEADS, HEAD_DIM, 1)
    qspec = pl.BlockSpec((1, N_HEADS, HEAD_DIM, 1), lambda b, pt: (b, 0, 0, 0))
    pspec = lambda pg: pl.BlockSpec((1, 1, N_HEADS, HEAD_DIM, page), lambda b, pt: (layer, pt[b, pg], 0, 0, 0))
    out = pl.pallas_call(
        functools.partial(_moba_dec_kernel, n_pages=n_pages),
        grid_spec=pltpu.PrefetchScalarGridSpec(
            num_scalar_prefetch=1,
            grid=(bsz,),
            in_specs=[qspec, qspec, qspec] + [pspec(pg) for pg in range(n_pages)] * 2,
            out_specs=qspec,
            scratch_shapes=[pltpu.VMEM((N_HEADS * n_pages, page), F32)]),
        out_shape=jax.ShapeDtypeStruct((bsz, N_HEADS, HEAD_DIM, 1), F32),
        compiler_params=_params(("arbitrary",)),
        name="moba_decode",
    )(page_table, cols(q), cols(k_new), cols(v_new), *([k_t] * n_pages), *([v_t] * n_pages))
    return out.reshape(bsz, BRANCH_W)


def _final_norm_kernel(x_ref, g_ref, o_ref):
    x = x_ref[...]
    o_ref[...] = x * lax.rsqrt(jnp.mean(x * x, axis=-1, keepdims=True) + RMS_EPS) * g_ref[...]


def _final_norm(x2d, g, *, tm):
    rows = x2d.shape[0]
    return pl.pallas_call(
        _final_norm_kernel,
        grid=(rows // tm,),
        in_specs=[pl.BlockSpec((tm, D_MODEL), lambda i: (i, 0)), pl.BlockSpec((1, D_MODEL), lambda i: (0, 0))],
        out_specs=pl.BlockSpec((tm, D_MODEL), lambda i: (i, 0)),
        out_shape=jax.ShapeDtypeStruct((rows, D_MODEL), F32),
        compiler_params=_params(("arbitrary",)),
        name="final_norm",
    )(x2d, g.reshape(1, D_MODEL))


RWKV_BATCHES_PER_STEP = 2
RET_CHUNK = 256
S5_TIME_CHUNK = 128


MOE_TILE = 512
MOE_CAP_FRACTION = 3


def _ffn_any(x2d, g, mod3, ffn, *, rows_per_group):
    if len(ffn) == 3:
        tm = min(1024, rows_per_group)
        return _ffn(x2d, g, mod3, ffn, tm=tm, tiles_per_group=rows_per_group // tm, tf=256)
    tm = min(MOE_TILE, rows_per_group)
    return _moe(x2d, g, mod3, ffn, tm=tm, tiles_per_group=rows_per_group // tm,
                cap=tm * MOE_CAP_FRACTION // 8)


def _prompt_layer(x2d, bsz, t_len, mod_l, lp):
    mod3 = mod_l.reshape(bsz, 1, -1)
    tm = min(2048, t_len)
    tpg = t_len // tm
    proj, u_tb = _inproj(x2d, lp['norm_mix'], mod3, lp['w_in'], tm=tm, tiles_per_group=tpg,
                         tb_shape=(t_len, bsz * BRANCH_W))
    proj3 = proj.reshape(bsz, t_len, IN_W)
    z_state = jnp.zeros((bsz, S5_W), F32)
    y_s5, s5_re, s5_im = _s5(u_tb.reshape(t_len * bsz, BRANCH_W), z_state, z_state, lp['s5p'], lp['s5_d'],
                             lp['s5_w_glu'], nb=bsz, t_len=t_len, tc=min(S5_TIME_CHUNK, t_len))
    z_mat = jnp.zeros((bsz, N_HEADS, HEAD_DIM, HEAD_DIM), F32)
    y_rwkv, s_rwkv, shift_n = _rwkv(proj3, jnp.zeros((bsz, RWKV_IN_W), F32), z_mat, lp['rwkv'],
                                    nb=RWKV_BATCHES_PER_STEP)
    y_ret, s_ret = _ret(proj3, z_mat, 0, chunk=min(RET_CHUNK, t_len))
    y_moba = _moba_prompt(proj3)
    kv = lambda k: proj3[:, :, (COL_MOBA + k) * BRANCH_W:(COL_MOBA + k + 1) * BRANCH_W].reshape(
        bsz, t_len, N_HEADS, HEAD_DIM)
    tmm = min(256, t_len)
    tpm = t_len // tmm
    s5_spec = pl.BlockSpec((tmm, BRANCH_W), lambda i: (i % tpm, i // tpm))
    flat = lambda y: y.reshape(bsz * t_len, BRANCH_W)
    x2d = _merge(x2d, y_s5.reshape(t_len, bsz * BRANCH_W), s5_spec, flat(y_rwkv), flat(y_ret), flat(y_moba),
                 proj, mod3, lp['w_branch'], lp['w_out'], tm=tmm, tiles_per_group=tpm)
    x2d = _ffn_any(x2d, lp['norm_ffn'], mod3, lp['ffn'], rows_per_group=t_len)
    g16 = (bsz, S5_GROUPS, S5_STATE)
    return x2d, (s5_re.reshape(g16), s5_im.reshape(g16), s_rwkv, shift_n, s_ret, kv(1), kv(2))


def _decode_layer(x2d, mod_l, lp, layer, pos0, s5_re0, s5_im0, rwkv_s0, shift0, ret_s0, cache_k, cache_v,
                  page_table):
    bsz = x2d.shape[0]
    mod3 = mod_l.reshape(1, bsz, -1)
    proj = _inproj(x2d, lp['norm_mix'], mod3, lp['w_in'], tm=bsz, tiles_per_group=1)[0]
    piece = lambda k: proj[:, k * BRANCH_W:(k + 1) * BRANCH_W]
    y_s5, s5_re, s5_im = _s5(piece(COL_S5), s5_re0.reshape(bsz, S5_W), s5_im0.reshape(bsz, S5_W), lp['s5p'],
                             lp['s5_d'], lp['s5_w_glu'], nb=bsz, t_len=1, tc=1)
    pre = _dec_pre(proj, shift0, lp['rwkv'], pos0)
    s_rwkv, y_raw, s_ret, o_raw = _dec_state(rwkv_s0, ret_s0, pre, piece(COL_RET + 2))
    y_rwkv, y_ret = _dec_post(y_raw, pre, o_raw, proj, lp['rwkv'])
    k_new = piece(COL_MOBA + 1)
    v_new = piece(COL_MOBA + 2)
    y_moba = _moba_dec(piece(COL_MOBA), k_new, v_new, cache_k, cache_v, page_table, layer)
    s5_spec = pl.BlockSpec((bsz, BRANCH_W), lambda i: (i, 0))
    x2d = _merge(x2d, y_s5, s5_spec, y_rwkv, y_ret, y_moba, proj, mod3, lp['w_branch'], lp['w_out'],
                 tm=bsz, tiles_per_group=1)
    x2d = _ffn_any(x2d, lp['norm_ffn'], mod3, lp['ffn'], rows_per_group=bsz)
    g16 = (bsz, S5_GROUPS, S5_STATE)
    kv4 = lambda t: t.reshape(bsz, 1, N_HEADS, HEAD_DIM)
    shift_n = proj[:, COL_RWKV * BRANCH_W:COL_RWKV * BRANCH_W + RWKV_IN_W]
    return x2d, (s5_re.reshape(g16), s5_im.reshape(g16), s_rwkv, shift_n, s_ret, kv4(k_new), kv4(v_new))


def kernel(x_prompt, x_sample, c_prompt, c_sample, state_s5_re, state_s5_im, state_rwkv, state_rwkv_shift, state_ret, cache_moba_k, cache_moba_v, page_table, norm_mix_g, norm_ffn_g, norm_final_g, w_ada, b_ada, w_in, s5_lambda_re, s5_lambda_im, s5_log_dt, s5_b_re, s5_b_im, s5_c_re, s5_c_im, s5_d, s5_w_glu, rwkv_mu, rwkv_w0, rwkv_w2, rwkv_a0, rwkv_a2, rwkv_g2, rwkv_k_k, rwkv_k_a, rwkv_r_k, rwkv_ln_g, rwkv_ln_b, w_branch, w_out, ffn_w1, ffn_w3, ffn_w2, moe_router, moe_w1, moe_w3, moe_w2):
    bp, t_len, _ = x_prompt.shape
    bs = x_sample.shape[0]
    depth = w_in.shape[0]
    past_len = page_table.shape[1] * cache_moba_k.shape[2]
    cache_kt = jnp.transpose(cache_moba_k, (0, 1, 3, 4, 2))
    cache_vt = jnp.transpose(cache_moba_v, (0, 1, 3, 4, 2))
    mod_all = _ada(jnp.concatenate([c_prompt, c_sample], axis=0), w_ada, b_ada)
    xp = x_prompt.reshape(bp * t_len, D_MODEL)
    xs = x_sample.reshape(bs, D_MODEL)
    outs_p = [[] for _ in range(7)]
    outs_s = [[] for _ in range(7)]
    for l in range(depth):
        if l % 2 == 0:
            ffn = (ffn_w1[l // 2], ffn_w3[l // 2], ffn_w2[l // 2])
        else:
            ffn = (moe_router[l // 2], moe_w1[l // 2].astype(BF16), moe_w3[l // 2].astype(BF16),
                   moe_w2[l // 2].astype(BF16))
        lp = {
            'norm_mix': norm_mix_g[l], 'norm_ffn': norm_ffn_g[l], 'w_in': w_in[l],
            's5p': _s5_params(s5_lambda_re[l], s5_lambda_im[l], s5_log_dt[l], s5_b_re[l], s5_b_im[l],
                              s5_c_re[l], s5_c_im[l]),
            's5_d': s5_d[l], 's5_w_glu': s5_w_glu[l],
            'rwkv': _rwkv_params(rwkv_mu[l], rwkv_w0[l], rwkv_w2[l], rwkv_a0[l], rwkv_a2[l], rwkv_g2[l],
                                 rwkv_k_k[l], rwkv_k_a[l], rwkv_r_k[l], rwkv_ln_g[l], rwkv_ln_b[l]),
            'w_branch': w_branch[l], 'w_out': w_out[l], 'ffn': ffn,
        }
        xp, st_p = _prompt_layer(xp, bp, t_len, mod_all[l, :bp], lp)
        xs, st_s = _decode_layer(xs, mod_all[l, bp:], lp, l, past_len, state_s5_re[l], state_s5_im[l],
                                 state_rwkv[l], state_rwkv_shift[l], state_ret[l], cache_kt, cache_vt,
                                 page_table)
        for j in range(7):
            outs_p[j].append(st_p[j])
            outs_s[j].append(st_s[j])
    y_prompt = _final_norm(xp, norm_final_g, tm=1024).reshape(bp, t_len, D_MODEL)
    y_sample = _final_norm(xs, norm_final_g, tm=bs).reshape(bs, 1, D_MODEL)
    stack = lambda outs: [jnp.stack(o, axis=0) for o in outs]
    return (y_prompt, y_sample, *stack(outs_p), *stack(outs_s))
```

```python
import functools
import math

import jax
import jax.numpy as jnp
from jax import lax
from jax.experimental import pallas as pl
from jax.experimental.pallas import tpu as pltpu

F32 = jnp.float32
BF16 = jnp.bfloat16
HIGHEST = lax.Precision.HIGHEST

D_MODEL = 1024
BRANCH_W = 256
HEAD_DIM = 64
N_HEADS = 4
N_BRANCH = 4
S5_GROUPS = 16
S5_STATE = 64
S5_CH = 16
S5_W = S5_GROUPS * S5_STATE
IN_W = 7168
RWKV_IN_W = 1024
RWKV_LN_EPS = 64e-5
RMS_EPS = 1e-6
ROPE_BASE = 10000.0
MOBA_BLOCK = 256
MOBA_TOPK = 3
N_EXPERTS = 8
ROUTER_PAD = 128
NEG_INF = float("-inf")

COL_S5 = 0
COL_RWKV = 1
COL_RET = 5
COL_MOBA = 9
COL_GATE = 12

VMEM_LIMIT = 48 * 1024 * 1024


def _params(sem):
    return pltpu.CompilerParams(dimension_semantics=sem, vmem_limit_bytes=VMEM_LIMIT)


def _dot(a, b):
    return jnp.dot(a.astype(BF16), b.astype(BF16), preferred_element_type=F32)


def _dot_hi(a, b):
    return jnp.dot(a, b, precision=HIGHEST, preferred_element_type=F32)


def _dot_nt(a, b):
    return lax.dot_general(a.astype(BF16), b.astype(BF16), (((1,), (1,)), ((), ())),
                           preferred_element_type=F32)


def _dot_nt_hi(a, b):
    return lax.dot_general(a, b, (((1,), (1,)), ((), ())), precision=HIGHEST,
                           preferred_element_type=F32)


def _dot_tn(a, b):
    return lax.dot_general(a.astype(BF16), b.astype(BF16), (((0,), (0,)), ((), ())),
                           preferred_element_type=F32)


def _head_masks(width=BRANCH_W):
    lane = lax.broadcasted_iota(jnp.int32, (1, width), 1)
    return [(lane // HEAD_DIM == h).astype(F32) for h in range(N_HEADS)]


def _head_ones():
    r = lax.broadcasted_iota(jnp.int32, (BRANCH_W, BRANCH_W), 0) // HEAD_DIM
    c = lax.broadcasted_iota(jnp.int32, (BRANCH_W, BRANCH_W), 1) // HEAD_DIM
    return (r == c).astype(F32)


def _split3(x):
    hi = x.astype(BF16)
    rest = x - hi.astype(F32)
    mid = rest.astype(BF16)
    return hi, mid, (rest - mid.astype(F32)).astype(BF16)


def _head_sum(x, ones_bd):
    w = ones_bd.astype(BF16)
    return sum(jnp.dot(p, w, preferred_element_type=F32) for p in _split3(x))


def _ada_kernel(c_ref, w_ref, b_ref, o_ref):
    c = c_ref[...]
    h = c * jax.nn.sigmoid(c)
    o_ref[0] = _dot(h, w_ref[0]) + b_ref[0]


def _ada(c_all, w_ada, b_ada):
    depth, _, width = w_ada.shape
    rows = c_all.shape[0]
    tn = 1024
    return pl.pallas_call(
        _ada_kernel,
        grid=(depth, width // tn),
        in_specs=[pl.BlockSpec((rows, D_MODEL), lambda l, j: (0, 0)),
                  pl.BlockSpec((1, D_MODEL, tn), lambda l, j: (l, 0, j)),
                  pl.BlockSpec((1, 1, tn), lambda l, j: (l, 0, j))],
        out_specs=pl.BlockSpec((1, rows, tn), lambda l, j: (l, 0, j)),
        out_shape=jax.ShapeDtypeStruct((depth, rows, width), F32),
        compiler_params=_params(("arbitrary", "arbitrary")),
        name="ada_mod",
    )(c_all, w_ada, b_ada.reshape(depth, 1, width))


def _modulated_norm(x, g, sc, sh):
    y = x * lax.rsqrt(jnp.mean(x * x, axis=-1, keepdims=True) + RMS_EPS) * g
    return y * (1.0 + sc) + sh


def _inproj_kernel(x_ref, g_ref, sc_ref, sh_ref, w_ref, o_ref, *rest, emit_tb):
    h_scr = rest[-1]
    j = pl.program_id(1)

    @pl.when(j == 0)
    def _():
        h_scr[...] = _modulated_norm(x_ref[...], g_ref[...], sc_ref[0], sh_ref[0]).astype(BF16)

    acc = jnp.dot(h_scr[...], w_ref[...].astype(BF16), preferred_element_type=F32)
    o_ref[...] = acc
    if emit_tb:
        u_ref = rest[0]

        @pl.when(j == 0)
        def _():
            u_ref[...] = acc[:, :BRANCH_W]


def _inproj(x2d, g, mod3, w, *, tm, tiles_per_group, tb_shape=None):
    rows = x2d.shape[0]
    m = mod3.shape[1]
    tn = 512
    tpg = tiles_per_group
    in_specs = [pl.BlockSpec((tm, D_MODEL), lambda i, j: (i, 0)),
                pl.BlockSpec((1, D_MODEL), lambda i, j: (0, 0)),
                pl.BlockSpec((1, m, D_MODEL), lambda i, j: (i // tpg, 0, 1)),
                pl.BlockSpec((1, m, D_MODEL), lambda i, j: (i // tpg, 0, 0)),
                pl.BlockSpec((D_MODEL, tn), lambda i, j: (0, j))]
    out_specs = [pl.BlockSpec((tm, tn), lambda i, j: (i, j))]
    out_shape = [jax.ShapeDtypeStruct((rows, IN_W), F32)]
    if tb_shape is not None:
        out_specs.append(pl.BlockSpec((tm, BRANCH_W), lambda i, j: (i % tpg, i // tpg)))
        out_shape.append(jax.ShapeDtypeStruct(tb_shape, F32))
    return pl.pallas_call(
        functools.partial(_inproj_kernel, emit_tb=tb_shape is not None),
        grid=(rows // tm, IN_W // tn),
        in_specs=in_specs, out_specs=out_specs, out_shape=out_shape,
        scratch_shapes=[pltpu.VMEM((tm, D_MODEL), BF16)],
        compiler_params=_params(("arbitrary", "arbitrary")),
        name="inproj",
    )(x2d, g.reshape(1, D_MODEL), mod3, mod3, w)


def _s5_kernel(u_ref, x0r_ref, x0i_ref, lbr_ref, lbi_ref, br_ref, bi_ref, cr_ref, ci_ref, d_ref, wg_ref,
               y_ref, sr_ref, si_ref, bur, bui, xr, xi, *, nb, tc):
    c = pl.program_id(0)

    @pl.when(c == 0)
    def _():
        xr[...] = x0r_ref[...]
        xi[...] = x0i_ref[...]

    u = u_ref[...]
    ub = u.astype(BF16)
    bur[...] = jnp.dot(ub, br_ref[...].astype(BF16), preferred_element_type=F32)
    bui[...] = jnp.dot(ub, bi_ref[...].astype(BF16), preferred_element_type=F32)
    lbr = jnp.broadcast_to(lbr_ref[...], (nb, S5_W))
    lbi = jnp.broadcast_to(lbi_ref[...], (nb, S5_W))

    def body(t, carry):
        sr, si = carry
        rows = pl.ds(pl.multiple_of(t * nb, nb), nb)
        nr = lbr * sr - lbi * si + bur[rows, :]
        ni = lbr * si + lbi * sr + bui[rows, :]
        bur[rows, :] = nr
        bui[rows, :] = ni
        return nr, ni

    sr, si = lax.fori_loop(0, tc, body, (xr[...], xi[...]))
    xr[...] = sr
    xi[...] = si
    sr_ref[...] = sr
    si_ref[...] = si
    y = _dot(bur[...], cr_ref[...]) - _dot(bui[...], ci_ref[...]) + d_ref[...] * u
    z = jax.nn.gelu(y)
    y_ref[...] = z * jax.nn.sigmoid(_dot(z, wg_ref[...]))


def _s5_params(lam_re, lam_im, log_dt, b_re, b_im, c_re, c_im):
    dt = jnp.exp(log_dt)[:, None]
    mag = jnp.exp(lam_re * dt)
    lb_re = mag * jnp.cos(lam_im * dt)
    lb_im = mag * jnp.sin(lam_im * dt)
    den = lam_re * lam_re + lam_im * lam_im
    q_re = ((lb_re - 1.0) * lam_re + lb_im * lam_im) / den
    q_im = (lb_im * lam_re - (lb_re - 1.0) * lam_im) / den
    bb_re = q_re[..., None] * b_re - q_im[..., None] * b_im
    bb_im = q_re[..., None] * b_im + q_im[..., None] * b_re
    eye = jnp.eye(S5_GROUPS, dtype=F32)
    to_in = lambda bb: jnp.einsum('gnc,gh->gchn', bb, eye).reshape(BRANCH_W, S5_W)
    to_out = lambda cc: jnp.einsum('gcn,gh->gnhc', cc, eye).reshape(S5_W, BRANCH_W)
    return (lb_re.reshape(1, S5_W), lb_im.reshape(1, S5_W), to_in(bb_re), to_in(bb_im),
            to_out(c_re), to_out(c_im))


def _s5(u_tb, x0_re, x0_im, s5p, d_skip, w_glu, *, nb, t_len, tc):
    lb_re, lb_im, bin_re, bin_im, cout_re, cout_im = s5p
    rows = tc * nb
    const = lambda shape: pl.BlockSpec(shape, lambda c: (0,) * len(shape))
    return pl.pallas_call(
        functools.partial(_s5_kernel, nb=nb, tc=tc),
        grid=(t_len // tc,),
        in_specs=[pl.BlockSpec((rows, BRANCH_W), lambda c: (c, 0)),
                  const((nb, S5_W)), const((nb, S5_W)), const((1, S5_W)), const((1, S5_W)),
                  const((BRANCH_W, S5_W)), const((BRANCH_W, S5_W)),
                  const((S5_W, BRANCH_W)), const((S5_W, BRANCH_W)),
                  const((1, BRANCH_W)), const((BRANCH_W, BRANCH_W))],
        out_specs=[pl.BlockSpec((rows, BRANCH_W), lambda c: (c, 0)), const((nb, S5_W)), const((nb, S5_W))],
        out_shape=[jax.ShapeDtypeStruct((t_len * nb, BRANCH_W), F32),
                   jax.ShapeDtypeStruct((nb, S5_W), F32), jax.ShapeDtypeStruct((nb, S5_W), F32)],
        scratch_shapes=[pltpu.VMEM((rows, S5_W), F32), pltpu.VMEM((rows, S5_W), F32),
                        pltpu.VMEM((nb, S5_W), F32), pltpu.VMEM((nb, S5_W), F32)],
        compiler_params=_params(("arbitrary",)),
        name="s5",
    )(u_tb, x0_re, x0_im, lb_re, lb_im, bin_re, bin_im, cout_re, cout_im,
      d_skip.reshape(1, BRANCH_W), w_glu)


def _rwkv_features(pm_r, pm_k, pm_v, pm_l, prm, ones_bd):
    w0, w2p, a0, a2p, g2p, k_k, k_a = prm
    w_raw = w0 + _dot(jnp.tanh(pm_l), w2p)
    lw = -jax.nn.sigmoid(w_raw) * math.exp(-0.5)
    a_sig = jax.nn.sigmoid(a0 + _dot(pm_l, a2p))
    g = _dot(jax.nn.sigmoid(pm_l), g2p)
    kk = pm_k * k_k
    kk = kk * lax.rsqrt(_head_sum(kk * kk, ones_bd) + 1e-12)
    k_mod = pm_k * (1.0 + (a_sig - 1.0) * k_a)
    return pm_r, k_mod, pm_v, lw, -kk, kk * a_sig, g


def _rwkv_post(y, r, k_mod, v, g, r_k, ln_g, ln_b, ones_bd):
    inv = 1.0 / HEAD_DIM
    mean = _head_sum(y, ones_bd) * inv
    yc = y - mean
    var = _head_sum(yc * yc, ones_bd) * inv
    yn = yc * lax.rsqrt(var + RWKV_LN_EPS) * ln_g + ln_b
    bonus = _head_sum(r * k_mod * r_k, ones_bd) * v
    return (yn + bonus) * g


RWKV_CHUNK = 64


def _cumsum_rows(tril, x):
    return sum(jnp.dot(tril, p, preferred_element_type=F32) for p in _split3(x))


def _per_head(x, ones_bd):
    return jnp.concatenate([x.astype(BF16)] * N_HEADS, axis=0) * ones_bd.astype(BF16)


def _rwkv_chunks(r, k_mod, v, lw, aa, bb, states, ones_bd):
    n = RWKV_CHUNK
    nb = len(states)
    seqs = range(nb)
    part = lambda x, b: x[b * n:(b + 1) * n]
    t_idx = lax.broadcasted_iota(jnp.int32, (n, BRANCH_W), 0)
    i_idx = lax.broadcasted_iota(jnp.int32, (n, BRANCH_W), 1) % n
    strict = (t_idx > i_idx).astype(F32)
    incl = (t_idx >= i_idx).astype(F32)
    eye_c = (t_idx == i_idx).astype(F32)
    tril = (lax.broadcasted_iota(jnp.int32, (n, n), 0) >= lax.broadcasted_iota(jnp.int32, (n, n), 1)).astype(BF16)
    bd = lambda x: _per_head(x, ones_bd)
    cum = jnp.concatenate([_cumsum_rows(tril, part(lw, b)) for b in seqs], axis=0)
    tot = [part(cum, b)[n - 1:n, :] for b in seqs]
    tot_rows = jnp.concatenate([jnp.broadcast_to(t, (n, BRANCH_W)) for t in tot], axis=0)
    e_neg = jnp.exp(-cum)
    e_rem = jnp.exp(tot_rows - cum)
    a_t = aa * jnp.exp(cum - lw)
    r_t = r * jnp.exp(cum)
    b_t = bb * e_neg
    k_t = k_mod * e_neg
    b_h = bb * e_rem
    k_h = k_mod * e_rem
    ar = [jnp.concatenate([part(a_t, b), part(r_t, b)], axis=0) for b in seqs]
    xb = [_dot_nt(ar[b], bd(part(b_t, b))) for b in seqs]
    xk = [_dot_nt(ar[b], bd(part(k_t, b))) for b in seqs]
    ss = [_dot_nt(ar[b], states[b]) for b in seqs]
    v_bd = [bd(part(v, b)) for b in seqs]
    pw = [xb[b][:n] * strict for b in seqs]
    mv = [_dot(xk[b][:n] * strict, v_bd[b]) for b in seqs]
    tinv = [eye_c + pw[b] for b in seqs]
    for _ in range(int(math.log2(n)) - 1):
        pw = [_dot(pw[b], bd(pw[b])) for b in seqs]
        tinv = [tinv[b] + _dot(pw[b], bd(tinv[b])) for b in seqs]
    u = [_dot(tinv[b], bd(ss[b][:n] + mv[b])) for b in seqs]
    y = [ss[b][n:] + _dot(xb[b][n:] * incl, bd(u[b])) + _dot(xk[b][n:] * incl, v_bd[b]) for b in seqs]
    upd = [_dot_tn(jnp.concatenate([u[b], part(v, b)], axis=0),
                   jnp.concatenate([part(b_h, b), part(k_h, b)], axis=0)) for b in seqs]
    new_states = [states[b] * jnp.exp(tot[b]) + ones_bd * upd[b] for b in seqs]
    return jnp.concatenate(y, axis=0), new_states


def _rwkv_kernel(r_ref, k_ref, v_ref, l_ref, shift_ref, s0_ref, mu_ref, w0_ref, w2_ref, a0_ref, a2_ref,
                 g2_ref, kk_ref, ka_ref, rk_ref, lng_ref, lnb_ref,
                 y_ref, s_ref, sh_ref, prev_scr, s_scr, *, nb):
    c = pl.program_id(1)
    n = RWKV_CHUNK

    @pl.when(c == 0)
    def _():
        prev_scr[...] = shift_ref[...]
        s_scr[...] = s0_ref[...]

    row = lax.broadcasted_iota(jnp.int32, (n, BRANCH_W), 0)
    ones_bd = _head_ones()
    prm = (w0_ref[...], w2_ref[...], a0_ref[...], a2_ref[...], g2_ref[...], kk_ref[...], ka_ref[...])
    pieces = []
    for idx, ref in enumerate((r_ref, k_ref, v_ref, l_ref)):
        lanes = slice(idx * BRANCH_W, (idx + 1) * BRANCH_W)
        shifted = []
        for b in range(nb):
            x = ref[b]
            x_prev = jnp.where(row == 0, prev_scr[b, :, lanes], pltpu.roll(x, 1, 0))
            shifted.append(x + (x_prev - x) * mu_ref[:, lanes])
            prev_scr[b, :, lanes] = x[n - 1:n, :]
            sh_ref[b, :, lanes] = x[n - 1:n, :]
        pieces.append(jnp.concatenate(shifted, axis=0))
    r, k_mod, v, lw, aa, bb, g = _rwkv_features(*pieces, prm, ones_bd)
    y, new_states = _rwkv_chunks(r, k_mod, v, lw, aa, bb, [s_scr[b] for b in range(nb)], ones_bd)
    out = _rwkv_post(y, r, k_mod, v, g, rk_ref[...], lng_ref[...], lnb_ref[...], ones_bd)
    for b in range(nb):
        s_scr[b] = new_states[b]
        s_ref[b] = new_states[b]
        y_ref[b] = out[b * n:(b + 1) * n]


def _rwkv_params(mu, w0, w2, a0, a2, g2, k_k, k_a, r_k, ln_g, ln_b):
    row = lambda t: t.reshape(1, -1)
    pad = lambda w, lo: jnp.zeros((BRANCH_W, BRANCH_W), F32).at[lo:lo + w.shape[0]].set(w)
    return (row(mu), row(w0), pad(w2, 0), row(a0), pad(a2, 64), pad(g2, 128), row(k_k), row(k_a),
            row(r_k), row(ln_g), row(ln_b))


def _block_diag_state(s):
    bsz = s.shape[0]
    eye = jnp.eye(N_HEADS, dtype=F32)
    return jnp.einsum('bhij,hg->bhigj', s, eye).reshape(bsz, BRANCH_W, BRANCH_W)


def _diag_blocks(s_bd):
    bsz = s_bd.shape[0]
    s5 = s_bd.reshape(bsz, N_HEADS, HEAD_DIM, N_HEADS, HEAD_DIM)
    return jnp.stack([s5[:, h, :, h, :] for h in range(N_HEADS)], axis=1)


def _rwkv(proj3, shift0, s0, rp, *, nb):
    bsz, t_len, _ = proj3.shape
    chunk = RWKV_CHUNK
    col = lambda k: pl.BlockSpec((nb, chunk, BRANCH_W), lambda b, c: (b, c, COL_RWKV + k))
    const = lambda shape: pl.BlockSpec(shape, lambda b, c: (0,) * len(shape))
    vec = const((1, BRANCH_W))
    mat = const((BRANCH_W, BRANCH_W))
    y, s_bd, shift_n = pl.pallas_call(
        functools.partial(_rwkv_kernel, nb=nb),
        grid=(bsz // nb, t_len // chunk),
        in_specs=[col(0), col(1), col(2), col(3),
                  pl.BlockSpec((nb, 1, RWKV_IN_W), lambda b, c: (b, 0, 0)),
                  pl.BlockSpec((nb, BRANCH_W, BRANCH_W), lambda b, c: (b, 0, 0)),
                  const((1, RWKV_IN_W)), vec, mat, vec, mat, mat, vec, vec, vec, vec, vec],
        out_specs=[pl.BlockSpec((nb, chunk, BRANCH_W), lambda b, c: (b, c, 0)),
                   pl.BlockSpec((nb, BRANCH_W, BRANCH_W), lambda b, c: (b, 0, 0)),
                   pl.BlockSpec((nb, 1, RWKV_IN_W), lambda b, c: (b, 0, 0))],
        out_shape=[jax.ShapeDtypeStruct((bsz, t_len, BRANCH_W), F32),
                   jax.ShapeDtypeStruct((bsz, BRANCH_W, BRANCH_W), F32),
                   jax.ShapeDtypeStruct((bsz, 1, RWKV_IN_W), F32)],
        scratch_shapes=[pltpu.VMEM((nb, 1, RWKV_IN_W), F32), pltpu.VMEM((nb, BRANCH_W, BRANCH_W), F32)],
        compiler_params=_params(("arbitrary", "arbitrary")),
        name="rwkv",
    )(proj3, proj3, proj3, proj3, shift0.reshape(bsz, 1, RWKV_IN_W), _block_diag_state(s0), *rp)
    return y, _diag_blocks(s_bd), shift_n.reshape(bsz, RWKV_IN_W)


def _rope_tables(t_len, pos0):
    half = HEAD_DIM // 2
    freqs = 1.0 / (ROPE_BASE ** jnp.linspace(0.0, 1.0, half, dtype=F32))
    pos = jnp.arange(t_len, dtype=F32) + pos0
    ang = pos[:, None] * freqs[None, :]
    cos = jnp.cos(ang)
    sin = jnp.sin(ang)
    cos_t = jnp.tile(jnp.concatenate([cos, cos], axis=-1), (1, N_HEADS))
    sin_t = jnp.tile(jnp.concatenate([-sin, sin], axis=-1), (1, N_HEADS))
    return cos_t, sin_t


def _rope(x, cos_t, sin_t):
    lane = lax.broadcasted_iota(jnp.int32, x.shape, 1)
    first = (lane % HEAD_DIM) < (HEAD_DIM // 2)
    swapped = jnp.where(first, pltpu.roll(x, BRANCH_W - HEAD_DIM // 2, 1), pltpu.roll(x, HEAD_DIM // 2, 1))
    return x * cos_t + swapped * sin_t


def _ret_tables(chunk):
    log_gamma = jnp.log(1.0 - jnp.exp2(-5.0 - jnp.arange(N_HEADS, dtype=F32)))
    i = jnp.arange(chunk, dtype=F32)
    diff = i[:, None] - i[None, :]
    dmask = jnp.where(diff >= 0, jnp.exp(log_gamma[:, None, None] * jnp.maximum(diff, 0.0)), 0.0)
    lanes = lambda t: jnp.repeat(t, HEAD_DIM, axis=-1)
    xi = lanes(jnp.exp(log_gamma[None, :] * (i[:, None] + 1.0)))
    zeta = lanes(jnp.exp(log_gamma[None, :] * (chunk - 1.0 - i[:, None])))
    g_chunk = lanes(jnp.exp(log_gamma * chunk)[None, :])
    return dmask, xi, zeta, g_chunk


def _ret_kernel(q_ref, k_ref, v_ref, g_ref, s0_ref, cos_ref, sin_ref, dm_ref, xi_ref, zeta_ref, gch_ref,
                y_ref, s_ref, s_scr):
    c = pl.program_id(1)

    @pl.when(c == 0)
    def _():
        s_scr[...] = s0_ref[0]

    cos_t = cos_ref[...]
    sin_t = sin_ref[...]
    q = _rope(q_ref[0], cos_t, sin_t)
    k = _rope(k_ref[0], cos_t, sin_t) * (HEAD_DIM ** -0.5)
    v = v_ref[0]
    s = s_scr[...]
    ones_bd = _head_ones()
    o = _dot(q, s) * xi_ref[...]
    for h, mh in enumerate(_head_masks()):
        att = _dot_nt(q * mh, k) * dm_ref[h]
        o = o + _dot(att, v) * mh
    s_new = s * gch_ref[...] + ones_bd * _dot_tn(k * zeta_ref[...], v)
    s_scr[...] = s_new
    s_ref[0] = s_new
    o = o * lax.rsqrt(_head_sum(o * o, ones_bd) * (1.0 / HEAD_DIM) + RMS_EPS)
    g = g_ref[0]
    y_ref[0] = o * (g * jax.nn.sigmoid(g))


def _ret(proj3, s0, pos0, *, chunk):
    bsz, t_len, _ = proj3.shape
    cos_t, sin_t = _rope_tables(t_len, pos0)
    dmask, xi, zeta, g_chunk = _ret_tables(chunk)
    col = lambda k: pl.BlockSpec((1, chunk, BRANCH_W), lambda b, c: (b, c, COL_RET + k))
    const = lambda shape: pl.BlockSpec(shape, lambda b, c: (0,) * len(shape))
    tab = pl.BlockSpec((chunk, BRANCH_W), lambda b, c: (c, 0))
    y, s_bd = pl.pallas_call(
        _ret_kernel,
        grid=(bsz, t_len // chunk),
        in_specs=[col(0), col(1), col(2), col(3),
                  pl.BlockSpec((1, BRANCH_W, BRANCH_W), lambda b, c: (b, 0, 0)),
                  tab, tab, const((N_HEADS, chunk, chunk)), const((chunk, BRANCH_W)),
                  const((chunk, BRANCH_W)), const((1, BRANCH_W))],
        out_specs=[pl.BlockSpec((1, chunk, BRANCH_W), lambda b, c: (b, c, 0)),
                   pl.BlockSpec((1, BRANCH_W, BRANCH_W), lambda b, c: (b, 0, 0))],
        out_shape=[jax.ShapeDtypeStruct((bsz, t_len, BRANCH_W), F32),
                   jax.ShapeDtypeStruct((bsz, BRANCH_W, BRANCH_W), F32)],
        scratch_shapes=[pltpu.VMEM((BRANCH_W, BRANCH_W), F32)],
        compiler_params=_params(("arbitrary", "arbitrary")),
        name="retention",
    )(proj3, proj3, proj3, proj3, _block_diag_state(s0), cos_t, sin_t, dmask, xi, zeta, g_chunk)
    return y, _diag_blocks(s_bd)


def _topk_rows(gs_t, n_valid):
    nblk = gs_t.shape[0]
    blk = lax.broadcasted_iota(jnp.int32, gs_t.shape, 0)
    valid = blk < n_valid
    gsm = jnp.where(valid, gs_t, NEG_INF)
    cnt = jnp.zeros(gs_t.shape, F32)
    for m in range(nblk):
        row = gsm[m:m + 1, :]
        cnt = cnt + jnp.where(row > gsm, 1.0, jnp.where((row == gsm) & (blk > m), 1.0, 0.0))
    return jnp.where(valid & (cnt < MOBA_TOPK), 1.0, 0.0)


def _dot_nt3(a, b):
    ah = a.astype(BF16)
    al = (a - ah.astype(F32)).astype(BF16)
    bh = b.astype(BF16)
    bl = (b - bh.astype(F32)).astype(BF16)
    d = lambda x, y: lax.dot_general(x, y, (((1,), (1,)), ((), ())), preferred_element_type=F32)
    return d(ah, bh) + d(ah, bl) + d(al, bh)


def _moba_kernel(q_ref, k_ref, v_ref, o_ref, km_scr, kb_scr, vb_scr, *, nblk):
    qi = pl.program_id(1)
    bs = MOBA_BLOCK
    masks = _head_masks()

    @pl.when(qi == 0)
    def _():
        for n in range(nblk):
            rows = slice(n * bs, (n + 1) * bs)
            kblk = k_ref[0, rows, :]
            kb_scr[rows, :] = kblk.astype(BF16)
            vb_scr[rows, :] = v_ref[0, rows, :].astype(BF16)
            km = jnp.mean(kblk, axis=0, keepdims=True)
            for h, mh in enumerate(masks):
                km_scr[h * nblk + n:h * nblk + n + 1, :] = km * mh

    q = q_ref[0]
    scale = HEAD_DIM ** -0.5
    ri = lax.broadcasted_iota(jnp.int32, (bs, bs), 0)
    ci = lax.broadcasted_iota(jnp.int32, (bs, bs), 1)
    eye_b = (lax.broadcasted_iota(jnp.int32, (nblk, 128), 0)
             == lax.broadcasted_iota(jnp.int32, (nblk, 128), 1)).astype(F32)
    blk = lax.broadcasted_iota(jnp.int32, (1, 128), 1)
    own = pl.ds(pl.multiple_of(qi * bs, bs), bs)
    k_own = kb_scr[own, :]
    v_own = vb_scr[own, :]
    gs_all = _dot_nt3(km_scr[...], q)
    heads = range(N_HEADS)
    nt = lambda x, y: lax.dot_general(x, y, (((1,), (1,)), ((), ())), preferred_element_type=F32)
    pv = lambda p, vblk: jnp.dot(p.astype(BF16), vblk, preferred_element_type=F32)
    spread = lambda cols: sum(masks[h] * cols[h] for h in heads)
    qh = [(q * masks[h]).astype(BF16) for h in heads]
    sel = [lax.dot_general(_topk_rows(gs_all[h * nblk:(h + 1) * nblk], qi), eye_b, (((0,), (0,)), ((), ())),
                           preferred_element_type=F32) for h in heads]
    s = [jnp.where(ci <= ri, nt(qh[h], k_own) * scale, NEG_INF) for h in heads]
    m0 = [jnp.max(s[h], axis=1, keepdims=True) for h in heads]
    p = [jnp.exp(s[h] - m0[h]) for h in heads]
    l0 = [jnp.sum(p[h], axis=1, keepdims=True) for h in heads]
    acc0 = sum(masks[h] * pv(p[h], v_own) for h in heads)

    def body(n, carry):
        m, l, acc = carry
        rows = pl.ds(pl.multiple_of(n * bs, bs), bs)
        kblk = kb_scr[rows, :]
        vblk = vb_scr[rows, :]
        seln = [jnp.sum(jnp.where(blk == n, sel[h], 0.0), axis=1, keepdims=True) for h in heads]
        s = [jnp.where(seln[h] > 0.0, nt(qh[h], kblk) * scale, NEG_INF) for h in heads]
        m_new = [jnp.maximum(m[h], jnp.max(s[h], axis=1, keepdims=True)) for h in heads]
        alpha = [jnp.exp(m[h] - m_new[h]) for h in heads]
        p = [jnp.exp(s[h] - m_new[h]) for h in heads]
        l = [alpha[h] * l[h] + jnp.sum(p[h], axis=1, keepdims=True) for h in heads]
        acc = spread(alpha) * acc + sum(masks[h] * pv(p[h], vblk) for h in heads)
        return tuple(m_new), tuple(l), acc

    _, l, acc = lax.fori_loop(0, qi, body, (tuple(m0), tuple(l0), acc0))
    o_ref[0] = acc / spread(l)


def _moba_prompt(proj3):
    bsz, t_len, _ = proj3.shape
    nblk = t_len // MOBA_BLOCK
    full = lambda k: pl.BlockSpec((1, t_len, BRANCH_W), lambda b, i: (b, 0, COL_MOBA + k))
    return pl.pallas_call(
        functools.partial(_moba_kernel, nblk=nblk),
        grid=(bsz, nblk),
        in_specs=[pl.BlockSpec((1, MOBA_BLOCK, BRANCH_W), lambda b, i: (b, i, COL_MOBA)), full(1), full(2)],
        out_specs=pl.BlockSpec((1, MOBA_BLOCK, BRANCH_W), lambda b, i: (b, i, 0)),
        out_shape=jax.ShapeDtypeStruct((bsz, t_len, BRANCH_W), F32),
        scratch_shapes=[pltpu.VMEM((N_HEADS * nblk, BRANCH_W), F32), pltpu.VMEM((t_len, BRANCH_W), BF16),
                        pltpu.VMEM((t_len, BRANCH_W), BF16)],
        compiler_params=_params(("arbitrary", "arbitrary")),
        name="moba_prompt",
    )(proj3, proj3, proj3)


def _merge_kernel(x_ref, y0_ref, y1_ref, y2_ref, y3_ref, g0_ref, g1_ref, g2_ref, g3_ref, gt_ref,
                  wb_ref, wo_ref, o_ref, wb_scr, wo_scr):
    @pl.when(pl.program_id(0) == 0)
    def _():
        wb_scr[...] = wb_ref[...].astype(BF16)
        wo_scr[...] = wo_ref[...].astype(BF16)

    mixed = None
    for g, (y_ref, g_ref) in enumerate(((y0_ref, g0_ref), (y1_ref, g1_ref), (y2_ref, g2_ref), (y3_ref, g3_ref))):
        up = jnp.dot(y_ref[...].astype(BF16), wb_scr[g], preferred_element_type=F32)
        term = jax.nn.sigmoid(g_ref[...]) * up
        mixed = term if mixed is None else mixed + term
    o_ref[...] = x_ref[...] + gt_ref[0] * jnp.dot(mixed.astype(BF16), wo_scr[...], preferred_element_type=F32)


def _merge(x2d, y_s5, s5_spec, y_rwkv, y_ret, y_moba, proj, mod3, w_branch, w_out, *, tm, tiles_per_group):
    rows = x2d.shape[0]
    m = mod3.shape[1]
    tpg = tiles_per_group
    ysp = pl.BlockSpec((tm, BRANCH_W), lambda i: (i, 0))
    gate = lambda g: pl.BlockSpec((tm, D_MODEL), lambda i: (i, COL_GATE // N_BRANCH + g))
    return pl.pallas_call(
        _merge_kernel,
        grid=(rows // tm,),
        in_specs=[pl.BlockSpec((tm, D_MODEL), lambda i: (i, 0)), s5_spec, ysp, ysp, ysp,
                  gate(0), gate(1), gate(2), gate(3),
                  pl.BlockSpec((1, m, D_MODEL), lambda i: (i // tpg, 0, 2)),
                  pl.BlockSpec((N_BRANCH, BRANCH_W, D_MODEL), lambda i: (0, 0, 0)),
                  pl.BlockSpec((D_MODEL, D_MODEL), lambda i: (0, 0))],
        out_specs=pl.BlockSpec((tm, D_MODEL), lambda i: (i, 0)),
        out_shape=jax.ShapeDtypeStruct((rows, D_MODEL), F32),
        scratch_shapes=[pltpu.VMEM((N_BRANCH, BRANCH_W, D_MODEL), BF16), pltpu.VMEM((D_MODEL, D_MODEL), BF16)],
        compiler_params=_params(("arbitrary",)),
        name="merge",
    )(x2d, y_s5, y_rwkv, y_ret, y_moba, proj, proj, proj, proj, mod3, w_branch, w_out)


def _router_combine(h, router):
    logits = _dot_hi(h, router)
    lane = lax.broadcasted_iota(jnp.int32, logits.shape, 1)
    logits = jnp.where(lane < N_EXPERTS, logits, NEG_INF)
    m1 = jnp.max(logits, axis=1, keepdims=True)
    i1 = jnp.min(jnp.where(logits == m1, lane, ROUTER_PAD), axis=1, keepdims=True)
    rest = jnp.where(lane == i1, NEG_INF, logits)
    m2 = jnp.max(rest, axis=1, keepdims=True)
    i2 = jnp.min(jnp.where(rest == m2, lane, ROUTER_PAD), axis=1, keepdims=True)
    e2 = jnp.exp(m2 - m1)
    den = 1.0 + e2
    comb = jnp.where(lane == i1, 1.0 / den, 0.0) + jnp.where(lane == i2, e2 / den, 0.0)
    return comb, jnp.where((lane == i1) | (lane == i2), 1.0, 0.0)


def _swiglu(hb, w1, w3, w2):
    a = jnp.dot(hb, w1, preferred_element_type=F32)
    b = jnp.dot(hb, w3, preferred_element_type=F32)
    act = (a * jax.nn.sigmoid(a)) * b
    return jnp.dot(act.astype(BF16), w2, preferred_element_type=F32)


def _ffn_kernel(x_ref, g_ref, sc_ref, sh_ref, gt_ref, w1_ref, w3_ref, w2_ref, o_ref, h_scr, acc_scr, *, n_j):
    j = pl.program_id(1)

    @pl.when(j == 0)
    def _():
        h_scr[...] = _modulated_norm(x_ref[...], g_ref[...], sc_ref[0], sh_ref[0]).astype(BF16)
        acc_scr[...] = jnp.zeros(acc_scr.shape, F32)

    acc_scr[...] += _swiglu(h_scr[...], w1_ref[...].astype(BF16), w3_ref[...].astype(BF16),
                            w2_ref[...].astype(BF16))

    @pl.when(j == n_j - 1)
    def _():
        o_ref[...] = x_ref[...] + gt_ref[0] * acc_scr[...]


def _ffn(x2d, g, mod3, weights, *, tm, tiles_per_group, tf):
    rows = x2d.shape[0]
    m = mod3.shape[1]
    tpg = tiles_per_group
    w1, w3, w2 = weights
    n_j = w1.shape[1] // tf
    modspec = lambda k: pl.BlockSpec((1, m, D_MODEL), lambda i, j: (i // tpg, 0, k))
    return pl.pallas_call(
        functools.partial(_ffn_kernel, n_j=n_j),
        grid=(rows // tm, n_j),
        in_specs=[pl.BlockSpec((tm, D_MODEL), lambda i, j: (i, 0)),
                  pl.BlockSpec((1, D_MODEL), lambda i, j: (0, 0)),
                  modspec(4), modspec(3), modspec(5),
                  pl.BlockSpec((D_MODEL, tf), lambda i, j: (0, j)),
                  pl.BlockSpec((D_MODEL, tf), lambda i, j: (0, j)),
                  pl.BlockSpec((tf, D_MODEL), lambda i, j: (j, 0))],
        out_specs=pl.BlockSpec((tm, D_MODEL), lambda i, j: (i, 0)),
        out_shape=jax.ShapeDtypeStruct((rows, D_MODEL), F32),
        scratch_shapes=[pltpu.VMEM((tm, D_MODEL), BF16), pltpu.VMEM((tm, D_MODEL), F32)],
        compiler_params=_params(("arbitrary", "arbitrary")),
        name="dense_ffn",
    )(x2d, g.reshape(1, D_MODEL), mod3, mod3, mod3, w1, w3, w2)


def _moe_kernel(x_ref, g_ref, sc_ref, sh_ref, gt_ref, rt_ref, w1_ref, w3_ref, w2_ref, o_ref,
                h_scr, acc_scr, comb_scr, asg_scr, rank_scr, *, n_e, cap):
    e = pl.program_id(1)
    tm = x_ref.shape[0]

    @pl.when(e == 0)
    def _():
        h = _modulated_norm(x_ref[...], g_ref[...], sc_ref[0], sh_ref[0])
        h_scr[...] = h.astype(BF16)
        acc_scr[...] = jnp.zeros(acc_scr.shape, F32)
        comb, asg = _router_combine(h, rt_ref[...])
        comb_scr[...] = comb
        asg_scr[...] = asg
        below = (lax.broadcasted_iota(jnp.int32, (tm, tm), 0) > lax.broadcasted_iota(jnp.int32, (tm, tm), 1))
        rank_scr[...] = jnp.dot(below.astype(BF16), asg.astype(BF16), preferred_element_type=F32)

    lane = lax.broadcasted_iota(jnp.int32, (tm, ROUTER_PAD), 1)
    pick = lambda ref: jnp.sum(jnp.where(lane == e, ref[...], 0.0), axis=1, keepdims=True)
    a_col = pick(asg_scr)
    r_col = pick(rank_scr)
    c_col = pick(comb_scr)
    count = jnp.sum(a_col).astype(jnp.int32)
    slot = lax.broadcasted_iota(jnp.int32, (1, cap), 1).astype(F32)
    w1 = w1_ref[0]
    w3 = w3_ref[0]
    w2 = w2_ref[0]

    def one_pass(pi, carry):
        base = (pi * cap).astype(F32)
        sel = jnp.where((r_col - base == slot) & (a_col > 0.0), 1.0, 0.0).astype(BF16)
        packed = lax.dot_general(sel, h_scr[...], (((0,), (0,)), ((), ())), preferred_element_type=F32)
        y = _swiglu(packed.astype(BF16), w1, w3, w2)
        y_hi = y.astype(BF16)
        y_lo = (y - y_hi.astype(F32)).astype(BF16)
        spread = (jnp.dot(sel, y_hi, preferred_element_type=F32)
                  + jnp.dot(sel, y_lo, preferred_element_type=F32))
        acc_scr[...] += c_col * spread
        return carry

    lax.fori_loop(0, (count + cap - 1) // cap, one_pass, 0)

    @pl.when(e == n_e - 1)
    def _():
        o_ref[...] = x_ref[...] + gt_ref[0] * acc_scr[...]


def _moe(x2d, g, mod3, weights, *, tm, tiles_per_group, cap):
    rows = x2d.shape[0]
    m = mod3.shape[1]
    tpg = tiles_per_group
    router, w1, w3, w2 = weights
    router = jnp.zeros((D_MODEL, ROUTER_PAD), F32).at[:, :N_EXPERTS].set(router)
    n_e, _, d_ff = w1.shape
    modspec = lambda k: pl.BlockSpec((1, m, D_MODEL), lambda i, e: (i // tpg, 0, k))
    return pl.pallas_call(
        functools.partial(_moe_kernel, n_e=n_e, cap=cap),
        grid=(rows // tm, n_e),
        in_specs=[pl.BlockSpec((tm, D_MODEL), lambda i, e: (i, 0)),
                  pl.BlockSpec((1, D_MODEL), lambda i, e: (0, 0)),
                  modspec(4), modspec(3), modspec(5),
                  pl.BlockSpec((D_MODEL, ROUTER_PAD), lambda i, e: (0, 0)),
                  pl.BlockSpec((1, D_MODEL, d_ff), lambda i, e: (e, 0, 0)),
                  pl.BlockSpec((1, D_MODEL, d_ff), lambda i, e: (e, 0, 0)),
                  pl.BlockSpec((1, d_ff, D_MODEL), lambda i, e: (e, 0, 0))],
        out_specs=pl.BlockSpec((tm, D_MODEL), lambda i, e: (i, 0)),
        out_shape=jax.ShapeDtypeStruct((rows, D_MODEL), F32),
        scratch_shapes=[pltpu.VMEM((tm, D_MODEL), BF16), pltpu.VMEM((tm, D_MODEL), F32),
                        pltpu.VMEM((tm, ROUTER_PAD), F32), pltpu.VMEM((tm, ROUTER_PAD), F32),
                        pltpu.VMEM((tm, ROUTER_PAD), F32)],
        compiler_params=_params(("arbitrary", "arbitrary")),
        name="moe_ffn",
    )(x2d, g.reshape(1, D_MODEL), mod3, mod3, mod3, router, w1, w3, w2)


N_PRE = 9


def _dec_pre_kernel(r_ref, k_ref, v_ref, l_ref, q_ref, kr_ref, shift_ref, mu_ref, w0_ref, w2_ref, a0_ref,
                    a2_ref, g2_ref, kk_ref, ka_ref, cos_ref, sin_ref, o_ref):
    pieces = []
    for idx, ref in enumerate((r_ref, k_ref, v_ref, l_ref)):
        lanes = slice(idx * BRANCH_W, (idx + 1) * BRANCH_W)
        x = ref[...]
        pieces.append(x + (shift_ref[:, lanes] - x) * mu_ref[:, lanes])
    ones_bd = _head_ones()
    prm = (w0_ref[...], w2_ref[...], a0_ref[...], a2_ref[...], g2_ref[...], kk_ref[...], ka_ref[...])
    r, k_mod, v, lw, aa, bb, g = _rwkv_features(*pieces, prm, ones_bd)
    q_r = _rope(q_ref[...], cos_ref[...], sin_ref[...])
    k_r = _rope(kr_ref[...], cos_ref[...], sin_ref[...]) * (HEAD_DIM ** -0.5)
    for idx, val in enumerate((r, k_mod, v, jnp.exp(lw), aa, bb, g, q_r, k_r)):
        o_ref[:, idx * BRANCH_W:(idx + 1) * BRANCH_W] = val


def _dec_pre(proj, shift0, rp, pos0):
    rows = proj.shape[0]
    cos_t, sin_t = _rope_tables(1, pos0)
    col = lambda k: pl.BlockSpec((rows, BRANCH_W), lambda i: (0, k))
    const = lambda shape: pl.BlockSpec(shape, lambda i: (0,) * len(shape))
    vec = const((1, BRANCH_W))
    mat = const((BRANCH_W, BRANCH_W))
    mu, w0, w2p, a0, a2p, g2p, k_k, k_a = rp[:8]
    return pl.pallas_call(
        _dec_pre_kernel,
        grid=(1,),
        in_specs=[col(COL_RWKV), col(COL_RWKV + 1), col(COL_RWKV + 2), col(COL_RWKV + 3),
                  col(COL_RET), col(COL_RET + 1), const((rows, RWKV_IN_W)), const((1, RWKV_IN_W)),
                  vec, mat, vec, mat, mat, vec, vec, vec, vec],
        out_specs=const((rows, N_PRE * BRANCH_W)),
        out_shape=jax.ShapeDtypeStruct((rows, N_PRE * BRANCH_W), F32),
        compiler_params=_params(("arbitrary",)),
        name="dec_pre",
    )(proj, proj, proj, proj, proj, proj, shift0, mu, w0, w2p, a0, a2p, g2p, k_k, k_a, cos_t, sin_t)


def _dec_state_kernel(sw_ref, aa_ref, w_ref, bb_ref, km_ref, r_ref, vv_ref,
                      sr_ref, q_ref, kc_ref, vr_ref, gm_ref,
                      sw_out, y_out, sr_out, o_out):
    s = sw_ref[...]
    sa = jnp.sum(s * aa_ref[...], axis=-1, keepdims=True)
    s = s * w_ref[...] + sa * bb_ref[...] + vv_ref[...] * km_ref[...]
    sw_out[...] = s
    y_out[...] = jnp.sum(s * r_ref[...], axis=-1, keepdims=True)
    t = sr_ref[...]
    q = q_ref[...]
    kc = kc_ref[...]
    vr = vr_ref[...]
    gm = gm_ref[...]
    inter = jnp.sum(q * t, axis=1, keepdims=True) * gm
    att = jnp.sum(q * kc, axis=1, keepdims=True)
    o_out[...] = att * vr + inter
    sr_out[...] = t * gm + kc * vr


def _dec_state(s_rwkv, s_ret, pre, v_ret):
    bsz = s_rwkv.shape[0]
    nbh = bsz * N_HEADS
    tb = 64
    piece = lambda k: pre[:, k * BRANCH_W:(k + 1) * BRANCH_W]
    as_row = lambda t: t.reshape(nbh, 1, HEAD_DIM)
    as_col = lambda t: t.reshape(nbh, HEAD_DIM, 1)
    log_gamma = jnp.log(1.0 - jnp.exp2(-5.0 - jnp.arange(N_HEADS, dtype=F32)))
    gamma = jnp.broadcast_to(jnp.exp(log_gamma)[None, :, None, None], (bsz, N_HEADS, 1, HEAD_DIM))
    mat = pl.BlockSpec((tb, HEAD_DIM, HEAD_DIM), lambda i: (i, 0, 0))
    row = pl.BlockSpec((tb, 1, HEAD_DIM), lambda i: (i, 0, 0))
    colv = pl.BlockSpec((tb, HEAD_DIM, 1), lambda i: (i, 0, 0))
    sw, y, sr, o = pl.pallas_call(
        _dec_state_kernel,
        grid=(nbh // tb,),
        in_specs=[mat, row, row, row, row, row, colv, mat, colv, colv, row, row],
        out_specs=[mat, colv, mat, row],
        out_shape=[jax.ShapeDtypeStruct((nbh, HEAD_DIM, HEAD_DIM), F32),
                   jax.ShapeDtypeStruct((nbh, HEAD_DIM, 1), F32),
                   jax.ShapeDtypeStruct((nbh, HEAD_DIM, HEAD_DIM), F32),
                   jax.ShapeDtypeStruct((nbh, 1, HEAD_DIM), F32)],
        compiler_params=_params(("arbitrary",)),
        name="dec_state",
    )(s_rwkv.reshape(nbh, HEAD_DIM, HEAD_DIM), as_row(piece(4)), as_row(piece(3)), as_row(piece(5)),
      as_row(piece(1)), as_row(piece(0)), as_col(piece(2)),
      s_ret.reshape(nbh, HEAD_DIM, HEAD_DIM), as_col(piece(7)), as_col(piece(8)), as_row(v_ret),
      gamma.reshape(nbh, 1, HEAD_DIM))
    shape4 = (bsz, N_HEADS, HEAD_DIM, HEAD_DIM)
    return sw.reshape(shape4), y.reshape(bsz, BRANCH_W), sr.reshape(shape4), o.reshape(bsz, BRANCH_W)


def _dec_post_kernel(y_ref, r_ref, km_ref, v_ref, g_ref, o_ref, gr_ref, rk_ref, lng_ref, lnb_ref,
                     yw_out, yr_out):
    ones_bd = _head_ones()
    yw_out[...] = _rwkv_post(y_ref[...], r_ref[...], km_ref[...], v_ref[...], g_ref[...],
                             rk_ref[...], lng_ref[...], lnb_ref[...], ones_bd)
    o = o_ref[...]
    o = o * lax.rsqrt(_head_sum(o * o, ones_bd) * (1.0 / HEAD_DIM) + RMS_EPS)
    g = gr_ref[...]
    yr_out[...] = o * (g * jax.nn.sigmoid(g))


def _dec_post(y_rwkv, pre, o_ret, proj, rp):
    rows = proj.shape[0]
    blk = pl.BlockSpec((rows, BRANCH_W), lambda i: (0, 0))
    col = lambda k: pl.BlockSpec((rows, BRANCH_W), lambda i: (0, k))
    vec = pl.BlockSpec((1, BRANCH_W), lambda i: (0, 0))
    r_k, ln_g, ln_b = rp[8:]
    return pl.pallas_call(
        _dec_post_kernel,
        grid=(1,),
        in_specs=[blk, col(0), col(1), col(2), col(6), blk, col(COL_RET + 3), vec, vec, vec],
        out_specs=[blk, blk],
        out_shape=[jax.ShapeDtypeStruct((rows, BRANCH_W), F32)] * 2,
        compiler_params=_params(("arbitrary",)),
        name="dec_post",
    )(y_rwkv, pre, pre, pre, pre, o_ret, proj, r_k, ln_g, ln_b)


def _moba_dec_kernel(pt_ref, q_ref, kn_ref, vn_ref, *refs, n_pages):
    del pt_ref
    k_refs = refs[:n_pages]
    v_refs = refs[n_pages:2 * n_pages]
    o_ref, sc_scr = refs[2 * n_pages:]
    page = k_refs[0].shape[-1]
    per = MOBA_BLOCK // page
    nblk = n_pages // per
    half = N_HEADS * nblk
    scale = HEAD_DIM ** -0.5
    for h in range(N_HEADS):
        q_c = q_ref[0, h]
        for pg in range(n_pages):
            row = (pg % per) * half + h * nblk + pg // per
            sc_scr[row:row + 1, :] = jnp.sum(k_refs[pg][0, 0, h] * q_c, axis=0, keepdims=True)
    raw = sc_scr[...]
    rs = jnp.sum(raw, axis=1, keepdims=True)
    gate = (rs[:half] + rs[half:]) * (1.0 / MOBA_BLOCK)
    ri = lax.broadcasted_iota(jnp.int32, (half, half), 0)
    ci = lax.broadcasted_iota(jnp.int32, (half, half), 1)
    g_self = jnp.broadcast_to(gate, (half, half))
    g_other = _dot_hi(jnp.ones((half, half), F32), jnp.where(ri == ci, g_self, 0.0))
    beats = jnp.where(g_other > g_self, 1.0, jnp.where((g_other == g_self) & (ci < ri), 1.0, 0.0))
    beats = jnp.where(ri // nblk == ci // nblk, beats, 0.0)
    sel = jnp.sum(beats, axis=1, keepdims=True) < MOBA_TOPK
    sel2 = jnp.concatenate([sel.astype(F32)] * per, axis=0) > 0.0
    masked = jnp.where(sel2, raw * scale, NEG_INF)
    for h in range(N_HEADS):
        q_c = q_ref[0, h]
        s_own = jnp.sum(q_c * kn_ref[0, h], axis=0, keepdims=True) * scale
        parts = [masked[par * half + h * nblk:par * half + (h + 1) * nblk] for par in range(per)]
        m = s_own
        for part in parts:
            m = jnp.maximum(m, jnp.max(jnp.max(part, axis=1, keepdims=True), axis=0, keepdims=True))
        p_own = jnp.exp(s_own - m)
        l = p_own
        acc = jnp.zeros((HEAD_DIM, page), F32)
        for par, part in enumerate(parts):
            p = jnp.exp(part - m)
            l = l + jnp.sum(jnp.sum(p, axis=1, keepdims=True), axis=0, keepdims=True)
            for n in range(nblk):
                acc = acc + p[n:n + 1, :] * v_refs[n * per + par][0, 0, h]
        o = jnp.sum(acc, axis=1, keepdims=True) + p_own * vn_ref[0, h]
        o_ref[0, h] = o / l


def _moba_dec(q, k_new, v_new, k_t, v_t, page_table, layer):
    bsz, n_pages = page_table.shape
    page = k_t.shape[-1]
    cols = lambda t: t.reshape(bsz, N_HEADS, HEAD_DIM, 1)
    qspec = pl.BlockSpec((1, N_HEADS, HEAD_DIM, 1), lambda b, pt: (b, 0, 0, 0))
    pspec = lambda pg: pl.BlockSpec((1, 1, N_HEADS, HEAD_DIM, page), lambda b, pt: (layer, pt[b, pg], 0, 0, 0))
    out = pl.pallas_call(
        functools.partial(_moba_dec_kernel, n_pages=n_pages),
        grid_spec=pltpu.PrefetchScalarGridSpec(
            num_scalar_prefetch=1,
            grid=(bsz,),
            in_specs=[qspec, qspec, qspec] + [pspec(pg) for pg in range(n_pages)] * 2,
            out_specs=qspec,
            scratch_shapes=[pltpu.VMEM((N_HEADS * n_pages, page), F32)]),
        out_shape=jax.ShapeDtypeStruct((bsz, N_HEADS, HEAD_DIM, 1), F32),
        compiler_params=_params(("arbitrary",)),
        name="moba_decode",
    )(page_table, cols(q), cols(k_new), cols(v_new), *([k_t] * n_pages), *([v_t] * n_pages))
    return out.reshape(bsz, BRANCH_W)


def _final_norm_kernel(x_ref, g_ref, o_ref):
    x = x_ref[...]
    o_ref[...] = x * lax.rsqrt(jnp.mean(x * x, axis=-1, keepdims=True) + RMS_EPS) * g_ref[...]


def _final_norm(x2d, g, *, tm):
    rows = x2d.shape[0]
    return pl.pallas_call(
        _final_norm_kernel,
        grid=(rows // tm,),
        in_specs=[pl.BlockSpec((tm, D_MODEL), lambda i: (i, 0)), pl.BlockSpec((1, D_MODEL), lambda i: (0, 0))],
        out_specs=pl.BlockSpec((tm, D_MODEL), lambda i: (i, 0)),
        out_shape=jax.ShapeDtypeStruct((rows, D_MODEL), F32),
        compiler_params=_params(("arbitrary",)),
        name="final_norm",
    )(x2d, g.reshape(1, D_MODEL))


RWKV_BATCHES_PER_STEP = 8
RET_CHUNK = 256
S5_TIME_CHUNK = 128


MOE_TILE = 512
BF16_ROWS = 16


def _moe_cap(tm):
    return -(-(tm * 5 // 16) // BF16_ROWS) * BF16_ROWS


def _ffn_any(x2d, g, mod3, ffn, *, rows_per_group):
    if len(ffn) == 3:
        tm = min(1024, rows_per_group)
        return _ffn(x2d, g, mod3, ffn, tm=tm, tiles_per_group=rows_per_group // tm, tf=256)
    tm = min(MOE_TILE, rows_per_group)
    return _moe(x2d, g, mod3, ffn, tm=tm, tiles_per_group=rows_per_group // tm,
                cap=_moe_cap(tm))


def _prompt_layer(x2d, bsz, t_len, mod_l, lp):
    mod3 = mod_l.reshape(bsz, 1, -1)
    tm = min(2048, t_len)
    tpg = t_len // tm
    proj, u_tb = _inproj(x2d, lp['norm_mix'], mod3, lp['w_in'], tm=tm, tiles_per_group=tpg,
                         tb_shape=(t_len, bsz * BRANCH_W))
    proj3 = proj.reshape(bsz, t_len, IN_W)
    z_state = jnp.zeros((bsz, S5_W), F32)
    y_s5, s5_re, s5_im = _s5(u_tb.reshape(t_len * bsz, BRANCH_W), z_state, z_state, lp['s5p'], lp['s5_d'],
                             lp['s5_w_glu'], nb=bsz, t_len=t_len, tc=min(S5_TIME_CHUNK, t_len))
    z_mat = jnp.zeros((bsz, N_HEADS, HEAD_DIM, HEAD_DIM), F32)
    y_rwkv, s_rwkv, shift_n = _rwkv(proj3, jnp.zeros((bsz, RWKV_IN_W), F32), z_mat, lp['rwkv'],
                                    nb=RWKV_BATCHES_PER_STEP)
    y_ret, s_ret = _ret(proj3, z_mat, 0, chunk=min(RET_CHUNK, t_len))
    y_moba = _moba_prompt(proj3)
    kv = lambda k: proj3[:, :, (COL_MOBA + k) * BRANCH_W:(COL_MOBA + k + 1) * BRANCH_W].reshape(
        bsz, t_len, N_HEADS, HEAD_DIM)
    tmm = min(256, t_len)
    tpm = t_len // tmm
    s5_spec = pl.BlockSpec((tmm, BRANCH_W), lambda i: (i % tpm, i // tpm))
    flat = lambda y: y.reshape(bsz * t_len, BRANCH_W)
    x2d = _merge(x2d, y_s5.reshape(t_len, bsz * BRANCH_W), s5_spec, flat(y_rwkv), flat(y_ret), flat(y_moba),
                 proj, mod3, lp['w_branch'], lp['w_out'], tm=tmm, tiles_per_group=tpm)
    x2d = _ffn_any(x2d, lp['norm_ffn'], mod3, lp['ffn'], rows_per_group=t_len)
    g16 = (bsz, S5_GROUPS, S5_STATE)
    return x2d, (s5_re.reshape(g16), s5_im.reshape(g16), s_rwkv, shift_n, s_ret, kv(1), kv(2))


def _decode_layer(x2d, mod_l, lp, layer, pos0, s5_re0, s5_im0, rwkv_s0, shift0, ret_s0, cache_k, cache_v,
                  page_table):
    bsz = x2d.shape[0]
    mod3 = mod_l.reshape(1, bsz, -1)
    proj = _inproj(x2d, lp['norm_mix'], mod3, lp['w_in'], tm=bsz, tiles_per_group=1)[0]
    piece = lambda k: proj[:, k * BRANCH_W:(k + 1) * BRANCH_W]
    y_s5, s5_re, s5_im = _s5(piece(COL_S5), s5_re0.reshape(bsz, S5_W), s5_im0.reshape(bsz, S5_W), lp['s5p'],
                             lp['s5_d'], lp['s5_w_glu'], nb=bsz, t_len=1, tc=1)
    pre = _dec_pre(proj, shift0, lp['rwkv'], pos0)
    s_rwkv, y_raw, s_ret, o_raw = _dec_state(rwkv_s0, ret_s0, pre, piece(COL_RET + 2))
    y_rwkv, y_ret = _dec_post(y_raw, pre, o_raw, proj, lp['rwkv'])
    k_new = piece(COL_MOBA + 1)
    v_new = piece(COL_MOBA + 2)
    y_moba = _moba_dec(piece(COL_MOBA), k_new, v_new, cache_k, cache_v, page_table, layer)
    s5_spec = pl.BlockSpec((bsz, BRANCH_W), lambda i: (i, 0))
    x2d = _merge(x2d, y_s5, s5_spec, y_rwkv, y_ret, y_moba, proj, mod3, lp['w_branch'], lp['w_out'],
                 tm=bsz, tiles_per_group=1)
    x2d = _ffn_any(x2d, lp['norm_ffn'], mod3, lp['ffn'], rows_per_group=bsz)
    g16 = (bsz, S5_GROUPS, S5_STATE)
    kv4 = lambda t: t.reshape(bsz, 1, N_HEADS, HEAD_DIM)
    shift_n = proj[:, COL_RWKV * BRANCH_W:COL_RWKV * BRANCH_W + RWKV_IN_W]
    return x2d, (s5_re.reshape(g16), s5_im.reshape(g16), s_rwkv, shift_n, s_ret, kv4(k_new), kv4(v_new))


def kernel(x_prompt, x_sample, c_prompt, c_sample, state_s5_re, state_s5_im, state_rwkv, state_rwkv_shift, state_ret, cache_moba_k, cache_moba_v, page_table, norm_mix_g, norm_ffn_g, norm_final_g, w_ada, b_ada, w_in, s5_lambda_re, s5_lambda_im, s5_log_dt, s5_b_re, s5_b_im, s5_c_re, s5_c_im, s5_d, s5_w_glu, rwkv_mu, rwkv_w0, rwkv_w2, rwkv_a0, rwkv_a2, rwkv_g2, rwkv_k_k, rwkv_k_a, rwkv_r_k, rwkv_ln_g, rwkv_ln_b, w_branch, w_out, ffn_w1, ffn_w3, ffn_w2, moe_router, moe_w1, moe_w3, moe_w2):
    bp, t_len, _ = x_prompt.shape
    bs = x_sample.shape[0]
    depth = w_in.shape[0]
    past_len = page_table.shape[1] * cache_moba_k.shape[2]
    cache_kt = jnp.transpose(cache_moba_k, (0, 1, 3, 4, 2))
    cache_vt = jnp.transpose(cache_moba_v, (0, 1, 3, 4, 2))
    mod_all = _ada(jnp.concatenate([c_prompt, c_sample], axis=0), w_ada, b_ada)
    xp = x_prompt.reshape(bp * t_len, D_MODEL)
    xs = x_sample.reshape(bs, D_MODEL)
    outs_p = [[] for _ in range(7)]
    outs_s = [[] for _ in range(7)]
    for l in range(depth):
        if l % 2 == 0:
            ffn = (ffn_w1[l // 2], ffn_w3[l // 2], ffn_w2[l // 2])
        else:
            ffn = (moe_router[l // 2], moe_w1[l // 2].astype(BF16), moe_w3[l // 2].astype(BF16),
                   moe_w2[l // 2].astype(BF16))
        lp = {
            'norm_mix': norm_mix_g[l], 'norm_ffn': norm_ffn_g[l], 'w_in': w_in[l],
            's5p': _s5_params(s5_lambda_re[l], s5_lambda_im[l], s5_log_dt[l], s5_b_re[l], s5_b_im[l],
                              s5_c_re[l], s5_c_im[l]),
            's5_d': s5_d[l], 's5_w_glu': s5_w_glu[l],
            'rwkv': _rwkv_params(rwkv_mu[l], rwkv_w0[l], rwkv_w2[l], rwkv_a0[l], rwkv_a2[l], rwkv_g2[l],
                                 rwkv_k_k[l], rwkv_k_a[l], rwkv_r_k[l], rwkv_ln_g[l], rwkv_ln_b[l]),
            'w_branch': w_branch[l], 'w_out': w_out[l], 'ffn': ffn,
        }
        xp, st_p = _prompt_layer(xp, bp, t_len, mod_all[l, :bp], lp)
        xs, st_s = _decode_layer(xs, mod_all[l, bp:], lp, l, past_len, state_s5_re[l], state_s5_im[l],
                                 state_rwkv[l], state_rwkv_shift[l], state_ret[l], cache_kt, cache_vt,
                                 page_table)
        for j in range(7):
            outs_p[j].append(st_p[j])
            outs_s[j].append(st_s[j])
    y_prompt = _final_norm(xp, norm_final_g, tm=1024).reshape(bp, t_len, D_MODEL)
    y_sample = _final_norm(xs, norm_final_g, tm=bs).reshape(bs, 1, D_MODEL)
    stack = lambda outs: [jnp.stack(o, axis=0) for o in outs]
    return (y_prompt, y_sample, *stack(outs_p), *stack(outs_s))
```

```python
import functools
import math

import jax
import jax.numpy as jnp
from jax import lax
from jax.experimental import pallas as pl
from jax.experimental.pallas import tpu as pltpu

F32 = jnp.float32
BF16 = jnp.bfloat16
HIGHEST = lax.Precision.HIGHEST

D_MODEL = 1024
BRANCH_W = 256
HEAD_DIM = 64
N_HEADS = 4
N_BRANCH = 4
S5_GROUPS = 16
S5_STATE = 64
S5_CH = 16
S5_W = S5_GROUPS * S5_STATE
IN_W = 7168
RWKV_IN_W = 1024
RWKV_LN_EPS = 64e-5
RMS_EPS = 1e-6
ROPE_BASE = 10000.0
MOBA_BLOCK = 256
MOBA_TOPK = 3
N_EXPERTS = 8
ROUTER_PAD = 128
NEG_INF = float("-inf")

COL_S5 = 0
COL_RWKV = 1
COL_RET = 5
COL_MOBA = 9
COL_GATE = 12

VMEM_LIMIT = 48 * 1024 * 1024


def _params(sem):
    return pltpu.CompilerParams(dimension_semantics=sem, vmem_limit_bytes=VMEM_LIMIT)


def _dot(a, b):
    return jnp.dot(a.astype(BF16), b.astype(BF16), preferred_element_type=F32)


def _dot_hi(a, b):
    return jnp.dot(a, b, precision=HIGHEST, preferred_element_type=F32)


def _dot_nt(a, b):
    return lax.dot_general(a.astype(BF16), b.astype(BF16), (((1,), (1,)), ((), ())),
                           preferred_element_type=F32)


def _dot_nt_hi(a, b):
    return lax.dot_general(a, b, (((1,), (1,)), ((), ())), precision=HIGHEST,
                           preferred_element_type=F32)


def _dot_tn(a, b):
    return lax.dot_general(a.astype(BF16), b.astype(BF16), (((0,), (0,)), ((), ())),
                           preferred_element_type=F32)


def _head_masks(width=BRANCH_W):
    lane = lax.broadcasted_iota(jnp.int32, (1, width), 1)
    return [(lane // HEAD_DIM == h).astype(F32) for h in range(N_HEADS)]


def _head_ones():
    r = lax.broadcasted_iota(jnp.int32, (BRANCH_W, BRANCH_W), 0) // HEAD_DIM
    c = lax.broadcasted_iota(jnp.int32, (BRANCH_W, BRANCH_W), 1) // HEAD_DIM
    return (r == c).astype(F32)


def _split3(x):
    hi = x.astype(BF16)
    rest = x - hi.astype(F32)
    mid = rest.astype(BF16)
    return hi, mid, (rest - mid.astype(F32)).astype(BF16)


def _head_sum(x, ones_bd):
    w = ones_bd.astype(BF16)
    return sum(jnp.dot(p, w, preferred_element_type=F32) for p in _split3(x))


def _ada_kernel(c_ref, w_ref, b_ref, o_ref):
    c = c_ref[...]
    h = c * jax.nn.sigmoid(c)
    o_ref[0] = _dot(h, w_ref[0]) + b_ref[0]


def _ada(c_all, w_ada, b_ada):
    depth, _, width = w_ada.shape
    rows = c_all.shape[0]
    tn = 1024
    return pl.pallas_call(
        _ada_kernel,
        grid=(depth, width // tn),
        in_specs=[pl.BlockSpec((rows, D_MODEL), lambda l, j: (0, 0)),
                  pl.BlockSpec((1, D_MODEL, tn), lambda l, j: (l, 0, j)),
                  pl.BlockSpec((1, 1, tn), lambda l, j: (l, 0, j))],
        out_specs=pl.BlockSpec((1, rows, tn), lambda l, j: (l, 0, j)),
        out_shape=jax.ShapeDtypeStruct((depth, rows, width), F32),
        compiler_params=_params(("arbitrary", "arbitrary")),
        name="ada_mod",
    )(c_all, w_ada, b_ada.reshape(depth, 1, width))


def _modulated_norm(x, g, sc, sh):
    y = x * lax.rsqrt(jnp.mean(x * x, axis=-1, keepdims=True) + RMS_EPS) * g
    return y * (1.0 + sc) + sh


INPROJ_TN = 512
MIX_W = COL_GATE * BRANCH_W
GATE_W = IN_W - MIX_W


def _inproj_kernel(x_ref, g_ref, sc_ref, sh_ref, w_ref, o_ref, gate_ref, *rest, emit_tb):
    h_scr = rest[-1]
    j = pl.program_id(1)
    n_mix = MIX_W // INPROJ_TN

    @pl.when(j == 0)
    def _():
        h_scr[...] = _modulated_norm(x_ref[...], g_ref[...], sc_ref[0], sh_ref[0]).astype(BF16)

    acc = jnp.dot(h_scr[...], w_ref[...].astype(BF16), preferred_element_type=F32)

    @pl.when(j < n_mix)
    def _():
        o_ref[...] = acc

    @pl.when(j >= n_mix)
    def _():
        gate_ref[...] = acc.astype(BF16)

    if emit_tb:
        u_ref = rest[0]

        @pl.when(j == 0)
        def _():
            u_ref[...] = acc[:, :BRANCH_W]


def _inproj(x2d, g, mod3, w, *, tm, tiles_per_group, tb_shape=None):
    rows = x2d.shape[0]
    m = mod3.shape[1]
    tn = INPROJ_TN
    n_mix = MIX_W // tn
    tpg = tiles_per_group
    in_specs = [pl.BlockSpec((tm, D_MODEL), lambda i, j: (i, 0)),
                pl.BlockSpec((1, D_MODEL), lambda i, j: (0, 0)),
                pl.BlockSpec((1, m, D_MODEL), lambda i, j: (i // tpg, 0, 1)),
                pl.BlockSpec((1, m, D_MODEL), lambda i, j: (i // tpg, 0, 0)),
                pl.BlockSpec((D_MODEL, tn), lambda i, j: (0, j))]
    out_specs = [pl.BlockSpec((tm, tn), lambda i, j: (i, jnp.minimum(j, n_mix - 1))),
                 pl.BlockSpec((tm, tn), lambda i, j: (i, jnp.maximum(j - n_mix, 0)))]
    out_shape = [jax.ShapeDtypeStruct((rows, MIX_W), F32), jax.ShapeDtypeStruct((rows, GATE_W), BF16)]
    if tb_shape is not None:
        out_specs.append(pl.BlockSpec((tm, BRANCH_W), lambda i, j: (i % tpg, i // tpg)))
        out_shape.append(jax.ShapeDtypeStruct(tb_shape, F32))
    return pl.pallas_call(
        functools.partial(_inproj_kernel, emit_tb=tb_shape is not None),
        grid=(rows // tm, IN_W // tn),
        in_specs=in_specs, out_specs=out_specs, out_shape=out_shape,
        scratch_shapes=[pltpu.VMEM((tm, D_MODEL), BF16)],
        compiler_params=_params(("arbitrary", "arbitrary")),
        name="inproj",
    )(x2d, g.reshape(1, D_MODEL), mod3, mod3, w)


def _s5_kernel(u_ref, x0r_ref, x0i_ref, lbr_ref, lbi_ref, br_ref, bi_ref, cr_ref, ci_ref, d_ref, wg_ref,
               y_ref, sr_ref, si_ref, bur, bui, xr, xi, *, nb, tc):
    c = pl.program_id(0)

    @pl.when(c == 0)
    def _():
        xr[...] = x0r_ref[...]
        xi[...] = x0i_ref[...]

    u = u_ref[...]
    ub = u.astype(BF16)
    bur[...] = jnp.dot(ub, br_ref[...].astype(BF16), preferred_element_type=F32)
    bui[...] = jnp.dot(ub, bi_ref[...].astype(BF16), preferred_element_type=F32)
    lbr = jnp.broadcast_to(lbr_ref[...], (nb, S5_W))
    lbi = jnp.broadcast_to(lbi_ref[...], (nb, S5_W))

    def body(t, carry):
        sr, si = carry
        rows = pl.ds(pl.multiple_of(t * nb, nb), nb)
        nr = lbr * sr - lbi * si + bur[rows, :]
        ni = lbr * si + lbi * sr + bui[rows, :]
        bur[rows, :] = nr
        bui[rows, :] = ni
        return nr, ni

    sr, si = lax.fori_loop(0, tc, body, (xr[...], xi[...]))
    xr[...] = sr
    xi[...] = si
    sr_ref[...] = sr
    si_ref[...] = si
    y = _dot(bur[...], cr_ref[...]) - _dot(bui[...], ci_ref[...]) + d_ref[...] * u
    z = jax.nn.gelu(y)
    y_ref[...] = z * jax.nn.sigmoid(_dot(z, wg_ref[...]))


def _s5_params(lam_re, lam_im, log_dt, b_re, b_im, c_re, c_im):
    dt = jnp.exp(log_dt)[:, None]
    mag = jnp.exp(lam_re * dt)
    lb_re = mag * jnp.cos(lam_im * dt)
    lb_im = mag * jnp.sin(lam_im * dt)
    den = lam_re * lam_re + lam_im * lam_im
    q_re = ((lb_re - 1.0) * lam_re + lb_im * lam_im) / den
    q_im = (lb_im * lam_re - (lb_re - 1.0) * lam_im) / den
    bb_re = q_re[..., None] * b_re - q_im[..., None] * b_im
    bb_im = q_re[..., None] * b_im + q_im[..., None] * b_re
    eye = jnp.eye(S5_GROUPS, dtype=F32)
    to_in = lambda bb: jnp.einsum('gnc,gh->gchn', bb, eye).reshape(BRANCH_W, S5_W)
    to_out = lambda cc: jnp.einsum('gcn,gh->gnhc', cc, eye).reshape(S5_W, BRANCH_W)
    return (lb_re.reshape(1, S5_W), lb_im.reshape(1, S5_W), to_in(bb_re), to_in(bb_im),
            to_out(c_re), to_out(c_im))


def _s5(u_tb, x0_re, x0_im, s5p, d_skip, w_glu, *, nb, t_len, tc):
    lb_re, lb_im, bin_re, bin_im, cout_re, cout_im = s5p
    rows = tc * nb
    const = lambda shape: pl.BlockSpec(shape, lambda c: (0,) * len(shape))
    return pl.pallas_call(
        functools.partial(_s5_kernel, nb=nb, tc=tc),
        grid=(t_len // tc,),
        in_specs=[pl.BlockSpec((rows, BRANCH_W), lambda c: (c, 0)),
                  const((nb, S5_W)), const((nb, S5_W)), const((1, S5_W)), const((1, S5_W)),
                  const((BRANCH_W, S5_W)), const((BRANCH_W, S5_W)),
                  const((S5_W, BRANCH_W)), const((S5_W, BRANCH_W)),
                  const((1, BRANCH_W)), const((BRANCH_W, BRANCH_W))],
        out_specs=[pl.BlockSpec((rows, BRANCH_W), lambda c: (c, 0)), const((nb, S5_W)), const((nb, S5_W))],
        out_shape=[jax.ShapeDtypeStruct((t_len * nb, BRANCH_W), F32),
                   jax.ShapeDtypeStruct((nb, S5_W), F32), jax.ShapeDtypeStruct((nb, S5_W), F32)],
        scratch_shapes=[pltpu.VMEM((rows, S5_W), F32), pltpu.VMEM((rows, S5_W), F32),
                        pltpu.VMEM((nb, S5_W), F32), pltpu.VMEM((nb, S5_W), F32)],
        compiler_params=_params(("arbitrary",)),
        name="s5",
    )(u_tb, x0_re, x0_im, lb_re, lb_im, bin_re, bin_im, cout_re, cout_im,
      d_skip.reshape(1, BRANCH_W), w_glu)


def _rwkv_features(pm_r, pm_k, pm_v, pm_l, prm, ones_bd):
    w0, w2p, a0, a2p, g2p, k_k, k_a = prm
    w_raw = w0 + _dot(jnp.tanh(pm_l), w2p)
    lw = -jax.nn.sigmoid(w_raw) * math.exp(-0.5)
    a_sig = jax.nn.sigmoid(a0 + _dot(pm_l, a2p))
    g = _dot(jax.nn.sigmoid(pm_l), g2p)
    kk = pm_k * k_k
    kk = kk * lax.rsqrt(_head_sum(kk * kk, ones_bd) + 1e-12)
    k_mod = pm_k * (1.0 + (a_sig - 1.0) * k_a)
    return pm_r, k_mod, pm_v, lw, -kk, kk * a_sig, g


def _rwkv_post(y, r, k_mod, v, g, r_k, ln_g, ln_b, ones_bd):
    inv = 1.0 / HEAD_DIM
    mean = _head_sum(y, ones_bd) * inv
    yc = y - mean
    var = _head_sum(yc * yc, ones_bd) * inv
    yn = yc * lax.rsqrt(var + RWKV_LN_EPS) * ln_g + ln_b
    bonus = _head_sum(r * k_mod * r_k, ones_bd) * v
    return (yn + bonus) * g


RWKV_CHUNK = 64


def _cumsum_rows(tril, x):
    return sum(jnp.dot(tril, p, preferred_element_type=F32) for p in _split3(x))


def _per_head(x, ones_bd):
    return jnp.concatenate([x.astype(BF16)] * N_HEADS, axis=0) * ones_bd.astype(BF16)


def _rwkv_chunks(r, k_mod, v, lw, aa, bb, states, ones_bd):
    n = RWKV_CHUNK
    nb = len(states)
    seqs = range(nb)
    part = lambda x, b: x[b * n:(b + 1) * n]
    t_idx = lax.broadcasted_iota(jnp.int32, (n, BRANCH_W), 0)
    i_idx = lax.broadcasted_iota(jnp.int32, (n, BRANCH_W), 1) % n
    strict = (t_idx > i_idx).astype(F32)
    incl = (t_idx >= i_idx).astype(F32)
    eye_c = (t_idx == i_idx).astype(F32)
    tril = (lax.broadcasted_iota(jnp.int32, (n, n), 0) >= lax.broadcasted_iota(jnp.int32, (n, n), 1)).astype(BF16)
    bd = lambda x: _per_head(x, ones_bd)
    cum = jnp.concatenate([_cumsum_rows(tril, part(lw, b)) for b in seqs], axis=0)
    tot = [part(cum, b)[n - 1:n, :] for b in seqs]
    tot_rows = jnp.concatenate([jnp.broadcast_to(t, (n, BRANCH_W)) for t in tot], axis=0)
    e_neg = jnp.exp(-cum)
    e_rem = jnp.exp(tot_rows - cum)
    a_t = aa * jnp.exp(cum - lw)
    r_t = r * jnp.exp(cum)
    b_t = bb * e_neg
    k_t = k_mod * e_neg
    b_h = bb * e_rem
    k_h = k_mod * e_rem
    ar = [jnp.concatenate([part(a_t, b), part(r_t, b)], axis=0) for b in seqs]
    xb = [_dot_nt(ar[b], bd(part(b_t, b))) for b in seqs]
    xk = [_dot_nt(ar[b], bd(part(k_t, b))) for b in seqs]
    ss = [_dot_nt(ar[b], states[b]) for b in seqs]
    v_bd = [bd(part(v, b)) for b in seqs]
    pw = [xb[b][:n] * strict for b in seqs]
    mv = [_dot(xk[b][:n] * strict, v_bd[b]) for b in seqs]
    tinv = [eye_c + pw[b] for b in seqs]
    for _ in range(int(math.log2(n)) - 1):
        pw = [_dot(pw[b], bd(pw[b])) for b in seqs]
        tinv = [tinv[b] + _dot(pw[b], bd(tinv[b])) for b in seqs]
    u = [_dot(tinv[b], bd(ss[b][:n] + mv[b])) for b in seqs]
    y = [ss[b][n:] + _dot(xb[b][n:] * incl, bd(u[b])) + _dot(xk[b][n:] * incl, v_bd[b]) for b in seqs]
    upd = [_dot_tn(jnp.concatenate([u[b], part(v, b)], axis=0),
                   jnp.concatenate([part(b_h, b), part(k_h, b)], axis=0)) for b in seqs]
    new_states = [states[b] * jnp.exp(tot[b]) + ones_bd * upd[b] for b in seqs]
    return jnp.concatenate(y, axis=0), new_states


def _rwkv_kernel(r_ref, k_ref, v_ref, l_ref, shift_ref, s0_ref, mu_ref, w0_ref, w2_ref, a0_ref, a2_ref,
                 g2_ref, kk_ref, ka_ref, rk_ref, lng_ref, lnb_ref,
                 y_ref, s_ref, sh_ref, prev_scr, s_scr, *, nb):
    c = pl.program_id(1)
    n = RWKV_CHUNK

    @pl.when(c == 0)
    def _():
        prev_scr[...] = shift_ref[...]
        s_scr[...] = s0_ref[...]

    row = lax.broadcasted_iota(jnp.int32, (n, BRANCH_W), 0)
    ones_bd = _head_ones()
    prm = (w0_ref[...], w2_ref[...], a0_ref[...], a2_ref[...], g2_ref[...], kk_ref[...], ka_ref[...])
    pieces = []
    for idx, ref in enumerate((r_ref, k_ref, v_ref, l_ref)):
        lanes = slice(idx * BRANCH_W, (idx + 1) * BRANCH_W)
        shifted = []
        for b in range(nb):
            x = ref[b]
            x_prev = jnp.where(row == 0, prev_scr[b, :, lanes], pltpu.roll(x, 1, 0))
            shifted.append(x + (x_prev - x) * mu_ref[:, lanes])
            prev_scr[b, :, lanes] = x[n - 1:n, :]
            sh_ref[b, :, lanes] = x[n - 1:n, :]
        pieces.append(jnp.concatenate(shifted, axis=0))
    r, k_mod, v, lw, aa, bb, g = _rwkv_features(*pieces, prm, ones_bd)
    y, new_states = _rwkv_chunks(r, k_mod, v, lw, aa, bb, [s_scr[b] for b in range(nb)], ones_bd)
    out = _rwkv_post(y, r, k_mod, v, g, rk_ref[...], lng_ref[...], lnb_ref[...], ones_bd)
    for b in range(nb):
        s_scr[b] = new_states[b]
        s_ref[b] = new_states[b]
        y_ref[b] = out[b * n:(b + 1) * n]


def _rwkv_params(mu, w0, w2, a0, a2, g2, k_k, k_a, r_k, ln_g, ln_b):
    row = lambda t: t.reshape(1, -1)
    pad = lambda w, lo: jnp.zeros((BRANCH_W, BRANCH_W), F32).at[lo:lo + w.shape[0]].set(w)
    return (row(mu), row(w0), pad(w2, 0), row(a0), pad(a2, 64), pad(g2, 128), row(k_k), row(k_a),
            row(r_k), row(ln_g), row(ln_b))


def _diag_blocks(s_bd):
    bsz = s_bd.shape[0]
    s5 = s_bd.reshape(bsz, N_HEADS, HEAD_DIM, N_HEADS, HEAD_DIM)
    return jnp.stack([s5[:, h, :, h, :] for h in range(N_HEADS)], axis=1)


def _rwkv(proj3, shift0, s0, rp, *, nb):
    bsz, t_len, _ = proj3.shape
    chunk = RWKV_CHUNK
    col = lambda k: pl.BlockSpec((nb, chunk, BRANCH_W), lambda b, c: (b, c, COL_RWKV + k))
    const = lambda shape: pl.BlockSpec(shape, lambda b, c: (0,) * len(shape))
    vec = const((1, BRANCH_W))
    mat = const((BRANCH_W, BRANCH_W))
    y, s_bd, shift_n = pl.pallas_call(
        functools.partial(_rwkv_kernel, nb=nb),
        grid=(bsz // nb, t_len // chunk),
        in_specs=[col(0), col(1), col(2), col(3),
                  pl.BlockSpec((nb, 1, RWKV_IN_W), lambda b, c: (b, 0, 0)),
                  pl.BlockSpec((nb, BRANCH_W, BRANCH_W), lambda b, c: (b, 0, 0)),
                  const((1, RWKV_IN_W)), vec, mat, vec, mat, mat, vec, vec, vec, vec, vec],
        out_specs=[pl.BlockSpec((nb, chunk, BRANCH_W), lambda b, c: (b, c, 0)),
                   pl.BlockSpec((nb, BRANCH_W, BRANCH_W), lambda b, c: (b, 0, 0)),
                   pl.BlockSpec((nb, 1, RWKV_IN_W), lambda b, c: (b, 0, 0))],
        out_shape=[jax.ShapeDtypeStruct((bsz, t_len, BRANCH_W), F32),
                   jax.ShapeDtypeStruct((bsz, BRANCH_W, BRANCH_W), F32),
                   jax.ShapeDtypeStruct((bsz, 1, RWKV_IN_W), F32)],
        scratch_shapes=[pltpu.VMEM((nb, 1, RWKV_IN_W), F32), pltpu.VMEM((nb, BRANCH_W, BRANCH_W), F32)],
        compiler_params=_params(("arbitrary", "arbitrary")),
        name="rwkv",
    )(proj3, proj3, proj3, proj3, shift0.reshape(bsz, 1, RWKV_IN_W), s0, *rp)
    return y, _diag_blocks(s_bd), shift_n.reshape(bsz, RWKV_IN_W)


def _rope_tables(t_len, pos0):
    half = HEAD_DIM // 2
    freqs = 1.0 / (ROPE_BASE ** jnp.linspace(0.0, 1.0, half, dtype=F32))
    pos = jnp.arange(t_len, dtype=F32) + pos0
    ang = pos[:, None] * freqs[None, :]
    cos = jnp.cos(ang)
    sin = jnp.sin(ang)
    cos_t = jnp.tile(jnp.concatenate([cos, cos], axis=-1), (1, N_HEADS))
    sin_t = jnp.tile(jnp.concatenate([-sin, sin], axis=-1), (1, N_HEADS))
    return cos_t, sin_t


def _rope(x, cos_t, sin_t):
    lane = lax.broadcasted_iota(jnp.int32, x.shape, 1)
    first = (lane % HEAD_DIM) < (HEAD_DIM // 2)
    swapped = jnp.where(first, pltpu.roll(x, BRANCH_W - HEAD_DIM // 2, 1), pltpu.roll(x, HEAD_DIM // 2, 1))
    return x * cos_t + swapped * sin_t


def _ret_tables(chunk):
    log_gamma = jnp.log(1.0 - jnp.exp2(-5.0 - jnp.arange(N_HEADS, dtype=F32)))
    i = jnp.arange(chunk, dtype=F32)
    diff = i[:, None] - i[None, :]
    dmask = jnp.where(diff >= 0, jnp.exp(log_gamma[:, None, None] * jnp.maximum(diff, 0.0)), 0.0)
    lanes = lambda t: jnp.repeat(t, HEAD_DIM, axis=-1)
    xi = lanes(jnp.exp(log_gamma[None, :] * (i[:, None] + 1.0)))
    zeta = lanes(jnp.exp(log_gamma[None, :] * (chunk - 1.0 - i[:, None])))
    g_chunk = lanes(jnp.exp(log_gamma * chunk)[None, :])
    return dmask, xi, zeta, g_chunk


def _ret_kernel(q_ref, k_ref, v_ref, g_ref, s0_ref, cos_ref, sin_ref, dm_ref, xi_ref, zeta_ref, gch_ref,
                y_ref, s_ref, s_scr):
    c = pl.program_id(1)

    @pl.when(c == 0)
    def _():
        s_scr[...] = s0_ref[0]

    cos_t = cos_ref[...]
    sin_t = sin_ref[...]
    q = _rope(q_ref[0], cos_t, sin_t)
    k = _rope(k_ref[0], cos_t, sin_t) * (HEAD_DIM ** -0.5)
    v = v_ref[0]
    s = s_scr[...]
    ones_bd = _head_ones()
    o = _dot(q, s) * xi_ref[...]
    for h, mh in enumerate(_head_masks()):
        att = _dot_nt(q * mh, k) * dm_ref[h]
        o = o + _dot(att, v) * mh
    s_new = s * gch_ref[...] + ones_bd * _dot_tn(k * zeta_ref[...], v)
    s_scr[...] = s_new
    s_ref[0] = s_new
    o = o * lax.rsqrt(_head_sum(o * o, ones_bd) * (1.0 / HEAD_DIM) + RMS_EPS)
    g = g_ref[0]
    y_ref[0] = o * (g * jax.nn.sigmoid(g))


def _ret(proj3, s0, pos0, *, chunk):
    bsz, t_len, _ = proj3.shape
    cos_t, sin_t = _rope_tables(t_len, pos0)
    dmask, xi, zeta, g_chunk = _ret_tables(chunk)
    col = lambda k: pl.BlockSpec((1, chunk, BRANCH_W), lambda b, c: (b, c, COL_RET + k))
    const = lambda shape: pl.BlockSpec(shape, lambda b, c: (0,) * len(shape))
    tab = pl.BlockSpec((chunk, BRANCH_W), lambda b, c: (c, 0))
    y, s_bd = pl.pallas_call(
        _ret_kernel,
        grid=(bsz, t_len // chunk),
        in_specs=[col(0), col(1), col(2), col(3),
                  pl.BlockSpec((1, BRANCH_W, BRANCH_W), lambda b, c: (b, 0, 0)),
                  tab, tab, const((N_HEADS, chunk, chunk)), const((chunk, BRANCH_W)),
                  const((chunk, BRANCH_W)), const((1, BRANCH_W))],
        out_specs=[pl.BlockSpec((1, chunk, BRANCH_W), lambda b, c: (b, c, 0)),
                   pl.BlockSpec((1, BRANCH_W, BRANCH_W), lambda b, c: (b, 0, 0))],
        out_shape=[jax.ShapeDtypeStruct((bsz, t_len, BRANCH_W), F32),
                   jax.ShapeDtypeStruct((bsz, BRANCH_W, BRANCH_W), F32)],
        scratch_shapes=[pltpu.VMEM((BRANCH_W, BRANCH_W), F32)],
        compiler_params=_params(("arbitrary", "arbitrary")),
        name="retention",
    )(proj3, proj3, proj3, proj3, s0, cos_t, sin_t, dmask, xi, zeta, g_chunk)
    return y, _diag_blocks(s_bd)


def _topk_rows(gs_t, n_valid):
    nblk = gs_t.shape[0]
    blk = lax.broadcasted_iota(jnp.int32, gs_t.shape, 0)
    valid = blk < n_valid
    gsm = jnp.where(valid, gs_t, NEG_INF)
    cnt = jnp.zeros(gs_t.shape, F32)
    for m in range(nblk):
        row = gsm[m:m + 1, :]
        cnt = cnt + jnp.where(row > gsm, 1.0, jnp.where((row == gsm) & (blk > m), 1.0, 0.0))
    return jnp.where(valid & (cnt < MOBA_TOPK), 1.0, 0.0)


def _dot_nt3(a, b):
    ah = a.astype(BF16)
    al = (a - ah.astype(F32)).astype(BF16)
    bh = b.astype(BF16)
    bl = (b - bh.astype(F32)).astype(BF16)
    d = lambda x, y: lax.dot_general(x, y, (((1,), (1,)), ((), ())), preferred_element_type=F32)
    return d(ah, bh) + d(ah, bl) + d(al, bh)


def _moba_kernel(q_ref, k_ref, v_ref, o_ref, km_scr, kb_scr, vb_scr, *, nblk):
    qi = pl.program_id(1)
    bs = MOBA_BLOCK
    masks = _head_masks()

    @pl.when(qi == 0)
    def _():
        for n in range(nblk):
            rows = slice(n * bs, (n + 1) * bs)
            kblk = k_ref[0, rows, :]
            kb_scr[rows, :] = kblk.astype(BF16)
            vb_scr[rows, :] = v_ref[0, rows, :].astype(BF16)
            km = jnp.mean(kblk, axis=0, keepdims=True)
            for h, mh in enumerate(masks):
                km_scr[h * nblk + n:h * nblk + n + 1, :] = km * mh

    q = q_ref[0]
    scale = HEAD_DIM ** -0.5
    ri = lax.broadcasted_iota(jnp.int32, (bs, bs), 0)
    ci = lax.broadcasted_iota(jnp.int32, (bs, bs), 1)
    eye_b = (lax.broadcasted_iota(jnp.int32, (nblk, 128), 0)
             == lax.broadcasted_iota(jnp.int32, (nblk, 128), 1)).astype(F32)
    blk = lax.broadcasted_iota(jnp.int32, (1, 128), 1)
    own = pl.ds(pl.multiple_of(qi * bs, bs), bs)
    k_own = kb_scr[own, :]
    v_own = vb_scr[own, :]
    gs_all = _dot_nt3(km_scr[...], q)
    heads = range(N_HEADS)
    nt = lambda x, y: lax.dot_general(x, y, (((1,), (1,)), ((), ())), preferred_element_type=F32)
    pv = lambda p, vblk: jnp.dot(p.astype(BF16), vblk, preferred_element_type=F32)
    spread = lambda cols: sum(masks[h] * cols[h] for h in heads)
    qh = [(q * masks[h]).astype(BF16) for h in heads]
    sel = [lax.dot_general(_topk_rows(gs_all[h * nblk:(h + 1) * nblk], qi), eye_b, (((0,), (0,)), ((), ())),
                           preferred_element_type=F32) for h in heads]
    s = [jnp.where(ci <= ri, nt(qh[h], k_own) * scale, NEG_INF) for h in heads]
    m0 = [jnp.max(s[h], axis=1, keepdims=True) for h in heads]
    p = [jnp.exp(s[h] - m0[h]) for h in heads]
    l0 = [jnp.sum(p[h], axis=1, keepdims=True) for h in heads]
    acc0 = sum(masks[h] * pv(p[h], v_own) for h in heads)

    def body(n, carry):
        m, l, acc = carry
        rows = pl.ds(pl.multiple_of(n * bs, bs), bs)
        kblk = kb_scr[rows, :]
        vblk = vb_scr[rows, :]
        seln = [jnp.sum(jnp.where(blk == n, sel[h], 0.0), axis=1, keepdims=True) for h in heads]
        s = [jnp.where(seln[h] > 0.0, nt(qh[h], kblk) * scale, NEG_INF) for h in heads]
        m_new = [jnp.maximum(m[h], jnp.max(s[h], axis=1, keepdims=True)) for h in heads]
        alpha = [jnp.exp(m[h] - m_new[h]) for h in heads]
        p = [jnp.exp(s[h] - m_new[h]) for h in heads]
        l = [alpha[h] * l[h] + jnp.sum(p[h], axis=1, keepdims=True) for h in heads]
        acc = spread(alpha) * acc + sum(masks[h] * pv(p[h], vblk) for h in heads)
        return tuple(m_new), tuple(l), acc

    _, l, acc = lax.fori_loop(0, qi, body, (tuple(m0), tuple(l0), acc0))
    o_ref[0] = acc / spread(l)


def _moba_prompt(proj3):
    bsz, t_len, _ = proj3.shape
    nblk = t_len // MOBA_BLOCK
    full = lambda k: pl.BlockSpec((1, t_len, BRANCH_W), lambda b, i: (b, 0, COL_MOBA + k))
    return pl.pallas_call(
        functools.partial(_moba_kernel, nblk=nblk),
        grid=(bsz, nblk),
        in_specs=[pl.BlockSpec((1, MOBA_BLOCK, BRANCH_W), lambda b, i: (b, i, COL_MOBA)), full(1), full(2)],
        out_specs=pl.BlockSpec((1, MOBA_BLOCK, BRANCH_W), lambda b, i: (b, i, 0)),
        out_shape=jax.ShapeDtypeStruct((bsz, t_len, BRANCH_W), F32),
        scratch_shapes=[pltpu.VMEM((N_HEADS * nblk, BRANCH_W), F32), pltpu.VMEM((t_len, BRANCH_W), BF16),
                        pltpu.VMEM((t_len, BRANCH_W), BF16)],
        compiler_params=_params(("arbitrary", "arbitrary")),
        name="moba_prompt",
    )(proj3, proj3, proj3)


def _merge_kernel(x_ref, y0_ref, y1_ref, y2_ref, y3_ref, g0_ref, g1_ref, g2_ref, g3_ref, gt_ref,
                  wb_ref, wo_ref, o_ref, wb_scr, wo_scr):
    @pl.when(pl.program_id(0) == 0)
    def _():
        wb_scr[...] = wb_ref[...].astype(BF16)
        wo_scr[...] = wo_ref[...].astype(BF16)

    mixed = None
    for g, (y_ref, g_ref) in enumerate(((y0_ref, g0_ref), (y1_ref, g1_ref), (y2_ref, g2_ref), (y3_ref, g3_ref))):
        up = jnp.dot(y_ref[...].astype(BF16), wb_scr[g], preferred_element_type=F32)
        term = jax.nn.sigmoid(g_ref[...].astype(F32)) * up
        mixed = term if mixed is None else mixed + term
    o_ref[...] = x_ref[...] + gt_ref[0] * jnp.dot(mixed.astype(BF16), wo_scr[...], preferred_element_type=F32)


def _merge(x2d, y_s5, s5_spec, y_rwkv, y_ret, y_moba, gates, mod3, w_branch, w_out, *, tm, tiles_per_group):
    rows = x2d.shape[0]
    m = mod3.shape[1]
    tpg = tiles_per_group
    ysp = pl.BlockSpec((tm, BRANCH_W), lambda i: (i, 0))
    gate = lambda g: pl.BlockSpec((tm, D_MODEL), lambda i: (i, g))
    return pl.pallas_call(
        _merge_kernel,
        grid=(rows // tm,),
        in_specs=[pl.BlockSpec((tm, D_MODEL), lambda i: (i, 0)), s5_spec, ysp, ysp, ysp,
                  gate(0), gate(1), gate(2), gate(3),
                  pl.BlockSpec((1, m, D_MODEL), lambda i: (i // tpg, 0, 2)),
                  pl.BlockSpec((N_BRANCH, BRANCH_W, D_MODEL), lambda i: (0, 0, 0)),
                  pl.BlockSpec((D_MODEL, D_MODEL), lambda i: (0, 0))],
        out_specs=pl.BlockSpec((tm, D_MODEL), lambda i: (i, 0)),
        out_shape=jax.ShapeDtypeStruct((rows, D_MODEL), F32),
        scratch_shapes=[pltpu.VMEM((N_BRANCH, BRANCH_W, D_MODEL), BF16), pltpu.VMEM((D_MODEL, D_MODEL), BF16)],
        compiler_params=_params(("arbitrary",)),
        name="merge",
    )(x2d, y_s5, y_rwkv, y_ret, y_moba, gates, gates, gates, gates, mod3, w_branch, w_out)


def _router_combine(h, router):
    logits = _dot_hi(h, router)
    lane = lax.broadcasted_iota(jnp.int32, logits.shape, 1)
    logits = jnp.where(lane < N_EXPERTS, logits, NEG_INF)
    m1 = jnp.max(logits, axis=1, keepdims=True)
    i1 = jnp.min(jnp.where(logits == m1, lane, ROUTER_PAD), axis=1, keepdims=True)
    rest = jnp.where(lane == i1, NEG_INF, logits)
    m2 = jnp.max(rest, axis=1, keepdims=True)
    i2 = jnp.min(jnp.where(rest == m2, lane, ROUTER_PAD), axis=1, keepdims=True)
    e2 = jnp.exp(m2 - m1)
    den = 1.0 + e2
    comb = jnp.where(lane == i1, 1.0 / den, 0.0) + jnp.where(lane == i2, e2 / den, 0.0)
    return comb, jnp.where((lane == i1) | (lane == i2), 1.0, 0.0)


def _swiglu(hb, w1, w3, w2):
    a = jnp.dot(hb, w1, preferred_element_type=F32)
    b = jnp.dot(hb, w3, preferred_element_type=F32)
    act = (a * jax.nn.sigmoid(a)) * b
    return jnp.dot(act.astype(BF16), w2, preferred_element_type=F32)


def _ffn_kernel(x_ref, g_ref, sc_ref, sh_ref, gt_ref, w1_ref, w3_ref, w2_ref, o_ref, h_scr, acc_scr, *, n_j):
    j = pl.program_id(1)

    @pl.when(j == 0)
    def _():
        h_scr[...] = _modulated_norm(x_ref[...], g_ref[...], sc_ref[0], sh_ref[0]).astype(BF16)
        acc_scr[...] = jnp.zeros(acc_scr.shape, F32)

    acc_scr[...] += _swiglu(h_scr[...], w1_ref[...].astype(BF16), w3_ref[...].astype(BF16),
                            w2_ref[...].astype(BF16))

    @pl.when(j == n_j - 1)
    def _():
        o_ref[...] = x_ref[...] + gt_ref[0] * acc_scr[...]


def _ffn(x2d, g, mod3, weights, *, tm, tiles_per_group, tf):
    rows = x2d.shape[0]
    m = mod3.shape[1]
    tpg = tiles_per_group
    w1, w3, w2 = weights
    n_j = w1.shape[1] // tf
    modspec = lambda k: pl.BlockSpec((1, m, D_MODEL), lambda i, j: (i // tpg, 0, k))
    return pl.pallas_call(
        functools.partial(_ffn_kernel, n_j=n_j),
        grid=(rows // tm, n_j),
        in_specs=[pl.BlockSpec((tm, D_MODEL), lambda i, j: (i, 0)),
                  pl.BlockSpec((1, D_MODEL), lambda i, j: (0, 0)),
                  modspec(4), modspec(3), modspec(5),
                  pl.BlockSpec((D_MODEL, tf), lambda i, j: (0, j)),
                  pl.BlockSpec((D_MODEL, tf), lambda i, j: (0, j)),
                  pl.BlockSpec((tf, D_MODEL), lambda i, j: (j, 0))],
        out_specs=pl.BlockSpec((tm, D_MODEL), lambda i, j: (i, 0)),
        out_shape=jax.ShapeDtypeStruct((rows, D_MODEL), F32),
        scratch_shapes=[pltpu.VMEM((tm, D_MODEL), BF16), pltpu.VMEM((tm, D_MODEL), F32)],
        compiler_params=_params(("arbitrary", "arbitrary")),
        name="dense_ffn",
    )(x2d, g.reshape(1, D_MODEL), mod3, mod3, mod3, w1, w3, w2)


def _moe_kernel(x_ref, g_ref, sc_ref, sh_ref, gt_ref, rt_ref, w1_ref, w3_ref, w2_ref, o_ref,
                h_scr, comb_scr, asg_scr, rank_scr, *, n_e, cap, sub):
    e = pl.program_id(1)
    tm = x_ref.shape[0]
    subs = [slice(s * sub, (s + 1) * sub) for s in range(tm // sub)]

    @pl.when(e == 0)
    def _():
        o_ref[...] = jnp.zeros(o_ref.shape, F32)
        below = (lax.broadcasted_iota(jnp.int32, (sub, sub), 0)
                 > lax.broadcasted_iota(jnp.int32, (sub, sub), 1)).astype(BF16)
        for rows in subs:
            per_row = sc_ref.shape[1] > 1
            sc = sc_ref[0, rows, :] if per_row else sc_ref[0]
            sh = sh_ref[0, rows, :] if per_row else sh_ref[0]
            h = _modulated_norm(x_ref[rows, :], g_ref[...], sc, sh)
            h_scr[rows, :] = h.astype(BF16)
            comb, asg = _router_combine(h, rt_ref[...])
            comb_scr[rows, :] = comb
            asg_scr[rows, :] = asg
            rank_scr[rows, :] = jnp.dot(below, asg.astype(BF16), preferred_element_type=F32)

    lane = lax.broadcasted_iota(jnp.int32, (sub, ROUTER_PAD), 1)
    slot = lax.broadcasted_iota(jnp.int32, (1, cap), 1).astype(F32)
    w1 = w1_ref[0]
    w3 = w3_ref[0]
    w2 = w2_ref[0]
    for rows in subs:
        pick = lambda ref: jnp.sum(jnp.where(lane == e, ref[rows, :], 0.0), axis=1, keepdims=True)
        a_col = pick(asg_scr)
        r_col = pick(rank_scr)
        c_col = pick(comb_scr)
        count = jnp.sum(a_col).astype(jnp.int32)

        def one_pass(pi, carry, rows=rows, a_col=a_col, r_col=r_col, c_col=c_col):
            base = (pi * cap).astype(F32)
            sel = jnp.where((r_col - base == slot) & (a_col > 0.0), 1.0, 0.0).astype(BF16)
            packed = lax.dot_general(sel, h_scr[rows, :], (((0,), (0,)), ((), ())), preferred_element_type=F32)
            y = _swiglu(packed.astype(BF16), w1, w3, w2)
            y_hi = y.astype(BF16)
            y_lo = (y - y_hi.astype(F32)).astype(BF16)
            spread = (jnp.dot(sel, y_hi, preferred_element_type=F32)
                      + jnp.dot(sel, y_lo, preferred_element_type=F32))
            o_ref[rows, :] += c_col * spread
            return carry

        lax.fori_loop(0, (count + cap - 1) // cap, one_pass, 0)

    @pl.when(e == n_e - 1)
    def _():
        o_ref[...] = x_ref[...] + gt_ref[0] * o_ref[...]


def _moe(x2d, g, mod3, weights, *, tm, tiles_per_group, sub):
    rows = x2d.shape[0]
    m = mod3.shape[1]
    tpg = tiles_per_group
    cap = _moe_cap(sub)
    router, w1, w3, w2 = weights
    router = jnp.zeros((D_MODEL, ROUTER_PAD), F32).at[:, :N_EXPERTS].set(router)
    n_e, _, d_ff = w1.shape
    modspec = lambda k: pl.BlockSpec((1, m, D_MODEL), lambda i, e: (i // tpg, 0, k))
    return pl.pallas_call(
        functools.partial(_moe_kernel, n_e=n_e, cap=cap, sub=sub),
        grid=(rows // tm, n_e),
        in_specs=[pl.BlockSpec((tm, D_MODEL), lambda i, e: (i, 0)),
                  pl.BlockSpec((1, D_MODEL), lambda i, e: (0, 0)),
                  modspec(4), modspec(3), modspec(5),
                  pl.BlockSpec((D_MODEL, ROUTER_PAD), lambda i, e: (0, 0)),
                  pl.BlockSpec((1, D_MODEL, d_ff), lambda i, e: (e, 0, 0)),
                  pl.BlockSpec((1, D_MODEL, d_ff), lambda i, e: (e, 0, 0)),
                  pl.BlockSpec((1, d_ff, D_MODEL), lambda i, e: (e, 0, 0))],
        out_specs=pl.BlockSpec((tm, D_MODEL), lambda i, e: (i, 0)),
        out_shape=jax.ShapeDtypeStruct((rows, D_MODEL), F32),
        scratch_shapes=[pltpu.VMEM((tm, D_MODEL), BF16),
                        pltpu.VMEM((tm, ROUTER_PAD), F32), pltpu.VMEM((tm, ROUTER_PAD), F32),
                        pltpu.VMEM((tm, ROUTER_PAD), F32)],
        compiler_params=_params(("arbitrary", "arbitrary")),
        name="moe_ffn",
    )(x2d, g.reshape(1, D_MODEL), mod3, mod3, mod3, router, w1, w3, w2)


N_PRE = 9


def _dec_pre_kernel(r_ref, k_ref, v_ref, l_ref, q_ref, kr_ref, shift_ref, mu_ref, w0_ref, w2_ref, a0_ref,
                    a2_ref, g2_ref, kk_ref, ka_ref, cos_ref, sin_ref, o_ref):
    pieces = []
    for idx, ref in enumerate((r_ref, k_ref, v_ref, l_ref)):
        lanes = slice(idx * BRANCH_W, (idx + 1) * BRANCH_W)
        x = ref[...]
        pieces.append(x + (shift_ref[:, lanes] - x) * mu_ref[:, lanes])
    ones_bd = _head_ones()
    prm = (w0_ref[...], w2_ref[...], a0_ref[...], a2_ref[...], g2_ref[...], kk_ref[...], ka_ref[...])
    r, k_mod, v, lw, aa, bb, g = _rwkv_features(*pieces, prm, ones_bd)
    q_r = _rope(q_ref[...], cos_ref[...], sin_ref[...])
    k_r = _rope(kr_ref[...], cos_ref[...], sin_ref[...]) * (HEAD_DIM ** -0.5)
    for idx, val in enumerate((r, k_mod, v, jnp.exp(lw), aa, bb, g, q_r, k_r)):
        o_ref[:, idx * BRANCH_W:(idx + 1) * BRANCH_W] = val


def _dec_pre(proj, shift0, rp, pos0):
    rows = proj.shape[0]
    cos_t, sin_t = _rope_tables(1, pos0)
    col = lambda k: pl.BlockSpec((rows, BRANCH_W), lambda i: (0, k))
    const = lambda shape: pl.BlockSpec(shape, lambda i: (0,) * len(shape))
    vec = const((1, BRANCH_W))
    mat = const((BRANCH_W, BRANCH_W))
    mu, w0, w2p, a0, a2p, g2p, k_k, k_a = rp[:8]
    return pl.pallas_call(
        _dec_pre_kernel,
        grid=(1,),
        in_specs=[col(COL_RWKV), col(COL_RWKV + 1), col(COL_RWKV + 2), col(COL_RWKV + 3),
                  col(COL_RET), col(COL_RET + 1), const((rows, RWKV_IN_W)), const((1, RWKV_IN_W)),
                  vec, mat, vec, mat, mat, vec, vec, vec, vec],
        out_specs=const((rows, N_PRE * BRANCH_W)),
        out_shape=jax.ShapeDtypeStruct((rows, N_PRE * BRANCH_W), F32),
        compiler_params=_params(("arbitrary",)),
        name="dec_pre",
    )(proj, proj, proj, proj, proj, proj, shift0, mu, w0, w2p, a0, a2p, g2p, k_k, k_a, cos_t, sin_t)


def _dec_state_kernel(sw_ref, aa_ref, w_ref, bb_ref, km_ref, r_ref, vv_ref,
                      sr_ref, q_ref, kc_ref, vr_ref, gm_ref,
                      sw_out, y_out, sr_out, o_out):
    s = sw_ref[...]
    sa = jnp.sum(s * aa_ref[...], axis=-1, keepdims=True)
    s = s * w_ref[...] + sa * bb_ref[...] + vv_ref[...] * km_ref[...]
    sw_out[...] = s
    y_out[...] = jnp.sum(s * r_ref[...], axis=-1, keepdims=True)
    t = sr_ref[...]
    q = q_ref[...]
    kc = kc_ref[...]
    vr = vr_ref[...]
    gm = gm_ref[...]
    inter = jnp.sum(q * t, axis=1, keepdims=True) * gm
    att = jnp.sum(q * kc, axis=1, keepdims=True)
    o_out[...] = att * vr + inter
    sr_out[...] = t * gm + kc * vr


def _dec_state(s_rwkv, s_ret, pre, v_ret):
    bsz = s_rwkv.shape[0]
    nbh = bsz * N_HEADS
    tb = 64
    piece = lambda k: pre[:, k * BRANCH_W:(k + 1) * BRANCH_W]
    as_row = lambda t: t.reshape(nbh, 1, HEAD_DIM)
    as_col = lambda t: t.reshape(nbh, HEAD_DIM, 1)
    log_gamma = jnp.log(1.0 - jnp.exp2(-5.0 - jnp.arange(N_HEADS, dtype=F32)))
    gamma = jnp.broadcast_to(jnp.exp(log_gamma)[None, :, None, None], (bsz, N_HEADS, 1, HEAD_DIM))
    mat = pl.BlockSpec((tb, HEAD_DIM, HEAD_DIM), lambda i: (i, 0, 0))
    row = pl.BlockSpec((tb, 1, HEAD_DIM), lambda i: (i, 0, 0))
    colv = pl.BlockSpec((tb, HEAD_DIM, 1), lambda i: (i, 0, 0))
    sw, y, sr, o = pl.pallas_call(
        _dec_state_kernel,
        grid=(nbh // tb,),
        in_specs=[mat, row, row, row, row, row, colv, mat, colv, colv, row, row],
        out_specs=[mat, colv, mat, row],
        out_shape=[jax.ShapeDtypeStruct((nbh, HEAD_DIM, HEAD_DIM), F32),
                   jax.ShapeDtypeStruct((nbh, HEAD_DIM, 1), F32),
                   jax.ShapeDtypeStruct((nbh, HEAD_DIM, HEAD_DIM), F32),
                   jax.ShapeDtypeStruct((nbh, 1, HEAD_DIM), F32)],
        compiler_params=_params(("arbitrary",)),
        name="dec_state",
    )(s_rwkv.reshape(nbh, HEAD_DIM, HEAD_DIM), as_row(piece(4)), as_row(piece(3)), as_row(piece(5)),
      as_row(piece(1)), as_row(piece(0)), as_col(piece(2)),
      s_ret.reshape(nbh, HEAD_DIM, HEAD_DIM), as_col(piece(7)), as_col(piece(8)), as_row(v_ret),
      gamma.reshape(nbh, 1, HEAD_DIM))
    shape4 = (bsz, N_HEADS, HEAD_DIM, HEAD_DIM)
    return sw.reshape(shape4), y.reshape(bsz, BRANCH_W), sr.reshape(shape4), o.reshape(bsz, BRANCH_W)


def _dec_post_kernel(y_ref, r_ref, km_ref, v_ref, g_ref, o_ref, gr_ref, rk_ref, lng_ref, lnb_ref,
                     yw_out, yr_out):
    ones_bd = _head_ones()
    yw_out[...] = _rwkv_post(y_ref[...], r_ref[...], km_ref[...], v_ref[...], g_ref[...],
                             rk_ref[...], lng_ref[...], lnb_ref[...], ones_bd)
    o = o_ref[...]
    o = o * lax.rsqrt(_head_sum(o * o, ones_bd) * (1.0 / HEAD_DIM) + RMS_EPS)
    g = gr_ref[...]
    yr_out[...] = o * (g * jax.nn.sigmoid(g))


def _dec_post(y_rwkv, pre, o_ret, proj, rp):
    rows = proj.shape[0]
    blk = pl.BlockSpec((rows, BRANCH_W), lambda i: (0, 0))
    col = lambda k: pl.BlockSpec((rows, BRANCH_W), lambda i: (0, k))
    vec = pl.BlockSpec((1, BRANCH_W), lambda i: (0, 0))
    r_k, ln_g, ln_b = rp[8:]
    return pl.pallas_call(
        _dec_post_kernel,
        grid=(1,),
        in_specs=[blk, col(0), col(1), col(2), col(6), blk, col(COL_RET + 3), vec, vec, vec],
        out_specs=[blk, blk],
        out_shape=[jax.ShapeDtypeStruct((rows, BRANCH_W), F32)] * 2,
        compiler_params=_params(("arbitrary",)),
        name="dec_post",
    )(y_rwkv, pre, pre, pre, pre, o_ret, proj, r_k, ln_g, ln_b)


def _moba_dec_kernel(pt_ref, q_ref, kn_ref, vn_ref, *refs, n_pages):
    del pt_ref
    k_refs = refs[:n_pages]
    v_refs = refs[n_pages:2 * n_pages]
    o_ref, sc_scr = refs[2 * n_pages:]
    page = k_refs[0].shape[-1]
    per = MOBA_BLOCK // page
    nblk = n_pages // per
    half = N_HEADS * nblk
    scale = HEAD_DIM ** -0.5
    for h in range(N_HEADS):
        q_c = q_ref[0, h]
        for pg in range(n_pages):
            row = (pg % per) * half + h * nblk + pg // per
            sc_scr[row:row + 1, :] = jnp.sum(k_refs[pg][0, 0, h] * q_c, axis=0, keepdims=True)
    raw = sc_scr[...]
    rs = jnp.sum(raw, axis=1, keepdims=True)
    gate = (rs[:half] + rs[half:]) * (1.0 / MOBA_BLOCK)
    ri = lax.broadcasted_iota(jnp.int32, (half, half), 0)
    ci = lax.broadcasted_iota(jnp.int32, (half, half), 1)
    g_self = jnp.broadcast_to(gate, (half, half))
    g_other = _dot_hi(jnp.ones((half, half), F32), jnp.where(ri == ci, g_self, 0.0))
    beats = jnp.where(g_other > g_self, 1.0, jnp.where((g_other == g_self) & (ci < ri), 1.0, 0.0))
    beats = jnp.where(ri // nblk == ci // nblk, beats, 0.0)
    sel = jnp.sum(beats, axis=1, keepdims=True) < MOBA_TOPK
    sel2 = jnp.concatenate([sel.astype(F32)] * per, axis=0) > 0.0
    masked = jnp.where(sel2, raw * scale, NEG_INF)
    for h in range(N_HEADS):
        q_c = q_ref[0, h]
        s_own = jnp.sum(q_c * kn_ref[0, h], axis=0, keepdims=True) * scale
        parts = [masked[par * half + h * nblk:par * half + (h + 1) * nblk] for par in range(per)]
        m = s_own
        for part in parts:
            m = jnp.maximum(m, jnp.max(jnp.max(part, axis=1, keepdims=True), axis=0, keepdims=True))
        p_own = jnp.exp(s_own - m)
        l = p_own
        acc = jnp.zeros((HEAD_DIM, page), F32)
        for par, part in enumerate(parts):
            p = jnp.exp(part - m)
            l = l + jnp.sum(jnp.sum(p, axis=1, keepdims=True), axis=0, keepdims=True)
            for n in range(nblk):
                acc = acc + p[n:n + 1, :] * v_refs[n * per + par][0, 0, h]
        o = jnp.sum(acc, axis=1, keepdims=True) + p_own * vn_ref[0, h]
        o_ref[0, h] = o / l


def _moba_dec(q, k_new, v_new, k_t, v_t, page_table, layer):
    bsz, n_pages = page_table.shape
    page = k_t.shape[-1]
    cols = lambda t: t.reshape(bsz, N_HEADS, HEAD_DIM, 1)
    qspec = pl.BlockSpec((1, N_HEADS, HEAD_DIM, 1), lambda b, pt: (b, 0, 0, 0))
    pspec = lambda pg: pl.BlockSpec((1, 1, N_HEADS, HEAD_DIM, page), lambda b, pt: (layer, pt[b, pg], 0, 0, 0))
    out = pl.pallas_call(
        functools.partial(_moba_dec_kernel, n_pages=n_pages),
        grid_spec=pltpu.PrefetchScalarGridSpec(
            num_scalar_prefetch=1,
            grid=(bsz,),
            in_specs=[qspec, qspec, qspec] + [pspec(pg) for pg in range(n_pages)] * 2,
            out_specs=qspec,
            scratch_shapes=[pltpu.VMEM((N_HEADS * n_pages, page), F32)]),
        out_shape=jax.ShapeDtypeStruct((bsz, N_HEADS, HEAD_DIM, 1), F32),
        compiler_params=_params(("arbitrary",)),
        name="moba_decode",
    )(page_table, cols(q), cols(k_new), cols(v_new), *([k_t] * n_pages), *([v_t] * n_pages))
    return out.reshape(bsz, BRANCH_W)


def _final_norm_kernel(x_ref, g_ref, o_ref):
    x = x_ref[...]
    o_ref[...] = x * lax.rsqrt(jnp.mean(x * x, axis=-1, keepdims=True) + RMS_EPS) * g_ref[...]


def _final_norm(x2d, g, *, tm):
    rows = x2d.shape[0]
    return pl.pallas_call(
        _final_norm_kernel,
        grid=(rows // tm,),
        in_specs=[pl.BlockSpec((tm, D_MODEL), lambda i: (i, 0)), pl.BlockSpec((1, D_MODEL), lambda i: (0, 0))],
        out_specs=pl.BlockSpec((tm, D_MODEL), lambda i: (i, 0)),
        out_shape=jax.ShapeDtypeStruct((rows, D_MODEL), F32),
        compiler_params=_params(("arbitrary",)),
        name="final_norm",
    )(x2d, g.reshape(1, D_MODEL))


RWKV_BATCHES_PER_STEP = 8
RET_CHUNK = 256
S5_TIME_CHUNK = 128


MOE_TILE = 1024
MOE_SUB = 512
BF16_ROWS = 16


def _moe_cap(tm):
    return -(-(tm * 5 // 16) // BF16_ROWS) * BF16_ROWS


def _ffn_any(x2d, g, mod3, ffn, *, rows_per_group):
    if len(ffn) == 3:
        tm = min(1024, rows_per_group)
        return _ffn(x2d, g, mod3, ffn, tm=tm, tiles_per_group=rows_per_group // tm, tf=256)
    tm = min(MOE_TILE, rows_per_group)
    return _moe(x2d, g, mod3, ffn, tm=tm, tiles_per_group=rows_per_group // tm, sub=min(MOE_SUB, tm))


def _prompt_layer(x2d, bsz, t_len, mod_l, lp):
    mod3 = mod_l.reshape(bsz, 1, -1)
    tm = min(2048, t_len)
    tpg = t_len // tm
    proj, gates, u_tb = _inproj(x2d, lp['norm_mix'], mod3, lp['w_in'], tm=tm, tiles_per_group=tpg,
                                tb_shape=(t_len, bsz * BRANCH_W))
    proj3 = proj.reshape(bsz, t_len, MIX_W)
    z_state = jnp.zeros((bsz, S5_W), F32)
    y_s5, s5_re, s5_im = _s5(u_tb.reshape(t_len * bsz, BRANCH_W), z_state, z_state, lp['s5p'], lp['s5_d'],
                             lp['s5_w_glu'], nb=bsz, t_len=t_len, tc=min(S5_TIME_CHUNK, t_len))
    z_bd = jnp.zeros((bsz, BRANCH_W, BRANCH_W), F32)
    y_rwkv, s_rwkv, shift_n = _rwkv(proj3, jnp.zeros((bsz, RWKV_IN_W), F32), z_bd, lp['rwkv'],
                                    nb=RWKV_BATCHES_PER_STEP)
    y_ret, s_ret = _ret(proj3, z_bd, 0, chunk=min(RET_CHUNK, t_len))
    y_moba = _moba_prompt(proj3)
    kv = lambda k: proj3[:, :, (COL_MOBA + k) * BRANCH_W:(COL_MOBA + k + 1) * BRANCH_W].reshape(
        bsz, t_len, N_HEADS, HEAD_DIM)
    tmm = min(256, t_len)
    tpm = t_len // tmm
    s5_spec = pl.BlockSpec((tmm, BRANCH_W), lambda i: (i % tpm, i // tpm))
    flat = lambda y: y.reshape(bsz * t_len, BRANCH_W)
    x2d = _merge(x2d, y_s5.reshape(t_len, bsz * BRANCH_W), s5_spec, flat(y_rwkv), flat(y_ret), flat(y_moba),
                 gates, mod3, lp['w_branch'], lp['w_out'], tm=tmm, tiles_per_group=tpm)
    x2d = _ffn_any(x2d, lp['norm_ffn'], mod3, lp['ffn'], rows_per_group=t_len)
    g16 = (bsz, S5_GROUPS, S5_STATE)
    return x2d, (s5_re.reshape(g16), s5_im.reshape(g16), s_rwkv, shift_n, s_ret, kv(1), kv(2))


def _decode_layer(x2d, mod_l, lp, layer, pos0, s5_re0, s5_im0, rwkv_s0, shift0, ret_s0, cache_k, cache_v,
                  page_table):
    bsz = x2d.shape[0]
    mod3 = mod_l.reshape(1, bsz, -1)
    proj, gates = _inproj(x2d, lp['norm_mix'], mod3, lp['w_in'], tm=bsz, tiles_per_group=1)
    piece = lambda k: proj[:, k * BRANCH_W:(k + 1) * BRANCH_W]
    y_s5, s5_re, s5_im = _s5(piece(COL_S5), s5_re0.reshape(bsz, S5_W), s5_im0.reshape(bsz, S5_W), lp['s5p'],
                             lp['s5_d'], lp['s5_w_glu'], nb=bsz, t_len=1, tc=1)
    pre = _dec_pre(proj, shift0, lp['rwkv'], pos0)
    s_rwkv, y_raw, s_ret, o_raw = _dec_state(rwkv_s0, ret_s0, pre, piece(COL_RET + 2))
    y_rwkv, y_ret = _dec_post(y_raw, pre, o_raw, proj, lp['rwkv'])
    k_new = piece(COL_MOBA + 1)
    v_new = piece(COL_MOBA + 2)
    y_moba = _moba_dec(piece(COL_MOBA), k_new, v_new, cache_k, cache_v, page_table, layer)
    s5_spec = pl.BlockSpec((bsz, BRANCH_W), lambda i: (i, 0))
    x2d = _merge(x2d, y_s5, s5_spec, y_rwkv, y_ret, y_moba, gates, mod3, lp['w_branch'], lp['w_out'],
                 tm=bsz, tiles_per_group=1)
    x2d = _ffn_any(x2d, lp['norm_ffn'], mod3, lp['ffn'], rows_per_group=bsz)
    g16 = (bsz, S5_GROUPS, S5_STATE)
    kv4 = lambda t: t.reshape(bsz, 1, N_HEADS, HEAD_DIM)
    shift_n = proj[:, COL_RWKV * BRANCH_W:COL_RWKV * BRANCH_W + RWKV_IN_W]
    return x2d, (s5_re.reshape(g16), s5_im.reshape(g16), s_rwkv, shift_n, s_ret, kv4(k_new), kv4(v_new))


def kernel(x_prompt, x_sample, c_prompt, c_sample, state_s5_re, state_s5_im, state_rwkv, state_rwkv_shift, state_ret, cache_moba_k, cache_moba_v, page_table, norm_mix_g, norm_ffn_g, norm_final_g, w_ada, b_ada, w_in, s5_lambda_re, s5_lambda_im, s5_log_dt, s5_b_re, s5_b_im, s5_c_re, s5_c_im, s5_d, s5_w_glu, rwkv_mu, rwkv_w0, rwkv_w2, rwkv_a0, rwkv_a2, rwkv_g2, rwkv_k_k, rwkv_k_a, rwkv_r_k, rwkv_ln_g, rwkv_ln_b, w_branch, w_out, ffn_w1, ffn_w3, ffn_w2, moe_router, moe_w1, moe_w3, moe_w2):
    bp, t_len, _ = x_prompt.shape
    bs = x_sample.shape[0]
    depth = w_in.shape[0]
    past_len = page_table.shape[1] * cache_moba_k.shape[2]
    cache_kt = jnp.transpose(cache_moba_k, (0, 1, 3, 4, 2))
    cache_vt = jnp.transpose(cache_moba_v, (0, 1, 3, 4, 2))
    mod_all = _ada(jnp.concatenate([c_prompt, c_sample], axis=0), w_ada, b_ada)
    xp = x_prompt.reshape(bp * t_len, D_MODEL)
    xs = x_sample.reshape(bs, D_MODEL)
    outs_p = [[] for _ in range(7)]
    outs_s = [[] for _ in range(7)]
    for l in range(depth):
        if l % 2 == 0:
            ffn = (ffn_w1[l // 2], ffn_w3[l // 2], ffn_w2[l // 2])
        else:
            ffn = (moe_router[l // 2], moe_w1[l // 2].astype(BF16), moe_w3[l // 2].astype(BF16),
                   moe_w2[l // 2].astype(BF16))
        lp = {
            'norm_mix': norm_mix_g[l], 'norm_ffn': norm_ffn_g[l], 'w_in': w_in[l],
            's5p': _s5_params(s5_lambda_re[l], s5_lambda_im[l], s5_log_dt[l], s5_b_re[l], s5_b_im[l],
                              s5_c_re[l], s5_c_im[l]),
            's5_d': s5_d[l], 's5_w_glu': s5_w_glu[l],
            'rwkv': _rwkv_params(rwkv_mu[l], rwkv_w0[l], rwkv_w2[l], rwkv_a0[l], rwkv_a2[l], rwkv_g2[l],
                                 rwkv_k_k[l], rwkv_k_a[l], rwkv_r_k[l], rwkv_ln_g[l], rwkv_ln_b[l]),
            'w_branch': w_branch[l], 'w_out': w_out[l], 'ffn': ffn,
        }
        xp, st_p = _prompt_layer(xp, bp, t_len, mod_all[l, :bp], lp)
        xs, st_s = _decode_layer(xs, mod_all[l, bp:], lp, l, past_len, state_s5_re[l], state_s5_im[l],
                                 state_rwkv[l], state_rwkv_shift[l], state_ret[l], cache_kt, cache_vt,
                                 page_table)
        for j in range(7):
            outs_p[j].append(st_p[j])
            outs_s[j].append(st_s[j])
    y_prompt = _final_norm(xp, norm_final_g, tm=1024).reshape(bp, t_len, D_MODEL)
    y_sample = _final_norm(xs, norm_final_g, tm=bs).reshape(bs, 1, D_MODEL)
    stack = lambda outs: [jnp.stack(o, axis=0) for o in outs]
    return (y_prompt, y_sample, *stack(outs_p), *stack(outs_s))
```

```python
import functools
import math

import jax
import jax.numpy as jnp
from jax import lax
from jax.experimental import pallas as pl
from jax.experimental.pallas import tpu as pltpu

F32 = jnp.float32
BF16 = jnp.bfloat16
HIGHEST = lax.Precision.HIGHEST

D_MODEL = 1024
BRANCH_W = 256
HEAD_DIM = 64
N_HEADS = 4
N_BRANCH = 4
S5_GROUPS = 16
S5_STATE = 64
S5_CH = 16
S5_W = S5_GROUPS * S5_STATE
IN_W = 7168
RWKV_IN_W = 1024
RWKV_LN_EPS = 64e-5
RMS_EPS = 1e-6
ROPE_BASE = 10000.0
MOBA_BLOCK = 256
MOBA_TOPK = 3
N_EXPERTS = 8
ROUTER_PAD = 128
NEG_INF = float("-inf")

COL_S5 = 0
COL_RWKV = 1
COL_RET = 5
COL_MOBA = 9
COL_GATE = 12

VMEM_LIMIT = 48 * 1024 * 1024


def _params(sem):
    return pltpu.CompilerParams(dimension_semantics=sem, vmem_limit_bytes=VMEM_LIMIT)


def _dot(a, b):
    return jnp.dot(a.astype(BF16), b.astype(BF16), preferred_element_type=F32)


def _dot_hi(a, b):
    return jnp.dot(a, b, precision=HIGHEST, preferred_element_type=F32)


def _dot_nt(a, b):
    return lax.dot_general(a.astype(BF16), b.astype(BF16), (((1,), (1,)), ((), ())),
                           preferred_element_type=F32)


def _dot_nt_hi(a, b):
    return lax.dot_general(a, b, (((1,), (1,)), ((), ())), precision=HIGHEST,
                           preferred_element_type=F32)


def _dot_tn(a, b):
    return lax.dot_general(a.astype(BF16), b.astype(BF16), (((0,), (0,)), ((), ())),
                           preferred_element_type=F32)


def _head_masks(width=BRANCH_W):
    lane = lax.broadcasted_iota(jnp.int32, (1, width), 1)
    return [(lane // HEAD_DIM == h).astype(F32) for h in range(N_HEADS)]


def _head_ones():
    r = lax.broadcasted_iota(jnp.int32, (BRANCH_W, BRANCH_W), 0) // HEAD_DIM
    c = lax.broadcasted_iota(jnp.int32, (BRANCH_W, BRANCH_W), 1) // HEAD_DIM
    return (r == c).astype(F32)


def _split3(x):
    hi = x.astype(BF16)
    rest = x - hi.astype(F32)
    mid = rest.astype(BF16)
    return hi, mid, (rest - mid.astype(F32)).astype(BF16)


def _head_sum(x, ones_bd):
    w = ones_bd.astype(BF16)
    return sum(jnp.dot(p, w, preferred_element_type=F32) for p in _split3(x))


def _ada_kernel(c_ref, w_ref, b_ref, o_ref):
    c = c_ref[...]
    h = c * jax.nn.sigmoid(c)
    o_ref[0] = _dot(h, w_ref[0]) + b_ref[0]


def _ada(c_all, w_ada, b_ada):
    depth, _, width = w_ada.shape
    rows = c_all.shape[0]
    tn = 1024
    return pl.pallas_call(
        _ada_kernel,
        grid=(depth, width // tn),
        in_specs=[pl.BlockSpec((rows, D_MODEL), lambda l, j: (0, 0)),
                  pl.BlockSpec((1, D_MODEL, tn), lambda l, j: (l, 0, j)),
                  pl.BlockSpec((1, 1, tn), lambda l, j: (l, 0, j))],
        out_specs=pl.BlockSpec((1, rows, tn), lambda l, j: (l, 0, j)),
        out_shape=jax.ShapeDtypeStruct((depth, rows, width), F32),
        compiler_params=_params(("arbitrary", "arbitrary")),
        name="ada_mod",
    )(c_all, w_ada, b_ada.reshape(depth, 1, width))


def _modulated_norm(x, g, sc, sh):
    y = x * lax.rsqrt(jnp.mean(x * x, axis=-1, keepdims=True) + RMS_EPS) * g
    return y * (1.0 + sc) + sh


INPROJ_TN = 512
MIX_W = COL_GATE * BRANCH_W
GATE_W = IN_W - MIX_W


def _inproj_kernel(x_ref, g_ref, sc_ref, sh_ref, w_ref, o_ref, gate_ref, *rest, emit_tb):
    h_scr = rest[-1]
    j = pl.program_id(1)
    n_mix = MIX_W // INPROJ_TN

    @pl.when(j == 0)
    def _():
        h_scr[...] = _modulated_norm(x_ref[...], g_ref[...], sc_ref[0], sh_ref[0]).astype(BF16)

    acc = jnp.dot(h_scr[...], w_ref[0].astype(BF16), preferred_element_type=F32)

    @pl.when(j < n_mix)
    def _():
        o_ref[...] = acc

    @pl.when(j >= n_mix)
    def _():
        gate_ref[...] = acc.astype(BF16)

    if emit_tb:
        u_ref = rest[0]

        @pl.when(j == 0)
        def _():
            u_ref[...] = acc[:, :BRANCH_W]


def _inproj(x2d, g, mod3, w, layer, *, tm, tiles_per_group, tb_shape=None):
    rows = x2d.shape[0]
    m = mod3.shape[1]
    tn = INPROJ_TN
    n_mix = MIX_W // tn
    tpg = tiles_per_group
    in_specs = [pl.BlockSpec((tm, D_MODEL), lambda i, j: (i, 0)),
                pl.BlockSpec((1, D_MODEL), lambda i, j: (0, 0)),
                pl.BlockSpec((1, m, D_MODEL), lambda i, j: (i // tpg, 0, 1)),
                pl.BlockSpec((1, m, D_MODEL), lambda i, j: (i // tpg, 0, 0)),
                pl.BlockSpec((1, D_MODEL, tn), lambda i, j: (layer, 0, j))]
    out_specs = [pl.BlockSpec((tm, tn), lambda i, j: (i, jnp.minimum(j, n_mix - 1))),
                 pl.BlockSpec((tm, tn), lambda i, j: (i, jnp.maximum(j - n_mix, 0)))]
    out_shape = [jax.ShapeDtypeStruct((rows, MIX_W), F32), jax.ShapeDtypeStruct((rows, GATE_W), BF16)]
    if tb_shape is not None:
        out_specs.append(pl.BlockSpec((tm, BRANCH_W), lambda i, j: (i % tpg, i // tpg)))
        out_shape.append(jax.ShapeDtypeStruct(tb_shape, F32))
    return pl.pallas_call(
        functools.partial(_inproj_kernel, emit_tb=tb_shape is not None),
        grid=(rows // tm, IN_W // tn),
        in_specs=in_specs, out_specs=out_specs, out_shape=out_shape,
        scratch_shapes=[pltpu.VMEM((tm, D_MODEL), BF16)],
        compiler_params=_params(("arbitrary", "arbitrary")),
        name="inproj",
    )(x2d, g.reshape(1, D_MODEL), mod3, mod3, w)


def _s5_kernel(u_ref, x0r_ref, x0i_ref, lbr_ref, lbi_ref, br_ref, bi_ref, cr_ref, ci_ref, d_ref, wg_ref,
               y_ref, sr_ref, si_ref, bur, bui, xr, xi, *, nb, tc):
    c = pl.program_id(0)

    @pl.when(c == 0)
    def _():
        xr[...] = x0r_ref[...]
        xi[...] = x0i_ref[...]

    u = u_ref[...]
    ub = u.astype(BF16)
    bur[...] = jnp.dot(ub, br_ref[...].astype(BF16), preferred_element_type=F32)
    bui[...] = jnp.dot(ub, bi_ref[...].astype(BF16), preferred_element_type=F32)
    lbr = jnp.broadcast_to(lbr_ref[...], (nb, S5_W))
    lbi = jnp.broadcast_to(lbi_ref[...], (nb, S5_W))

    def body(t, carry):
        sr, si = carry
        rows = pl.ds(pl.multiple_of(t * nb, nb), nb)
        nr = lbr * sr - lbi * si + bur[rows, :]
        ni = lbr * si + lbi * sr + bui[rows, :]
        bur[rows, :] = nr
        bui[rows, :] = ni
        return nr, ni

    sr, si = lax.fori_loop(0, tc, body, (xr[...], xi[...]))
    xr[...] = sr
    xi[...] = si
    sr_ref[...] = sr
    si_ref[...] = si
    y = _dot(bur[...], cr_ref[...]) - _dot(bui[...], ci_ref[...]) + d_ref[...] * u
    z = jax.nn.gelu(y)
    y_ref[...] = z * jax.nn.sigmoid(_dot(z, wg_ref[...]))


def _s5_params(lam_re, lam_im, log_dt, b_re, b_im, c_re, c_im):
    dt = jnp.exp(log_dt)[:, None]
    mag = jnp.exp(lam_re * dt)
    lb_re = mag * jnp.cos(lam_im * dt)
    lb_im = mag * jnp.sin(lam_im * dt)
    den = lam_re * lam_re + lam_im * lam_im
    q_re = ((lb_re - 1.0) * lam_re + lb_im * lam_im) / den
    q_im = (lb_im * lam_re - (lb_re - 1.0) * lam_im) / den
    bb_re = q_re[..., None] * b_re - q_im[..., None] * b_im
    bb_im = q_re[..., None] * b_im + q_im[..., None] * b_re
    eye = jnp.eye(S5_GROUPS, dtype=F32)
    to_in = lambda bb: jnp.einsum('gnc,gh->gchn', bb, eye).reshape(BRANCH_W, S5_W)
    to_out = lambda cc: jnp.einsum('gcn,gh->gnhc', cc, eye).reshape(S5_W, BRANCH_W)
    return (lb_re.reshape(1, S5_W), lb_im.reshape(1, S5_W), to_in(bb_re), to_in(bb_im),
            to_out(c_re), to_out(c_im))


def _s5(u_tb, x0_re, x0_im, s5p, d_skip, w_glu, *, nb, t_len, tc):
    lb_re, lb_im, bin_re, bin_im, cout_re, cout_im = s5p
    rows = tc * nb
    const = lambda shape: pl.BlockSpec(shape, lambda c: (0,) * len(shape))
    return pl.pallas_call(
        functools.partial(_s5_kernel, nb=nb, tc=tc),
        grid=(t_len // tc,),
        in_specs=[pl.BlockSpec((rows, BRANCH_W), lambda c: (c, 0)),
                  const((nb, S5_W)), const((nb, S5_W)), const((1, S5_W)), const((1, S5_W)),
                  const((BRANCH_W, S5_W)), const((BRANCH_W, S5_W)),
                  const((S5_W, BRANCH_W)), const((S5_W, BRANCH_W)),
                  const((1, BRANCH_W)), const((BRANCH_W, BRANCH_W))],
        out_specs=[pl.BlockSpec((rows, BRANCH_W), lambda c: (c, 0)), const((nb, S5_W)), const((nb, S5_W))],
        out_shape=[jax.ShapeDtypeStruct((t_len * nb, BRANCH_W), F32),
                   jax.ShapeDtypeStruct((nb, S5_W), F32), jax.ShapeDtypeStruct((nb, S5_W), F32)],
        scratch_shapes=[pltpu.VMEM((rows, S5_W), F32), pltpu.VMEM((rows, S5_W), F32),
                        pltpu.VMEM((nb, S5_W), F32), pltpu.VMEM((nb, S5_W), F32)],
        compiler_params=_params(("arbitrary",)),
        name="s5",
    )(u_tb, x0_re, x0_im, lb_re, lb_im, bin_re, bin_im, cout_re, cout_im,
      d_skip.reshape(1, BRANCH_W), w_glu)


def _rwkv_features(pm_r, pm_k, pm_v, pm_l, prm, ones_bd):
    w0, w2p, a0, a2p, g2p, k_k, k_a = prm
    w_raw = w0 + _dot(jnp.tanh(pm_l), w2p)
    lw = -jax.nn.sigmoid(w_raw) * math.exp(-0.5)
    a_sig = jax.nn.sigmoid(a0 + _dot(pm_l, a2p))
    g = _dot(jax.nn.sigmoid(pm_l), g2p)
    kk = pm_k * k_k
    kk = kk * lax.rsqrt(_head_sum(kk * kk, ones_bd) + 1e-12)
    k_mod = pm_k * (1.0 + (a_sig - 1.0) * k_a)
    return pm_r, k_mod, pm_v, lw, -kk, kk * a_sig, g


def _rwkv_post(y, r, k_mod, v, g, r_k, ln_g, ln_b, ones_bd):
    inv = 1.0 / HEAD_DIM
    mean = _head_sum(y, ones_bd) * inv
    yc = y - mean
    var = _head_sum(yc * yc, ones_bd) * inv
    yn = yc * lax.rsqrt(var + RWKV_LN_EPS) * ln_g + ln_b
    bonus = _head_sum(r * k_mod * r_k, ones_bd) * v
    return (yn + bonus) * g


RWKV_CHUNK = 64


def _cumsum_rows(tril, x):
    return sum(jnp.dot(tril, p, preferred_element_type=F32) for p in _split3(x))


def _per_head(x, ones_bd):
    return jnp.concatenate([x.astype(BF16)] * N_HEADS, axis=0) * ones_bd.astype(BF16)


def _rwkv_chunks(r, k_mod, v, lw, aa, bb, states, ones_bd):
    n = RWKV_CHUNK
    nb = len(states)
    seqs = range(nb)
    part = lambda x, b: x[b * n:(b + 1) * n]
    t_idx = lax.broadcasted_iota(jnp.int32, (n, BRANCH_W), 0)
    i_idx = lax.broadcasted_iota(jnp.int32, (n, BRANCH_W), 1) % n
    strict = (t_idx > i_idx).astype(F32)
    incl = (t_idx >= i_idx).astype(F32)
    eye_c = (t_idx == i_idx).astype(F32)
    tril = (lax.broadcasted_iota(jnp.int32, (n, n), 0) >= lax.broadcasted_iota(jnp.int32, (n, n), 1)).astype(BF16)
    bd = lambda x: _per_head(x, ones_bd)
    cum = jnp.concatenate([_cumsum_rows(tril, part(lw, b)) for b in seqs], axis=0)
    tot = [part(cum, b)[n - 1:n, :] for b in seqs]
    tot_rows = jnp.concatenate([jnp.broadcast_to(t, (n, BRANCH_W)) for t in tot], axis=0)
    e_neg = jnp.exp(-cum)
    e_rem = jnp.exp(tot_rows - cum)
    a_t = aa * jnp.exp(cum - lw)
    r_t = r * jnp.exp(cum)
    b_t = bb * e_neg
    k_t = k_mod * e_neg
    b_h = bb * e_rem
    k_h = k_mod * e_rem
    ar = [jnp.concatenate([part(a_t, b), part(r_t, b)], axis=0) for b in seqs]
    xb = [_dot_nt(ar[b], bd(part(b_t, b))) for b in seqs]
    xk = [_dot_nt(ar[b], bd(part(k_t, b))) for b in seqs]
    ss = [_dot_nt(ar[b], states[b]) for b in seqs]
    v_bd = [bd(part(v, b)) for b in seqs]
    pw = [xb[b][:n] * strict for b in seqs]
    mv = [_dot(xk[b][:n] * strict, v_bd[b]) for b in seqs]
    tinv = [eye_c + pw[b] for b in seqs]
    for _ in range(int(math.log2(n)) - 1):
        pw = [_dot(pw[b], bd(pw[b])) for b in seqs]
        tinv = [tinv[b] + _dot(pw[b], bd(tinv[b])) for b in seqs]
    u = [_dot(tinv[b], bd(ss[b][:n] + mv[b])) for b in seqs]
    y = [ss[b][n:] + _dot(xb[b][n:] * incl, bd(u[b])) + _dot(xk[b][n:] * incl, v_bd[b]) for b in seqs]
    upd = [_dot_tn(jnp.concatenate([u[b], part(v, b)], axis=0),
                   jnp.concatenate([part(b_h, b), part(k_h, b)], axis=0)) for b in seqs]
    new_states = [states[b] * jnp.exp(tot[b]) + ones_bd * upd[b] for b in seqs]
    return jnp.concatenate(y, axis=0), new_states


def _rwkv_kernel(r_ref, k_ref, v_ref, l_ref, shift_ref, s0_ref, mu_ref, w0_ref, w2_ref, a0_ref, a2_ref,
                 g2_ref, kk_ref, ka_ref, rk_ref, lng_ref, lnb_ref,
                 y_ref, s_ref, sh_ref, prev_scr, s_scr, *, nb):
    c = pl.program_id(1)
    n = RWKV_CHUNK

    @pl.when(c == 0)
    def _():
        prev_scr[...] = shift_ref[...]
        s_scr[...] = s0_ref[...]

    row = lax.broadcasted_iota(jnp.int32, (n, BRANCH_W), 0)
    ones_bd = _head_ones()
    prm = (w0_ref[...], w2_ref[...], a0_ref[...], a2_ref[...], g2_ref[...], kk_ref[...], ka_ref[...])
    pieces = []
    for idx, ref in enumerate((r_ref, k_ref, v_ref, l_ref)):
        lanes = slice(idx * BRANCH_W, (idx + 1) * BRANCH_W)
        shifted = []
        for b in range(nb):
            x = ref[b]
            x_prev = jnp.where(row == 0, prev_scr[b, :, lanes], pltpu.roll(x, 1, 0))
            shifted.append(x + (x_prev - x) * mu_ref[:, lanes])
            prev_scr[b, :, lanes] = x[n - 1:n, :]
            sh_ref[b, :, lanes] = x[n - 1:n, :]
        pieces.append(jnp.concatenate(shifted, axis=0))
    r, k_mod, v, lw, aa, bb, g = _rwkv_features(*pieces, prm, ones_bd)
    y, new_states = _rwkv_chunks(r, k_mod, v, lw, aa, bb, [s_scr[b] for b in range(nb)], ones_bd)
    out = _rwkv_post(y, r, k_mod, v, g, rk_ref[...], lng_ref[...], lnb_ref[...], ones_bd)
    for b in range(nb):
        s_scr[b] = new_states[b]
        s_ref[b] = new_states[b]
        y_ref[b] = out[b * n:(b + 1) * n]


def _rwkv_params(mu, w0, w2, a0, a2, g2, k_k, k_a, r_k, ln_g, ln_b):
    row = lambda t: t.reshape(1, -1)
    pad = lambda w, lo: jnp.zeros((BRANCH_W, BRANCH_W), F32).at[lo:lo + w.shape[0]].set(w)
    return (row(mu), row(w0), pad(w2, 0), row(a0), pad(a2, 64), pad(g2, 128), row(k_k), row(k_a),
            row(r_k), row(ln_g), row(ln_b))


def _diag_blocks(s_bd):
    bsz = s_bd.shape[0]
    s5 = s_bd.reshape(bsz, N_HEADS, HEAD_DIM, N_HEADS, HEAD_DIM)
    return jnp.stack([s5[:, h, :, h, :] for h in range(N_HEADS)], axis=1)


def _rwkv(proj3, shift0, s0, rp, *, nb):
    bsz, t_len, _ = proj3.shape
    chunk = RWKV_CHUNK
    col = lambda k: pl.BlockSpec((nb, chunk, BRANCH_W), lambda b, c: (b, c, COL_RWKV + k))
    const = lambda shape: pl.BlockSpec(shape, lambda b, c: (0,) * len(shape))
    vec = const((1, BRANCH_W))
    mat = const((BRANCH_W, BRANCH_W))
    y, s_bd, shift_n = pl.pallas_call(
        functools.partial(_rwkv_kernel, nb=nb),
        grid=(bsz // nb, t_len // chunk),
        in_specs=[col(0), col(1), col(2), col(3),
                  pl.BlockSpec((nb, 1, RWKV_IN_W), lambda b, c: (b, 0, 0)),
                  pl.BlockSpec((nb, BRANCH_W, BRANCH_W), lambda b, c: (b, 0, 0)),
                  const((1, RWKV_IN_W)), vec, mat, vec, mat, mat, vec, vec, vec, vec, vec],
        out_specs=[pl.BlockSpec((nb, chunk, BRANCH_W), lambda b, c: (b, c, 0)),
                   pl.BlockSpec((nb, BRANCH_W, BRANCH_W), lambda b, c: (b, 0, 0)),
                   pl.BlockSpec((nb, 1, RWKV_IN_W), lambda b, c: (b, 0, 0))],
        out_shape=[jax.ShapeDtypeStruct((bsz, t_len, BRANCH_W), F32),
                   jax.ShapeDtypeStruct((bsz, BRANCH_W, BRANCH_W), F32),
                   jax.ShapeDtypeStruct((bsz, 1, RWKV_IN_W), F32)],
        scratch_shapes=[pltpu.VMEM((nb, 1, RWKV_IN_W), F32), pltpu.VMEM((nb, BRANCH_W, BRANCH_W), F32)],
        compiler_params=_params(("arbitrary", "arbitrary")),
        name="rwkv",
    )(proj3, proj3, proj3, proj3, shift0.reshape(bsz, 1, RWKV_IN_W), s0, *rp)
    return y, _diag_blocks(s_bd), shift_n.reshape(bsz, RWKV_IN_W)


def _rope_tables(t_len, pos0):
    half = HEAD_DIM // 2
    freqs = 1.0 / (ROPE_BASE ** jnp.linspace(0.0, 1.0, half, dtype=F32))
    pos = jnp.arange(t_len, dtype=F32) + pos0
    ang = pos[:, None] * freqs[None, :]
    cos = jnp.cos(ang)
    sin = jnp.sin(ang)
    cos_t = jnp.tile(jnp.concatenate([cos, cos], axis=-1), (1, N_HEADS))
    sin_t = jnp.tile(jnp.concatenate([-sin, sin], axis=-1), (1, N_HEADS))
    return cos_t, sin_t


def _rope(x, cos_t, sin_t):
    lane = lax.broadcasted_iota(jnp.int32, x.shape, 1)
    first = (lane % HEAD_DIM) < (HEAD_DIM // 2)
    swapped = jnp.where(first, pltpu.roll(x, BRANCH_W - HEAD_DIM // 2, 1), pltpu.roll(x, HEAD_DIM // 2, 1))
    return x * cos_t + swapped * sin_t


def _ret_tables(chunk):
    log_gamma = jnp.log(1.0 - jnp.exp2(-5.0 - jnp.arange(N_HEADS, dtype=F32)))
    i = jnp.arange(chunk, dtype=F32)
    diff = i[:, None] - i[None, :]
    dmask = jnp.where(diff >= 0, jnp.exp(log_gamma[:, None, None] * jnp.maximum(diff, 0.0)), 0.0)
    lanes = lambda t: jnp.repeat(t, HEAD_DIM, axis=-1)
    xi = lanes(jnp.exp(log_gamma[None, :] * (i[:, None] + 1.0)))
    zeta = lanes(jnp.exp(log_gamma[None, :] * (chunk - 1.0 - i[:, None])))
    g_chunk = lanes(jnp.exp(log_gamma * chunk)[None, :])
    return dmask, xi, zeta, g_chunk


def _ret_kernel(q_ref, k_ref, v_ref, g_ref, s0_ref, cos_ref, sin_ref, dm_ref, xi_ref, zeta_ref, gch_ref,
                y_ref, s_ref, s_scr):
    c = pl.program_id(1)

    @pl.when(c == 0)
    def _():
        s_scr[...] = s0_ref[0]

    cos_t = cos_ref[...]
    sin_t = sin_ref[...]
    q = _rope(q_ref[0], cos_t, sin_t)
    k = _rope(k_ref[0], cos_t, sin_t) * (HEAD_DIM ** -0.5)
    v = v_ref[0]
    s = s_scr[...]
    ones_bd = _head_ones()
    o = _dot(q, s) * xi_ref[...]
    for h, mh in enumerate(_head_masks()):
        att = _dot_nt(q * mh, k) * dm_ref[h]
        o = o + _dot(att, v) * mh
    s_new = s * gch_ref[...] + ones_bd * _dot_tn(k * zeta_ref[...], v)
    s_scr[...] = s_new
    s_ref[0] = s_new
    o = o * lax.rsqrt(_head_sum(o * o, ones_bd) * (1.0 / HEAD_DIM) + RMS_EPS)
    g = g_ref[0]
    y_ref[0] = o * (g * jax.nn.sigmoid(g))


def _ret(proj3, s0, pos0, *, chunk):
    bsz, t_len, _ = proj3.shape
    cos_t, sin_t = _rope_tables(t_len, pos0)
    dmask, xi, zeta, g_chunk = _ret_tables(chunk)
    col = lambda k: pl.BlockSpec((1, chunk, BRANCH_W), lambda b, c: (b, c, COL_RET + k))
    const = lambda shape: pl.BlockSpec(shape, lambda b, c: (0,) * len(shape))
    tab = pl.BlockSpec((chunk, BRANCH_W), lambda b, c: (c, 0))
    y, s_bd = pl.pallas_call(
        _ret_kernel,
        grid=(bsz, t_len // chunk),
        in_specs=[col(0), col(1), col(2), col(3),
                  pl.BlockSpec((1, BRANCH_W, BRANCH_W), lambda b, c: (b, 0, 0)),
                  tab, tab, const((N_HEADS, chunk, chunk)), const((chunk, BRANCH_W)),
                  const((chunk, BRANCH_W)), const((1, BRANCH_W))],
        out_specs=[pl.BlockSpec((1, chunk, BRANCH_W), lambda b, c: (b, c, 0)),
                   pl.BlockSpec((1, BRANCH_W, BRANCH_W), lambda b, c: (b, 0, 0))],
        out_shape=[jax.ShapeDtypeStruct((bsz, t_len, BRANCH_W), F32),
                   jax.ShapeDtypeStruct((bsz, BRANCH_W, BRANCH_W), F32)],
        scratch_shapes=[pltpu.VMEM((BRANCH_W, BRANCH_W), F32)],
        compiler_params=_params(("arbitrary", "arbitrary")),
        name="retention",
    )(proj3, proj3, proj3, proj3, s0, cos_t, sin_t, dmask, xi, zeta, g_chunk)
    return y, _diag_blocks(s_bd)


def _topk_rows(gs_t, n_valid):
    nblk = gs_t.shape[0]
    blk = lax.broadcasted_iota(jnp.int32, gs_t.shape, 0)
    valid = blk < n_valid
    gsm = jnp.where(valid, gs_t, NEG_INF)
    cnt = jnp.zeros(gs_t.shape, F32)
    for m in range(nblk):
        row = gsm[m:m + 1, :]
        cnt = cnt + jnp.where(row > gsm, 1.0, jnp.where((row == gsm) & (blk > m), 1.0, 0.0))
    return jnp.where(valid & (cnt < MOBA_TOPK), 1.0, 0.0)


def _dot_nt3(a, b):
    ah = a.astype(BF16)
    al = (a - ah.astype(F32)).astype(BF16)
    bh = b.astype(BF16)
    bl = (b - bh.astype(F32)).astype(BF16)
    d = lambda x, y: lax.dot_general(x, y, (((1,), (1,)), ((), ())), preferred_element_type=F32)
    return d(ah, bh) + d(ah, bl) + d(al, bh)


def _moba_kernel(q_ref, k_ref, v_ref, o_ref, km_scr, kb_scr, vb_scr, *, nblk):
    qi = pl.program_id(1)
    bs = MOBA_BLOCK
    masks = _head_masks()

    @pl.when(qi == 0)
    def _():
        for n in range(nblk):
            rows = slice(n * bs, (n + 1) * bs)
            kblk = k_ref[0, rows, :]
            kb_scr[rows, :] = kblk.astype(BF16)
            vb_scr[rows, :] = v_ref[0, rows, :].astype(BF16)
            km = jnp.mean(kblk, axis=0, keepdims=True)
            for h, mh in enumerate(masks):
                km_scr[h * nblk + n:h * nblk + n + 1, :] = km * mh

    q = q_ref[0]
    scale = HEAD_DIM ** -0.5
    ri = lax.broadcasted_iota(jnp.int32, (bs, bs), 0)
    ci = lax.broadcasted_iota(jnp.int32, (bs, bs), 1)
    eye_b = (lax.broadcasted_iota(jnp.int32, (nblk, 128), 0)
             == lax.broadcasted_iota(jnp.int32, (nblk, 128), 1)).astype(F32)
    blk = lax.broadcasted_iota(jnp.int32, (1, 128), 1)
    own = pl.ds(pl.multiple_of(qi * bs, bs), bs)
    k_own = kb_scr[own, :]
    v_own = vb_scr[own, :]
    gs_all = _dot_nt3(km_scr[...], q)
    heads = range(N_HEADS)
    nt = lambda x, y: lax.dot_general(x, y, (((1,), (1,)), ((), ())), preferred_element_type=F32)
    pv = lambda p, vblk: jnp.dot(p.astype(BF16), vblk, preferred_element_type=F32)
    spread = lambda cols: sum(masks[h] * cols[h] for h in heads)
    qh = [(q * masks[h]).astype(BF16) for h in heads]
    sel = [lax.dot_general(_topk_rows(gs_all[h * nblk:(h + 1) * nblk], qi), eye_b, (((0,), (0,)), ((), ())),
                           preferred_element_type=F32) for h in heads]
    s = [jnp.where(ci <= ri, nt(qh[h], k_own) * scale, NEG_INF) for h in heads]
    m0 = [jnp.max(s[h], axis=1, keepdims=True) for h in heads]
    p = [jnp.exp(s[h] - m0[h]) for h in heads]
    l0 = [jnp.sum(p[h], axis=1, keepdims=True) for h in heads]
    acc0 = sum(masks[h] * pv(p[h], v_own) for h in heads)

    def body(n, carry):
        m, l, acc = carry
        rows = pl.ds(pl.multiple_of(n * bs, bs), bs)
        kblk = kb_scr[rows, :]
        vblk = vb_scr[rows, :]
        seln = [jnp.sum(jnp.where(blk == n, sel[h], 0.0), axis=1, keepdims=True) for h in heads]
        s = [jnp.where(seln[h] > 0.0, nt(qh[h], kblk) * scale, NEG_INF) for h in heads]
        m_new = [jnp.maximum(m[h], jnp.max(s[h], axis=1, keepdims=True)) for h in heads]
        alpha = [jnp.exp(m[h] - m_new[h]) for h in heads]
        p = [jnp.exp(s[h] - m_new[h]) for h in heads]
        l = [alpha[h] * l[h] + jnp.sum(p[h], axis=1, keepdims=True) for h in heads]
        acc = spread(alpha) * acc + sum(masks[h] * pv(p[h], vblk) for h in heads)
        return tuple(m_new), tuple(l), acc

    _, l, acc = lax.fori_loop(0, qi, body, (tuple(m0), tuple(l0), acc0))
    o_ref[0] = acc / spread(l)


def _moba_prompt(proj3):
    bsz, t_len, _ = proj3.shape
    nblk = t_len // MOBA_BLOCK
    full = lambda k: pl.BlockSpec((1, t_len, BRANCH_W), lambda b, i: (b, 0, COL_MOBA + k))
    return pl.pallas_call(
        functools.partial(_moba_kernel, nblk=nblk),
        grid=(bsz, nblk),
        in_specs=[pl.BlockSpec((1, MOBA_BLOCK, BRANCH_W), lambda b, i: (b, i, COL_MOBA)), full(1), full(2)],
        out_specs=pl.BlockSpec((1, MOBA_BLOCK, BRANCH_W), lambda b, i: (b, i, 0)),
        out_shape=jax.ShapeDtypeStruct((bsz, t_len, BRANCH_W), F32),
        scratch_shapes=[pltpu.VMEM((N_HEADS * nblk, BRANCH_W), F32), pltpu.VMEM((t_len, BRANCH_W), BF16),
                        pltpu.VMEM((t_len, BRANCH_W), BF16)],
        compiler_params=_params(("arbitrary", "arbitrary")),
        name="moba_prompt",
    )(proj3, proj3, proj3)


def _merge_kernel(x_ref, y0_ref, y1_ref, y2_ref, y3_ref, g0_ref, g1_ref, g2_ref, g3_ref, gt_ref,
                  wb_ref, wo_ref, o_ref, wb_scr, wo_scr):
    @pl.when(pl.program_id(0) == 0)
    def _():
        wb_scr[...] = wb_ref[0].astype(BF16)
        wo_scr[...] = wo_ref[0].astype(BF16)

    mixed = None
    for g, (y_ref, g_ref) in enumerate(((y0_ref, g0_ref), (y1_ref, g1_ref), (y2_ref, g2_ref), (y3_ref, g3_ref))):
        y = y_ref[...] if len(y_ref.shape) == 2 else y_ref[0]
        up = jnp.dot(y.astype(BF16), wb_scr[g], preferred_element_type=F32)
        term = jax.nn.sigmoid(g_ref[...].astype(F32)) * up
        mixed = term if mixed is None else mixed + term
    o_ref[...] = x_ref[...] + gt_ref[0] * jnp.dot(mixed.astype(BF16), wo_scr[...], preferred_element_type=F32)


def _merge(x2d, y_s5, s5_spec, y_rwkv, y_ret, y_moba, gates, mod3, w_branch, w_out, layer, *, tm,
           tiles_per_group):
    rows = x2d.shape[0]
    m = mod3.shape[1]
    tpg = tiles_per_group
    tiles_per_seq = y_rwkv.shape[1] // tm
    ysp = pl.BlockSpec((1, tm, BRANCH_W), lambda i: (i // tiles_per_seq, i % tiles_per_seq, 0))
    gate = lambda g: pl.BlockSpec((tm, D_MODEL), lambda i: (i, g))
    return pl.pallas_call(
        _merge_kernel,
        grid=(rows // tm,),
        in_specs=[pl.BlockSpec((tm, D_MODEL), lambda i: (i, 0)), s5_spec, ysp, ysp, ysp,
                  gate(0), gate(1), gate(2), gate(3),
                  pl.BlockSpec((1, m, D_MODEL), lambda i: (i // tpg, 0, 2)),
                  pl.BlockSpec((1, N_BRANCH, BRANCH_W, D_MODEL), lambda i: (layer, 0, 0, 0)),
                  pl.BlockSpec((1, D_MODEL, D_MODEL), lambda i: (layer, 0, 0))],
        out_specs=pl.BlockSpec((tm, D_MODEL), lambda i: (i, 0)),
        out_shape=jax.ShapeDtypeStruct((rows, D_MODEL), F32),
        scratch_shapes=[pltpu.VMEM((N_BRANCH, BRANCH_W, D_MODEL), BF16), pltpu.VMEM((D_MODEL, D_MODEL), BF16)],
        compiler_params=_params(("arbitrary",)),
        name="merge",
    )(x2d, y_s5, y_rwkv, y_ret, y_moba, gates, gates, gates, gates, mod3, w_branch, w_out)


def _router_combine(h, router):
    logits = _dot_hi(h, router)
    lane = lax.broadcasted_iota(jnp.int32, logits.shape, 1)
    logits = jnp.where(lane < N_EXPERTS, logits, NEG_INF)
    m1 = jnp.max(logits, axis=1, keepdims=True)
    i1 = jnp.min(jnp.where(logits == m1, lane, ROUTER_PAD), axis=1, keepdims=True)
    rest = jnp.where(lane == i1, NEG_INF, logits)
    m2 = jnp.max(rest, axis=1, keepdims=True)
    i2 = jnp.min(jnp.where(rest == m2, lane, ROUTER_PAD), axis=1, keepdims=True)
    e2 = jnp.exp(m2 - m1)
    den = 1.0 + e2
    comb = jnp.where(lane == i1, 1.0 / den, 0.0) + jnp.where(lane == i2, e2 / den, 0.0)
    return comb, jnp.where((lane == i1) | (lane == i2), 1.0, 0.0)


def _swiglu(hb, w1, w3, w2):
    a = jnp.dot(hb, w1, preferred_element_type=F32)
    b = jnp.dot(hb, w3, preferred_element_type=F32)
    act = (a * jax.nn.sigmoid(a)) * b
    return jnp.dot(act.astype(BF16), w2, preferred_element_type=F32)


def _ffn_kernel(x_ref, g_ref, sc_ref, sh_ref, gt_ref, w1_ref, w3_ref, w2_ref, o_ref, h_scr, acc_scr, *, n_j):
    j = pl.program_id(1)

    @pl.when(j == 0)
    def _():
        h_scr[...] = _modulated_norm(x_ref[...], g_ref[...], sc_ref[0], sh_ref[0]).astype(BF16)
        acc_scr[...] = jnp.zeros(acc_scr.shape, F32)

    acc_scr[...] += _swiglu(h_scr[...], w1_ref[0].astype(BF16), w3_ref[0].astype(BF16),
                            w2_ref[0].astype(BF16))

    @pl.when(j == n_j - 1)
    def _():
        o_ref[...] = x_ref[...] + gt_ref[0] * acc_scr[...]


def _ffn(x2d, g, mod3, weights, idx, *, tm, tiles_per_group, tf):
    rows = x2d.shape[0]
    m = mod3.shape[1]
    tpg = tiles_per_group
    w1, w3, w2 = weights
    n_j = w1.shape[2] // tf
    modspec = lambda k: pl.BlockSpec((1, m, D_MODEL), lambda i, j: (i // tpg, 0, k))
    return pl.pallas_call(
        functools.partial(_ffn_kernel, n_j=n_j),
        grid=(rows // tm, n_j),
        in_specs=[pl.BlockSpec((tm, D_MODEL), lambda i, j: (i, 0)),
                  pl.BlockSpec((1, D_MODEL), lambda i, j: (0, 0)),
                  modspec(4), modspec(3), modspec(5),
                  pl.BlockSpec((1, D_MODEL, tf), lambda i, j: (idx, 0, j)),
                  pl.BlockSpec((1, D_MODEL, tf), lambda i, j: (idx, 0, j)),
                  pl.BlockSpec((1, tf, D_MODEL), lambda i, j: (idx, j, 0))],
        out_specs=pl.BlockSpec((tm, D_MODEL), lambda i, j: (i, 0)),
        out_shape=jax.ShapeDtypeStruct((rows, D_MODEL), F32),
        scratch_shapes=[pltpu.VMEM((tm, D_MODEL), BF16), pltpu.VMEM((tm, D_MODEL), F32)],
        compiler_params=_params(("arbitrary", "arbitrary")),
        name="dense_ffn",
    )(x2d, g.reshape(1, D_MODEL), mod3, mod3, mod3, w1, w3, w2)


def _moe_kernel(x_ref, g_ref, sc_ref, sh_ref, gt_ref, rt_ref, w1_ref, w3_ref, w2_ref, o_ref,
                h_scr, comb_scr, asg_scr, rank_scr, *, n_e, cap, sub):
    e = pl.program_id(1)
    tm = x_ref.shape[0]
    subs = [slice(s * sub, (s + 1) * sub) for s in range(tm // sub)]

    @pl.when(e == 0)
    def _():
        o_ref[...] = jnp.zeros(o_ref.shape, F32)
        below = (lax.broadcasted_iota(jnp.int32, (sub, sub), 0)
                 > lax.broadcasted_iota(jnp.int32, (sub, sub), 1)).astype(BF16)
        for rows in subs:
            per_row = sc_ref.shape[1] > 1
            sc = sc_ref[0, rows, :] if per_row else sc_ref[0]
            sh = sh_ref[0, rows, :] if per_row else sh_ref[0]
            h = _modulated_norm(x_ref[rows, :], g_ref[...], sc, sh)
            h_scr[rows, :] = h.astype(BF16)
            comb, asg = _router_combine(h, rt_ref[...])
            comb_scr[rows, :] = comb
            asg_scr[rows, :] = asg
            rank_scr[rows, :] = jnp.dot(below, asg.astype(BF16), preferred_element_type=F32)

    lane = lax.broadcasted_iota(jnp.int32, (sub, ROUTER_PAD), 1)
    slot = lax.broadcasted_iota(jnp.int32, (1, cap), 1).astype(F32)
    w1 = w1_ref[0, 0]
    w3 = w3_ref[0, 0]
    w2 = w2_ref[0, 0]
    for rows in subs:
        pick = lambda ref: jnp.sum(jnp.where(lane == e, ref[rows, :], 0.0), axis=1, keepdims=True)
        a_col = pick(asg_scr)
        r_col = pick(rank_scr)
        c_col = pick(comb_scr)
        count = jnp.sum(a_col).astype(jnp.int32)

        def one_pass(pi, carry, rows=rows, a_col=a_col, r_col=r_col, c_col=c_col):
            base = (pi * cap).astype(F32)
            sel = jnp.where((r_col - base == slot) & (a_col > 0.0), 1.0, 0.0).astype(BF16)
            packed = lax.dot_general(sel, h_scr[rows, :], (((0,), (0,)), ((), ())), preferred_element_type=F32)
            y = _swiglu(packed.astype(BF16), w1, w3, w2)
            y_hi = y.astype(BF16)
            y_lo = (y - y_hi.astype(F32)).astype(BF16)
            spread = (jnp.dot(sel, y_hi, preferred_element_type=F32)
                      + jnp.dot(sel, y_lo, preferred_element_type=F32))
            o_ref[rows, :] += c_col * spread
            return carry

        lax.fori_loop(0, (count + cap - 1) // cap, one_pass, 0)

    @pl.when(e == n_e - 1)
    def _():
        o_ref[...] = x_ref[...] + gt_ref[0] * o_ref[...]


def _moe(x2d, g, mod3, weights, idx, *, tm, tiles_per_group, sub):
    rows = x2d.shape[0]
    m = mod3.shape[1]
    tpg = tiles_per_group
    cap = _moe_cap(sub)
    router, w1, w3, w2 = weights
    router = jnp.zeros((D_MODEL, ROUTER_PAD), F32).at[:, :N_EXPERTS].set(router[idx])
    _, n_e, _, d_ff = w1.shape
    modspec = lambda k: pl.BlockSpec((1, m, D_MODEL), lambda i, e: (i // tpg, 0, k))
    return pl.pallas_call(
        functools.partial(_moe_kernel, n_e=n_e, cap=cap, sub=sub),
        grid=(rows // tm, n_e),
        in_specs=[pl.BlockSpec((tm, D_MODEL), lambda i, e: (i, 0)),
                  pl.BlockSpec((1, D_MODEL), lambda i, e: (0, 0)),
                  modspec(4), modspec(3), modspec(5),
                  pl.BlockSpec((D_MODEL, ROUTER_PAD), lambda i, e: (0, 0)),
                  pl.BlockSpec((1, 1, D_MODEL, d_ff), lambda i, e: (idx, e, 0, 0)),
                  pl.BlockSpec((1, 1, D_MODEL, d_ff), lambda i, e: (idx, e, 0, 0)),
                  pl.BlockSpec((1, 1, d_ff, D_MODEL), lambda i, e: (idx, e, 0, 0))],
        out_specs=pl.BlockSpec((tm, D_MODEL), lambda i, e: (i, 0)),
        out_shape=jax.ShapeDtypeStruct((rows, D_MODEL), F32),
        scratch_shapes=[pltpu.VMEM((tm, D_MODEL), BF16),
                        pltpu.VMEM((tm, ROUTER_PAD), F32), pltpu.VMEM((tm, ROUTER_PAD), F32),
                        pltpu.VMEM((tm, ROUTER_PAD), F32)],
        compiler_params=_params(("arbitrary", "arbitrary")),
        name="moe_ffn",
    )(x2d, g.reshape(1, D_MODEL), mod3, mod3, mod3, router, w1, w3, w2)


N_PRE = 9


FT_R, FT_KMOD, FT_V, FT_DECAY, FT_AA, FT_BB, FT_QRET, FT_KRET, FT_VRET, FT_MQ, FT_MK, FT_MV = range(12)
N_FT = 12


def _dec_pre_kernel(r_ref, k_ref, v_ref, l_ref, q_ref, kr_ref, vr_ref, mq_ref, mk_ref, mv_ref, shift_ref,
                    mu_ref, w0_ref, w2_ref, a0_ref, a2_ref, g2_ref, kk_ref, ka_ref, cos_ref, sin_ref,
                    o_ref, ft_ref):
    pieces = []
    for idx, ref in enumerate((r_ref, k_ref, v_ref, l_ref)):
        lanes = slice(idx * BRANCH_W, (idx + 1) * BRANCH_W)
        x = ref[...]
        pieces.append(x + (shift_ref[:, lanes] - x) * mu_ref[:, lanes])
    ones_bd = _head_ones()
    prm = (w0_ref[...], w2_ref[...], a0_ref[...], a2_ref[...], g2_ref[...], kk_ref[...], ka_ref[...])
    r, k_mod, v, lw, aa, bb, g = _rwkv_features(*pieces, prm, ones_bd)
    q_r = _rope(q_ref[...], cos_ref[...], sin_ref[...])
    k_r = _rope(kr_ref[...], cos_ref[...], sin_ref[...]) * (HEAD_DIM ** -0.5)
    for idx, val in enumerate((r, k_mod, v, jnp.exp(lw), aa, bb, g, q_r, k_r)):
        o_ref[:, idx * BRANCH_W:(idx + 1) * BRANCH_W] = val
    for idx, val in enumerate((r, k_mod, v, jnp.exp(lw), aa, bb, q_r, k_r, vr_ref[...], mq_ref[...],
                               mk_ref[...], mv_ref[...])):
        ft_ref[idx * BRANCH_W:(idx + 1) * BRANCH_W, :] = val.T


def _dec_pre(proj, shift0, rp, pos0):
    rows = proj.shape[0]
    cos_t, sin_t = _rope_tables(1, pos0)
    col = lambda k: pl.BlockSpec((rows, BRANCH_W), lambda i: (0, k))
    const = lambda shape: pl.BlockSpec(shape, lambda i: (0,) * len(shape))
    vec = const((1, BRANCH_W))
    mat = const((BRANCH_W, BRANCH_W))
    mu, w0, w2p, a0, a2p, g2p, k_k, k_a = rp[:8]
    return pl.pallas_call(
        _dec_pre_kernel,
        grid=(1,),
        in_specs=[col(COL_RWKV), col(COL_RWKV + 1), col(COL_RWKV + 2), col(COL_RWKV + 3),
                  col(COL_RET), col(COL_RET + 1), col(COL_RET + 2),
                  col(COL_MOBA), col(COL_MOBA + 1), col(COL_MOBA + 2),
                  const((rows, RWKV_IN_W)), const((1, RWKV_IN_W)),
                  vec, mat, vec, mat, mat, vec, vec, vec, vec],
        out_specs=[const((rows, N_PRE * BRANCH_W)), const((N_FT * BRANCH_W, rows))],
        out_shape=[jax.ShapeDtypeStruct((rows, N_PRE * BRANCH_W), F32),
                   jax.ShapeDtypeStruct((N_FT * BRANCH_W, rows), F32)],
        compiler_params=_params(("arbitrary",)),
        name="dec_pre",
    )(*([proj] * 10), shift0, mu, w0, w2p, a0, a2p, g2p, k_k, k_a, cos_t, sin_t)


def _dec_state_kernel(sw_ref, sr_ref, aa_ref, w_ref, bb_ref, km_ref, r_ref, vv_ref, q_ref, kc_ref, vr_ref,
                      gm_ref, sw_out, sr_out, y_out, o_out):
    aa = aa_ref[...]
    w = w_ref[...]
    bb = bb_ref[...]
    km = km_ref[...]
    r = r_ref[...]

    def rwkv_row(v, carry):
        s = sw_ref[0, 0, v]
        sa = jnp.sum(s * aa, axis=0, keepdims=True)
        s = s * w + sa * bb + vv_ref[pl.ds(v, 1), :] * km
        sw_out[0, v] = s
        y_out[pl.ds(v, 1), :] = jnp.sum(s * r, axis=0, keepdims=True)
        return carry

    lax.fori_loop(0, HEAD_DIM, rwkv_row, 0)

    gm = gm_ref[0, 0:1, :]
    vr = vr_ref[...]

    def ret_row(d, inter):
        t = sr_ref[0, 0, d]
        inter = inter + q_ref[pl.ds(d, 1), :] * t
        sr_out[0, d] = t * gm + kc_ref[pl.ds(d, 1), :] * vr
        return inter

    inter = lax.fori_loop(0, HEAD_DIM, ret_row, jnp.zeros(vr.shape, F32))
    att = jnp.sum(q_ref[...] * kc_ref[...], axis=0, keepdims=True)
    o_out[...] = att * vr + inter * gm


def _dec_state(s_rwkv_t, s_ret_t, feat_t, layer):
    bsz = feat_t.shape[1]
    log_gamma = jnp.log(1.0 - jnp.exp2(-5.0 - jnp.arange(N_HEADS, dtype=F32)))
    gamma = jnp.broadcast_to(jnp.exp(log_gamma)[:, None, None], (N_HEADS, 8, bsz))
    state = pl.BlockSpec((1, 1, HEAD_DIM, HEAD_DIM, bsz), lambda h: (layer, h, 0, 0, 0))
    feat = lambda piece: pl.BlockSpec((HEAD_DIM, bsz), lambda h: (piece * N_HEADS + h, 0))
    new_state = pl.BlockSpec((1, HEAD_DIM, HEAD_DIM, bsz), lambda h: (h, 0, 0, 0))
    head_rows = pl.BlockSpec((HEAD_DIM, bsz), lambda h: (h, 0))
    state_shape = jax.ShapeDtypeStruct((N_HEADS, HEAD_DIM, HEAD_DIM, bsz), F32)
    rows_shape = jax.ShapeDtypeStruct((BRANCH_W, bsz), F32)
    return pl.pallas_call(
        _dec_state_kernel,
        grid=(N_HEADS,),
        in_specs=[state, state, feat(FT_AA), feat(FT_DECAY), feat(FT_BB), feat(FT_KMOD), feat(FT_R),
                  feat(FT_V), feat(FT_QRET), feat(FT_KRET), feat(FT_VRET),
                  pl.BlockSpec((1, 8, bsz), lambda h: (h, 0, 0))],
        out_specs=[new_state, new_state, head_rows, head_rows],
        out_shape=[state_shape, state_shape, rows_shape, rows_shape],
        compiler_params=_params(("arbitrary",)),
        name="dec_state",
    )(s_rwkv_t, s_ret_t, *([feat_t] * 9), gamma)


def _dec_post_kernel(y_ref, r_ref, km_ref, v_ref, g_ref, o_ref, gr_ref, rk_ref, lng_ref, lnb_ref,
                     yw_out, yr_out):
    ones_bd = _head_ones()
    yw_out[...] = _rwkv_post(y_ref[...].T, r_ref[...], km_ref[...], v_ref[...], g_ref[...],
                             rk_ref[...], lng_ref[...], lnb_ref[...], ones_bd)
    o = o_ref[...].T
    o = o * lax.rsqrt(_head_sum(o * o, ones_bd) * (1.0 / HEAD_DIM) + RMS_EPS)
    g = gr_ref[...]
    yr_out[...] = o * (g * jax.nn.sigmoid(g))


def _dec_post(y_rwkv_t, pre, o_ret_t, proj, rp):
    rows = proj.shape[0]
    blk = pl.BlockSpec((rows, BRANCH_W), lambda i: (0, 0))
    blk_t = pl.BlockSpec((BRANCH_W, rows), lambda i: (0, 0))
    col = lambda k: pl.BlockSpec((rows, BRANCH_W), lambda i: (0, k))
    vec = pl.BlockSpec((1, BRANCH_W), lambda i: (0, 0))
    r_k, ln_g, ln_b = rp[8:]
    return pl.pallas_call(
        _dec_post_kernel,
        grid=(1,),
        in_specs=[blk_t, col(0), col(1), col(2), col(6), blk_t, col(COL_RET + 3), vec, vec, vec],
        out_specs=[blk, blk],
        out_shape=[jax.ShapeDtypeStruct((rows, BRANCH_W), F32)] * 2,
        compiler_params=_params(("arbitrary",)),
        name="dec_post",
    )(y_rwkv_t, pre, pre, pre, pre, o_ret_t, proj, r_k, ln_g, ln_b)


def _moba_dec_kernel(pt_ref, q_ref, kn_ref, vn_ref, *refs, n_pages):
    del pt_ref
    k_refs = refs[:n_pages]
    v_refs = refs[n_pages:2 * n_pages]
    o_ref, sc_scr = refs[2 * n_pages:]
    page = k_refs[0].shape[-1]
    per = MOBA_BLOCK // page
    nblk = n_pages // per
    half = N_HEADS * nblk
    scale = HEAD_DIM ** -0.5
    b = pl.program_id(0)
    mine = lax.broadcasted_iota(jnp.int32, (1, q_ref.shape[1]), 1) == b
    column = lambda ref, h: jnp.sum(jnp.where(mine, ref[h * HEAD_DIM:(h + 1) * HEAD_DIM, :], 0.0),
                                    axis=1, keepdims=True)

    @pl.when(b == 0)
    def _():
        o_ref[...] = jnp.zeros(o_ref.shape, F32)

    q_cols = [column(q_ref, h) for h in range(N_HEADS)]
    for h in range(N_HEADS):
        q_c = q_cols[h]
        for pg in range(n_pages):
            row = (pg % per) * half + h * nblk + pg // per
            sc_scr[row:row + 1, :] = jnp.sum(k_refs[pg][0, 0, h] * q_c, axis=0, keepdims=True)
    raw = sc_scr[...]
    rs = jnp.sum(raw, axis=1, keepdims=True)
    gate = (rs[:half] + rs[half:]) * (1.0 / MOBA_BLOCK)
    ri = lax.broadcasted_iota(jnp.int32, (half, half), 0)
    ci = lax.broadcasted_iota(jnp.int32, (half, half), 1)
    g_self = jnp.broadcast_to(gate, (half, half))
    g_other = _dot_hi(jnp.ones((half, half), F32), jnp.where(ri == ci, g_self, 0.0))
    beats = jnp.where(g_other > g_self, 1.0, jnp.where((g_other == g_self) & (ci < ri), 1.0, 0.0))
    beats = jnp.where(ri // nblk == ci // nblk, beats, 0.0)
    sel = jnp.sum(beats, axis=1, keepdims=True) < MOBA_TOPK
    sel2 = jnp.concatenate([sel.astype(F32)] * per, axis=0) > 0.0
    masked = jnp.where(sel2, raw * scale, NEG_INF)
    for h in range(N_HEADS):
        q_c = q_cols[h]
        s_own = jnp.sum(q_c * column(kn_ref, h), axis=0, keepdims=True) * scale
        parts = [masked[par * half + h * nblk:par * half + (h + 1) * nblk] for par in range(per)]
        m = s_own
        for part in parts:
            m = jnp.maximum(m, jnp.max(jnp.max(part, axis=1, keepdims=True), axis=0, keepdims=True))
        p_own = jnp.exp(s_own - m)
        l = p_own
        acc = jnp.zeros((HEAD_DIM, page), F32)
        for par, part in enumerate(parts):
            p = jnp.exp(part - m)
            l = l + jnp.sum(jnp.sum(p, axis=1, keepdims=True), axis=0, keepdims=True)
            for n in range(nblk):
                acc = acc + p[n:n + 1, :] * v_refs[n * per + par][0, 0, h]
        o = (jnp.sum(acc, axis=1, keepdims=True) + p_own * column(vn_ref, h)) / l
        rows = slice(h * HEAD_DIM, (h + 1) * HEAD_DIM)
        o_ref[rows, :] = jnp.where(mine, o, o_ref[rows, :])


def _moba_dec(feat_t, k_t, v_t, page_table, layer):
    bsz, n_pages = page_table.shape
    page = k_t.shape[-1]
    fspec = lambda piece: pl.BlockSpec((BRANCH_W, bsz), lambda b, pt: (piece, 0))
    pspec = lambda pg: pl.BlockSpec((1, 1, N_HEADS, HEAD_DIM, page), lambda b, pt: (layer, pt[b, pg], 0, 0, 0))
    return pl.pallas_call(
        functools.partial(_moba_dec_kernel, n_pages=n_pages),
        grid_spec=pltpu.PrefetchScalarGridSpec(
            num_scalar_prefetch=1,
            grid=(bsz,),
            in_specs=[fspec(FT_MQ), fspec(FT_MK), fspec(FT_MV)] + [pspec(pg) for pg in range(n_pages)] * 2,
            out_specs=pl.BlockSpec((BRANCH_W, bsz), lambda b, pt: (0, 0)),
            scratch_shapes=[pltpu.VMEM((N_HEADS * n_pages, page), F32)]),
        out_shape=jax.ShapeDtypeStruct((BRANCH_W, bsz), F32),
        compiler_params=_params(("arbitrary",)),
        name="moba_decode",
    )(page_table, feat_t, feat_t, feat_t, *([k_t] * n_pages), *([v_t] * n_pages))


def _final_norm_kernel(x_ref, g_ref, o_ref):
    x = x_ref[...]
    o_ref[...] = x * lax.rsqrt(jnp.mean(x * x, axis=-1, keepdims=True) + RMS_EPS) * g_ref[...]


def _final_norm(x2d, g, *, tm):
    rows = x2d.shape[0]
    return pl.pallas_call(
        _final_norm_kernel,
        grid=(rows // tm,),
        in_specs=[pl.BlockSpec((tm, D_MODEL), lambda i: (i, 0)), pl.BlockSpec((1, D_MODEL), lambda i: (0, 0))],
        out_specs=pl.BlockSpec((tm, D_MODEL), lambda i: (i, 0)),
        out_shape=jax.ShapeDtypeStruct((rows, D_MODEL), F32),
        compiler_params=_params(("arbitrary",)),
        name="final_norm",
    )(x2d, g.reshape(1, D_MODEL))


RWKV_BATCHES_PER_STEP = 8
RET_CHUNK = 256
S5_TIME_CHUNK = 128


MOE_TILE = 1024
MOE_SUB = 512
BF16_ROWS = 16


def _moe_cap(tm):
    return -(-(tm * 5 // 16) // BF16_ROWS) * BF16_ROWS


def _ffn_any(x2d, g, mod3, ffn, idx, *, rows_per_group):
    if len(ffn) == 3:
        tm = min(1024, rows_per_group)
        return _ffn(x2d, g, mod3, ffn, idx, tm=tm, tiles_per_group=rows_per_group // tm, tf=256)
    tm = min(MOE_TILE, rows_per_group)
    return _moe(x2d, g, mod3, ffn, idx, tm=tm, tiles_per_group=rows_per_group // tm, sub=min(MOE_SUB, tm))


def _prompt_layer(x2d, bsz, t_len, mod_l, lp, layer):
    mod3 = mod_l.reshape(bsz, 1, -1)
    tm = min(2048, t_len)
    tpg = t_len // tm
    proj, gates, u_tb = _inproj(x2d, lp['norm_mix'], mod3, lp['w_in'], layer, tm=tm, tiles_per_group=tpg,
                                tb_shape=(t_len, bsz * BRANCH_W))
    proj3 = proj.reshape(bsz, t_len, MIX_W)
    z_state = jnp.zeros((bsz, S5_W), F32)
    y_s5, s5_re, s5_im = _s5(u_tb.reshape(t_len * bsz, BRANCH_W), z_state, z_state, lp['s5p'], lp['s5_d'],
                             lp['s5_w_glu'], nb=bsz, t_len=t_len, tc=min(S5_TIME_CHUNK, t_len))
    z_bd = jnp.zeros((bsz, BRANCH_W, BRANCH_W), F32)
    y_rwkv, s_rwkv, shift_n = _rwkv(proj3, jnp.zeros((bsz, RWKV_IN_W), F32), z_bd, lp['rwkv'],
                                    nb=RWKV_BATCHES_PER_STEP)
    y_ret, s_ret = _ret(proj3, z_bd, 0, chunk=min(RET_CHUNK, t_len))
    y_moba = _moba_prompt(proj3)
    kv = lambda k: proj3[:, :, (COL_MOBA + k) * BRANCH_W:(COL_MOBA + k + 1) * BRANCH_W].reshape(
        bsz, t_len, N_HEADS, HEAD_DIM)
    tmm = min(256, t_len)
    tpm = t_len // tmm
    s5_spec = pl.BlockSpec((tmm, BRANCH_W), lambda i: (i % tpm, i // tpm))
    x2d = _merge(x2d, y_s5.reshape(t_len, bsz * BRANCH_W), s5_spec, y_rwkv, y_ret, y_moba,
                 gates, mod3, lp['w_branch'], lp['w_out'], layer, tm=tmm, tiles_per_group=tpm)
    x2d = _ffn_any(x2d, lp['norm_ffn'], mod3, lp['ffn'], layer // 2, rows_per_group=t_len)
    g16 = (bsz, S5_GROUPS, S5_STATE)
    return x2d, (s5_re.reshape(g16), s5_im.reshape(g16), s_rwkv, shift_n, s_ret, kv(1), kv(2))


def _decode_layer(x2d, mod_l, lp, layer, pos0, s5_re0, s5_im0, s_rwkv_t, shift0, s_ret_t, cache_k, cache_v,
                  page_table):
    bsz = x2d.shape[0]
    mod3 = mod_l.reshape(1, bsz, -1)
    proj, gates = _inproj(x2d, lp['norm_mix'], mod3, lp['w_in'], layer, tm=bsz, tiles_per_group=1)
    piece = lambda k: proj[:, k * BRANCH_W:(k + 1) * BRANCH_W]
    y_s5, s5_re, s5_im = _s5(piece(COL_S5), s5_re0.reshape(bsz, S5_W), s5_im0.reshape(bsz, S5_W), lp['s5p'],
                             lp['s5_d'], lp['s5_w_glu'], nb=bsz, t_len=1, tc=1)
    pre, feat_t = _dec_pre(proj, shift0, lp['rwkv'], pos0)
    s_rwkv, s_ret, y_raw_t, o_raw_t = _dec_state(s_rwkv_t, s_ret_t, feat_t, layer)
    y_rwkv, y_ret = _dec_post(y_raw_t, pre, o_raw_t, proj, lp['rwkv'])
    k_new = piece(COL_MOBA + 1)
    v_new = piece(COL_MOBA + 2)
    y_moba = _moba_dec(feat_t, cache_k, cache_v, page_table, layer).T
    s5_spec = pl.BlockSpec((bsz, BRANCH_W), lambda i: (i, 0))
    x2d = _merge(x2d, y_s5, s5_spec, y_rwkv[None], y_ret[None], y_moba[None], gates, mod3, lp['w_branch'],
                 lp['w_out'], layer, tm=bsz, tiles_per_group=1)
    x2d = _ffn_any(x2d, lp['norm_ffn'], mod3, lp['ffn'], layer // 2, rows_per_group=bsz)
    g16 = (bsz, S5_GROUPS, S5_STATE)
    kv4 = lambda t: t.reshape(bsz, 1, N_HEADS, HEAD_DIM)
    shift_n = proj[:, COL_RWKV * BRANCH_W:COL_RWKV * BRANCH_W + RWKV_IN_W]
    return x2d, (s5_re.reshape(g16), s5_im.reshape(g16), s_rwkv, shift_n, s_ret, kv4(k_new), kv4(v_new))


def kernel(x_prompt, x_sample, c_prompt, c_sample, state_s5_re, state_s5_im, state_rwkv, state_rwkv_shift, state_ret, cache_moba_k, cache_moba_v, page_table, norm_mix_g, norm_ffn_g, norm_final_g, w_ada, b_ada, w_in, s5_lambda_re, s5_lambda_im, s5_log_dt, s5_b_re, s5_b_im, s5_c_re, s5_c_im, s5_d, s5_w_glu, rwkv_mu, rwkv_w0, rwkv_w2, rwkv_a0, rwkv_a2, rwkv_g2, rwkv_k_k, rwkv_k_a, rwkv_r_k, rwkv_ln_g, rwkv_ln_b, w_branch, w_out, ffn_w1, ffn_w3, ffn_w2, moe_router, moe_w1, moe_w3, moe_w2):
    bp, t_len, _ = x_prompt.shape
    bs = x_sample.shape[0]
    depth = w_in.shape[0]
    past_len = page_table.shape[1] * cache_moba_k.shape[2]
    cache_kt = jnp.transpose(cache_moba_k, (0, 1, 3, 4, 2))
    cache_vt = jnp.transpose(cache_moba_v, (0, 1, 3, 4, 2))
    s_rwkv_t = jnp.transpose(state_rwkv, (0, 2, 3, 4, 1))
    s_ret_t = jnp.transpose(state_ret, (0, 2, 3, 4, 1))
    mod_all = _ada(jnp.concatenate([c_prompt, c_sample], axis=0), w_ada, b_ada)
    xp = x_prompt.reshape(bp * t_len, D_MODEL)
    xs = x_sample.reshape(bs, D_MODEL)
    outs_p = [[] for _ in range(7)]
    outs_s = [[] for _ in range(7)]
    dense = (ffn_w1, ffn_w3, ffn_w2)
    experts = (moe_router, moe_w1.astype(BF16), moe_w3.astype(BF16), moe_w2.astype(BF16))
    for l in range(depth):
        lp = {
            'norm_mix': norm_mix_g[l], 'norm_ffn': norm_ffn_g[l], 'w_in': w_in,
            's5p': _s5_params(s5_lambda_re[l], s5_lambda_im[l], s5_log_dt[l], s5_b_re[l], s5_b_im[l],
                              s5_c_re[l], s5_c_im[l]),
            's5_d': s5_d[l], 's5_w_glu': s5_w_glu[l],
            'rwkv': _rwkv_params(rwkv_mu[l], rwkv_w0[l], rwkv_w2[l], rwkv_a0[l], rwkv_a2[l], rwkv_g2[l],
                                 rwkv_k_k[l], rwkv_k_a[l], rwkv_r_k[l], rwkv_ln_g[l], rwkv_ln_b[l]),
            'w_branch': w_branch, 'w_out': w_out, 'ffn': dense if l % 2 == 0 else experts,
        }
        xp, st_p = _prompt_layer(xp, bp, t_len, mod_all[l, :bp], lp, l)
        xs, st_s = _decode_layer(xs, mod_all[l, bp:], lp, l, past_len, state_s5_re[l], state_s5_im[l],
                                 s_rwkv_t, state_rwkv_shift[l], s_ret_t, cache_kt, cache_vt, page_table)
        for j in range(7):
            outs_p[j].append(st_p[j])
            outs_s[j].append(st_s[j])
    y_prompt = _final_norm(xp, norm_final_g, tm=1024).reshape(bp, t_len, D_MODEL)
    y_sample = _final_norm(xs, norm_final_g, tm=bs).reshape(bs, 1, D_MODEL)
    stack = lambda outs: [jnp.stack(o, axis=0) for o in outs]
    dec = stack(outs_s)
    for j in (2, 4):
        dec[j] = jnp.transpose(dec[j], (0, 4, 1, 2, 3))
    return (y_prompt, y_sample, *stack(outs_p), *dec)
```

```python
import functools
import math

import jax
import jax.numpy as jnp
from jax import lax
from jax.experimental import pallas as pl
from jax.experimental.pallas import tpu as pltpu

F32 = jnp.float32
BF16 = jnp.bfloat16
HIGHEST = lax.Precision.HIGHEST

D_MODEL = 1024
BRANCH_W = 256
HEAD_DIM = 64
N_HEADS = 4
N_BRANCH = 4
S5_GROUPS = 16
S5_STATE = 64
S5_CH = 16
S5_W = S5_GROUPS * S5_STATE
IN_W = 7168
RWKV_IN_W = 1024
RWKV_LN_EPS = 64e-5
RMS_EPS = 1e-6
ROPE_BASE = 10000.0
MOBA_BLOCK = 256
MOBA_TOPK = 3
N_EXPERTS = 8
ROUTER_PAD = 128
NEG_INF = float("-inf")

COL_S5 = 0
COL_RWKV = 1
COL_RET = 5
COL_MOBA = 9
COL_GATE = 12

VMEM_LIMIT = 48 * 1024 * 1024


VMEM_LIMIT_MOE = 56 * 1024 * 1024


def _params(sem, vmem_limit=VMEM_LIMIT):
    return pltpu.CompilerParams(dimension_semantics=sem, vmem_limit_bytes=vmem_limit)


def _dot(a, b):
    return jnp.dot(a.astype(BF16), b.astype(BF16), preferred_element_type=F32)


def _dot_hi(a, b):
    return jnp.dot(a, b, precision=HIGHEST, preferred_element_type=F32)


def _dot_nt(a, b):
    return lax.dot_general(a.astype(BF16), b.astype(BF16), (((1,), (1,)), ((), ())),
                           preferred_element_type=F32)


def _dot_nt_hi(a, b):
    return lax.dot_general(a, b, (((1,), (1,)), ((), ())), precision=HIGHEST,
                           preferred_element_type=F32)


def _dot_tn(a, b):
    return lax.dot_general(a.astype(BF16), b.astype(BF16), (((0,), (0,)), ((), ())),
                           preferred_element_type=F32)


def _head_masks(width=BRANCH_W):
    lane = lax.broadcasted_iota(jnp.int32, (1, width), 1)
    return [(lane // HEAD_DIM == h).astype(F32) for h in range(N_HEADS)]


def _head_ones():
    r = lax.broadcasted_iota(jnp.int32, (BRANCH_W, BRANCH_W), 0) // HEAD_DIM
    c = lax.broadcasted_iota(jnp.int32, (BRANCH_W, BRANCH_W), 1) // HEAD_DIM
    return (r == c).astype(F32)


def _split3(x):
    hi = x.astype(BF16)
    rest = x - hi.astype(F32)
    mid = rest.astype(BF16)
    return hi, mid, (rest - mid.astype(F32)).astype(BF16)


def _head_sum(x, ones_bd):
    w = ones_bd.astype(BF16)
    return sum(jnp.dot(p, w, preferred_element_type=F32) for p in _split3(x))


def _ada_kernel(c_ref, w_ref, b_ref, o_ref):
    c = c_ref[...]
    h = c * jax.nn.sigmoid(c)
    o_ref[0] = _dot(h, w_ref[0]) + b_ref[0]


def _ada(c_all, w_ada, b_ada):
    depth, _, width = w_ada.shape
    rows = c_all.shape[0]
    tn = 1024
    return pl.pallas_call(
        _ada_kernel,
        grid=(depth, width // tn),
        in_specs=[pl.BlockSpec((rows, D_MODEL), lambda l, j: (0, 0)),
                  pl.BlockSpec((1, D_MODEL, tn), lambda l, j: (l, 0, j)),
                  pl.BlockSpec((1, 1, tn), lambda l, j: (l, 0, j))],
        out_specs=pl.BlockSpec((1, rows, tn), lambda l, j: (l, 0, j)),
        out_shape=jax.ShapeDtypeStruct((depth, rows, width), F32),
        compiler_params=_params(("arbitrary", "arbitrary")),
        name="ada_mod",
    )(c_all, w_ada, b_ada.reshape(depth, 1, width))


def _modulated_norm(x, g, sc, sh):
    y = x * lax.rsqrt(jnp.mean(x * x, axis=-1, keepdims=True) + RMS_EPS) * g
    return y * (1.0 + sc) + sh


INPROJ_TN = 512
MIX_W = COL_GATE * BRANCH_W
GATE_W = IN_W - MIX_W


def _inproj_kernel(x_ref, g_ref, sc_ref, sh_ref, w_ref, o_ref, gate_ref, *rest, emit_tb):
    h_scr = rest[-1]
    j = pl.program_id(1)
    n_mix = MIX_W // INPROJ_TN

    @pl.when(j == 0)
    def _():
        h_scr[...] = _modulated_norm(x_ref[...], g_ref[...], sc_ref[0], sh_ref[0]).astype(BF16)

    acc = jnp.dot(h_scr[...], w_ref[0].astype(BF16), preferred_element_type=F32)

    @pl.when(j < n_mix)
    def _():
        o_ref[...] = acc

    @pl.when(j >= n_mix)
    def _():
        gate_ref[...] = acc.astype(BF16)

    if emit_tb:
        u_ref = rest[0]

        @pl.when(j == 0)
        def _():
            u_ref[...] = acc[:, :BRANCH_W]


def _inproj(x2d, g, mod3, w, layer, *, tm, tiles_per_group, tb_shape=None):
    rows = x2d.shape[0]
    m = mod3.shape[1]
    tn = INPROJ_TN
    n_mix = MIX_W // tn
    tpg = tiles_per_group
    in_specs = [pl.BlockSpec((tm, D_MODEL), lambda i, j: (i, 0)),
                pl.BlockSpec((1, D_MODEL), lambda i, j: (0, 0)),
                pl.BlockSpec((1, m, D_MODEL), lambda i, j: (i // tpg, 0, 1)),
                pl.BlockSpec((1, m, D_MODEL), lambda i, j: (i // tpg, 0, 0)),
                pl.BlockSpec((1, D_MODEL, tn), lambda i, j: (layer, 0, j))]
    out_specs = [pl.BlockSpec((tm, tn), lambda i, j: (i, jnp.minimum(j, n_mix - 1))),
                 pl.BlockSpec((tm, tn), lambda i, j: (i, jnp.maximum(j - n_mix, 0)))]
    out_shape = [jax.ShapeDtypeStruct((rows, MIX_W), F32), jax.ShapeDtypeStruct((rows, GATE_W), BF16)]
    if tb_shape is not None:
        out_specs.append(pl.BlockSpec((tm, BRANCH_W), lambda i, j: (i % tpg, i // tpg)))
        out_shape.append(jax.ShapeDtypeStruct(tb_shape, F32))
    return pl.pallas_call(
        functools.partial(_inproj_kernel, emit_tb=tb_shape is not None),
        grid=(rows // tm, IN_W // tn),
        in_specs=in_specs, out_specs=out_specs, out_shape=out_shape,
        scratch_shapes=[pltpu.VMEM((tm, D_MODEL), BF16)],
        compiler_params=_params(("arbitrary", "arbitrary")),
        name="inproj",
    )(x2d, g.reshape(1, D_MODEL), mod3, mod3, w)


def _s5_kernel(u_ref, x0r_ref, x0i_ref, lbr_ref, lbi_ref, br_ref, bi_ref, cr_ref, ci_ref, d_ref, wg_ref,
               y_ref, sr_ref, si_ref, bur, bui, xr, xi, *, nb, tc):
    c = pl.program_id(0)

    @pl.when(c == 0)
    def _():
        xr[...] = x0r_ref[...]
        xi[...] = x0i_ref[...]

    u = u_ref[...]
    ub = u.astype(BF16)
    bur[...] = jnp.dot(ub, br_ref[...].astype(BF16), preferred_element_type=F32)
    bui[...] = jnp.dot(ub, bi_ref[...].astype(BF16), preferred_element_type=F32)
    lbr = jnp.broadcast_to(lbr_ref[...], (nb, S5_W))
    lbi = jnp.broadcast_to(lbi_ref[...], (nb, S5_W))

    def body(t, carry):
        sr, si = carry
        rows = pl.ds(pl.multiple_of(t * nb, nb), nb)
        nr = lbr * sr - lbi * si + bur[rows, :]
        ni = lbr * si + lbi * sr + bui[rows, :]
        bur[rows, :] = nr
        bui[rows, :] = ni
        return nr, ni

    sr, si = lax.fori_loop(0, tc, body, (xr[...], xi[...]))
    xr[...] = sr
    xi[...] = si
    sr_ref[...] = sr
    si_ref[...] = si
    y = _dot(bur[...], cr_ref[...]) - _dot(bui[...], ci_ref[...]) + d_ref[...] * u
    z = jax.nn.gelu(y)
    y_ref[...] = z * jax.nn.sigmoid(_dot(z, wg_ref[...]))


def _s5_params(lam_re, lam_im, log_dt, b_re, b_im, c_re, c_im):
    dt = jnp.exp(log_dt)[:, None]
    mag = jnp.exp(lam_re * dt)
    lb_re = mag * jnp.cos(lam_im * dt)
    lb_im = mag * jnp.sin(lam_im * dt)
    den = lam_re * lam_re + lam_im * lam_im
    q_re = ((lb_re - 1.0) * lam_re + lb_im * lam_im) / den
    q_im = (lb_im * lam_re - (lb_re - 1.0) * lam_im) / den
    bb_re = q_re[..., None] * b_re - q_im[..., None] * b_im
    bb_im = q_re[..., None] * b_im + q_im[..., None] * b_re
    eye = jnp.eye(S5_GROUPS, dtype=F32)
    to_in = lambda bb: jnp.einsum('gnc,gh->gchn', bb, eye).reshape(BRANCH_W, S5_W)
    to_out = lambda cc: jnp.einsum('gcn,gh->gnhc', cc, eye).reshape(S5_W, BRANCH_W)
    return (lb_re.reshape(1, S5_W), lb_im.reshape(1, S5_W), to_in(bb_re), to_in(bb_im),
            to_out(c_re), to_out(c_im))


def _s5(u_tb, x0_re, x0_im, s5p, d_skip, w_glu, *, nb, t_len, tc):
    lb_re, lb_im, bin_re, bin_im, cout_re, cout_im = s5p
    rows = tc * nb
    const = lambda shape: pl.BlockSpec(shape, lambda c: (0,) * len(shape))
    return pl.pallas_call(
        functools.partial(_s5_kernel, nb=nb, tc=tc),
        grid=(t_len // tc,),
        in_specs=[pl.BlockSpec((rows, BRANCH_W), lambda c: (c, 0)),
                  const((nb, S5_W)), const((nb, S5_W)), const((1, S5_W)), const((1, S5_W)),
                  const((BRANCH_W, S5_W)), const((BRANCH_W, S5_W)),
                  const((S5_W, BRANCH_W)), const((S5_W, BRANCH_W)),
                  const((1, BRANCH_W)), const((BRANCH_W, BRANCH_W))],
        out_specs=[pl.BlockSpec((rows, BRANCH_W), lambda c: (c, 0)), const((nb, S5_W)), const((nb, S5_W))],
        out_shape=[jax.ShapeDtypeStruct((t_len * nb, BRANCH_W), F32),
                   jax.ShapeDtypeStruct((nb, S5_W), F32), jax.ShapeDtypeStruct((nb, S5_W), F32)],
        scratch_shapes=[pltpu.VMEM((rows, S5_W), F32), pltpu.VMEM((rows, S5_W), F32),
                        pltpu.VMEM((nb, S5_W), F32), pltpu.VMEM((nb, S5_W), F32)],
        compiler_params=_params(("arbitrary",)),
        name="s5",
    )(u_tb, x0_re, x0_im, lb_re, lb_im, bin_re, bin_im, cout_re, cout_im,
      d_skip.reshape(1, BRANCH_W), w_glu)


def _rwkv_features(pm_r, pm_k, pm_v, pm_l, prm, ones_bd):
    w0, w2p, a0, a2p, g2p, k_k, k_a = prm
    w_raw = w0 + _dot(jnp.tanh(pm_l), w2p)
    lw = -jax.nn.sigmoid(w_raw) * math.exp(-0.5)
    a_sig = jax.nn.sigmoid(a0 + _dot(pm_l, a2p))
    g = _dot(jax.nn.sigmoid(pm_l), g2p)
    kk = pm_k * k_k
    kk = kk * lax.rsqrt(_head_sum(kk * kk, ones_bd) + 1e-12)
    k_mod = pm_k * (1.0 + (a_sig - 1.0) * k_a)
    return pm_r, k_mod, pm_v, lw, -kk, kk * a_sig, g


def _rwkv_post(y, r, k_mod, v, g, r_k, ln_g, ln_b, ones_bd):
    inv = 1.0 / HEAD_DIM
    mean = _head_sum(y, ones_bd) * inv
    yc = y - mean
    var = _head_sum(yc * yc, ones_bd) * inv
    yn = yc * lax.rsqrt(var + RWKV_LN_EPS) * ln_g + ln_b
    bonus = _head_sum(r * k_mod * r_k, ones_bd) * v
    return (yn + bonus) * g


RWKV_CHUNK = 64


def _cumsum_rows(tril, x):
    return sum(jnp.dot(tril, p, preferred_element_type=F32) for p in _split3(x))


def _per_head(x, ones_bd):
    return jnp.concatenate([x.astype(BF16)] * N_HEADS, axis=0) * ones_bd.astype(BF16)


def _rwkv_chunks(r, k_mod, v, lw, aa, bb, states, ones_bd):
    n = RWKV_CHUNK
    nb = len(states)
    seqs = range(nb)
    part = lambda x, b: x[b * n:(b + 1) * n]
    t_idx = lax.broadcasted_iota(jnp.int32, (n, BRANCH_W), 0)
    i_idx = lax.broadcasted_iota(jnp.int32, (n, BRANCH_W), 1) % n
    strict = (t_idx > i_idx).astype(F32)
    incl = (t_idx >= i_idx).astype(F32)
    eye_c = (t_idx == i_idx).astype(F32)
    tril = (lax.broadcasted_iota(jnp.int32, (n, n), 0) >= lax.broadcasted_iota(jnp.int32, (n, n), 1)).astype(BF16)
    bd = lambda x: _per_head(x, ones_bd)
    cum = jnp.concatenate([_cumsum_rows(tril, part(lw, b)) for b in seqs], axis=0)
    tot = [part(cum, b)[n - 1:n, :] for b in seqs]
    tot_rows = jnp.concatenate([jnp.broadcast_to(t, (n, BRANCH_W)) for t in tot], axis=0)
    e_neg = jnp.exp(-cum)
    e_rem = jnp.exp(tot_rows - cum)
    a_t = aa * jnp.exp(cum - lw)
    r_t = r * jnp.exp(cum)
    b_t = bb * e_neg
    k_t = k_mod * e_neg
    b_h = bb * e_rem
    k_h = k_mod * e_rem
    ar = [jnp.concatenate([part(a_t, b), part(r_t, b)], axis=0) for b in seqs]
    xb = [_dot_nt(ar[b], bd(part(b_t, b))) for b in seqs]
    xk = [_dot_nt(ar[b], bd(part(k_t, b))) for b in seqs]
    ss = [_dot_nt(ar[b], states[b]) for b in seqs]
    v_bd = [bd(part(v, b)) for b in seqs]
    pw = [xb[b][:n] * strict for b in seqs]
    mv = [_dot(xk[b][:n] * strict, v_bd[b]) for b in seqs]
    tinv = [eye_c + pw[b] for b in seqs]
    for _ in range(int(math.log2(n)) - 1):
        pw = [_dot(pw[b], bd(pw[b])) for b in seqs]
        tinv = [tinv[b] + _dot(pw[b], bd(tinv[b])) for b in seqs]
    u = [_dot(tinv[b], bd(ss[b][:n] + mv[b])) for b in seqs]
    y = [ss[b][n:] + _dot(xb[b][n:] * incl, bd(u[b])) + _dot(xk[b][n:] * incl, v_bd[b]) for b in seqs]
    upd = [_dot_tn(jnp.concatenate([u[b], part(v, b)], axis=0),
                   jnp.concatenate([part(b_h, b), part(k_h, b)], axis=0)) for b in seqs]
    new_states = [states[b] * jnp.exp(tot[b]) + ones_bd * upd[b] for b in seqs]
    return jnp.concatenate(y, axis=0), new_states


def _rwkv_kernel(r_ref, k_ref, v_ref, l_ref, shift_ref, s0_ref, mu_ref, w0_ref, w2_ref, a0_ref, a2_ref,
                 g2_ref, kk_ref, ka_ref, rk_ref, lng_ref, lnb_ref,
                 y_ref, s_ref, sh_ref, prev_scr, s_scr, *, nb):
    c = pl.program_id(1)
    n = RWKV_CHUNK

    @pl.when(c == 0)
    def _():
        prev_scr[...] = shift_ref[...]
        s_scr[...] = s0_ref[...]

    row = lax.broadcasted_iota(jnp.int32, (n, BRANCH_W), 0)
    ones_bd = _head_ones()
    prm = (w0_ref[...], w2_ref[...], a0_ref[...], a2_ref[...], g2_ref[...], kk_ref[...], ka_ref[...])
    pieces = []
    for idx, ref in enumerate((r_ref, k_ref, v_ref, l_ref)):
        lanes = slice(idx * BRANCH_W, (idx + 1) * BRANCH_W)
        shifted = []
        for b in range(nb):
            x = ref[b]
            x_prev = jnp.where(row == 0, prev_scr[b, :, lanes], pltpu.roll(x, 1, 0))
            shifted.append(x + (x_prev - x) * mu_ref[:, lanes])
            prev_scr[b, :, lanes] = x[n - 1:n, :]
            sh_ref[b, :, lanes] = x[n - 1:n, :]
        pieces.append(jnp.concatenate(shifted, axis=0))
    r, k_mod, v, lw, aa, bb, g = _rwkv_features(*pieces, prm, ones_bd)
    y, new_states = _rwkv_chunks(r, k_mod, v, lw, aa, bb, [s_scr[b] for b in range(nb)], ones_bd)
    out = _rwkv_post(y, r, k_mod, v, g, rk_ref[...], lng_ref[...], lnb_ref[...], ones_bd)
    for b in range(nb):
        s_scr[b] = new_states[b]
        s_ref[b] = new_states[b]
        y_ref[b] = out[b * n:(b + 1) * n]


def _rwkv_params(mu, w0, w2, a0, a2, g2, k_k, k_a, r_k, ln_g, ln_b):
    row = lambda t: t.reshape(1, -1)
    pad = lambda w, lo: jnp.zeros((BRANCH_W, BRANCH_W), F32).at[lo:lo + w.shape[0]].set(w)
    return (row(mu), row(w0), pad(w2, 0), row(a0), pad(a2, 64), pad(g2, 128), row(k_k), row(k_a),
            row(r_k), row(ln_g), row(ln_b))


def _diag_blocks(s_bd):
    bsz = s_bd.shape[0]
    s5 = s_bd.reshape(bsz, N_HEADS, HEAD_DIM, N_HEADS, HEAD_DIM)
    return jnp.stack([s5[:, h, :, h, :] for h in range(N_HEADS)], axis=1)


def _rwkv(proj3, shift0, s0, rp, *, nb):
    bsz, t_len, _ = proj3.shape
    chunk = RWKV_CHUNK
    col = lambda k: pl.BlockSpec((nb, chunk, BRANCH_W), lambda b, c: (b, c, COL_RWKV + k))
    const = lambda shape: pl.BlockSpec(shape, lambda b, c: (0,) * len(shape))
    vec = const((1, BRANCH_W))
    mat = const((BRANCH_W, BRANCH_W))
    y, s_bd, shift_n = pl.pallas_call(
        functools.partial(_rwkv_kernel, nb=nb),
        grid=(bsz // nb, t_len // chunk),
        in_specs=[col(0), col(1), col(2), col(3),
                  pl.BlockSpec((nb, 1, RWKV_IN_W), lambda b, c: (b, 0, 0)),
                  pl.BlockSpec((nb, BRANCH_W, BRANCH_W), lambda b, c: (b, 0, 0)),
                  const((1, RWKV_IN_W)), vec, mat, vec, mat, mat, vec, vec, vec, vec, vec],
        out_specs=[pl.BlockSpec((nb, chunk, BRANCH_W), lambda b, c: (b, c, 0)),
                   pl.BlockSpec((nb, BRANCH_W, BRANCH_W), lambda b, c: (b, 0, 0)),
                   pl.BlockSpec((nb, 1, RWKV_IN_W), lambda b, c: (b, 0, 0))],
        out_shape=[jax.ShapeDtypeStruct((bsz, t_len, BRANCH_W), F32),
                   jax.ShapeDtypeStruct((bsz, BRANCH_W, BRANCH_W), F32),
                   jax.ShapeDtypeStruct((bsz, 1, RWKV_IN_W), F32)],
        scratch_shapes=[pltpu.VMEM((nb, 1, RWKV_IN_W), F32), pltpu.VMEM((nb, BRANCH_W, BRANCH_W), F32)],
        compiler_params=_params(("arbitrary", "arbitrary")),
        name="rwkv",
    )(proj3, proj3, proj3, proj3, shift0.reshape(bsz, 1, RWKV_IN_W), s0, *rp)
    return y, _diag_blocks(s_bd), shift_n.reshape(bsz, RWKV_IN_W)


def _rope_tables(t_len, pos0):
    half = HEAD_DIM // 2
    freqs = 1.0 / (ROPE_BASE ** jnp.linspace(0.0, 1.0, half, dtype=F32))
    pos = jnp.arange(t_len, dtype=F32) + pos0
    ang = pos[:, None] * freqs[None, :]
    cos = jnp.cos(ang)
    sin = jnp.sin(ang)
    cos_t = jnp.tile(jnp.concatenate([cos, cos], axis=-1), (1, N_HEADS))
    sin_t = jnp.tile(jnp.concatenate([-sin, sin], axis=-1), (1, N_HEADS))
    return cos_t, sin_t


def _rope(x, cos_t, sin_t):
    lane = lax.broadcasted_iota(jnp.int32, x.shape, 1)
    first = (lane % HEAD_DIM) < (HEAD_DIM // 2)
    swapped = jnp.where(first, pltpu.roll(x, BRANCH_W - HEAD_DIM // 2, 1), pltpu.roll(x, HEAD_DIM // 2, 1))
    return x * cos_t + swapped * sin_t


def _ret_tables(chunk):
    log_gamma = jnp.log(1.0 - jnp.exp2(-5.0 - jnp.arange(N_HEADS, dtype=F32)))
    i = jnp.arange(chunk, dtype=F32)
    diff = i[:, None] - i[None, :]
    dmask = jnp.where(diff >= 0, jnp.exp(log_gamma[:, None, None] * jnp.maximum(diff, 0.0)), 0.0)
    lanes = lambda t: jnp.repeat(t, HEAD_DIM, axis=-1)
    xi = lanes(jnp.exp(log_gamma[None, :] * (i[:, None] + 1.0)))
    zeta = lanes(jnp.exp(log_gamma[None, :] * (chunk - 1.0 - i[:, None])))
    g_chunk = lanes(jnp.exp(log_gamma * chunk)[None, :])
    return dmask, xi, zeta, g_chunk


def _ret_kernel(q_ref, k_ref, v_ref, g_ref, s0_ref, cos_ref, sin_ref, dm_ref, xi_ref, zeta_ref, gch_ref,
                y_ref, s_ref, s_scr):
    c = pl.program_id(1)

    @pl.when(c == 0)
    def _():
        s_scr[...] = s0_ref[0]

    cos_t = cos_ref[...]
    sin_t = sin_ref[...]
    q = _rope(q_ref[0], cos_t, sin_t)
    k = _rope(k_ref[0], cos_t, sin_t) * (HEAD_DIM ** -0.5)
    v = v_ref[0]
    s = s_scr[...]
    ones_bd = _head_ones()
    o = _dot(q, s) * xi_ref[...]
    for h, mh in enumerate(_head_masks()):
        att = _dot_nt(q * mh, k) * dm_ref[h]
        o = o + _dot(att, v) * mh
    s_new = s * gch_ref[...] + ones_bd * _dot_tn(k * zeta_ref[...], v)
    s_scr[...] = s_new
    s_ref[0] = s_new
    o = o * lax.rsqrt(_head_sum(o * o, ones_bd) * (1.0 / HEAD_DIM) + RMS_EPS)
    g = g_ref[0]
    y_ref[0] = o * (g * jax.nn.sigmoid(g))


def _ret(proj3, s0, pos0, *, chunk):
    bsz, t_len, _ = proj3.shape
    cos_t, sin_t = _rope_tables(t_len, pos0)
    dmask, xi, zeta, g_chunk = _ret_tables(chunk)
    col = lambda k: pl.BlockSpec((1, chunk, BRANCH_W), lambda b, c: (b, c, COL_RET + k))
    const = lambda shape: pl.BlockSpec(shape, lambda b, c: (0,) * len(shape))
    tab = pl.BlockSpec((chunk, BRANCH_W), lambda b, c: (c, 0))
    y, s_bd = pl.pallas_call(
        _ret_kernel,
        grid=(bsz, t_len // chunk),
        in_specs=[col(0), col(1), col(2), col(3),
                  pl.BlockSpec((1, BRANCH_W, BRANCH_W), lambda b, c: (b, 0, 0)),
                  tab, tab, const((N_HEADS, chunk, chunk)), const((chunk, BRANCH_W)),
                  const((chunk, BRANCH_W)), const((1, BRANCH_W))],
        out_specs=[pl.BlockSpec((1, chunk, BRANCH_W), lambda b, c: (b, c, 0)),
                   pl.BlockSpec((1, BRANCH_W, BRANCH_W), lambda b, c: (b, 0, 0))],
        out_shape=[jax.ShapeDtypeStruct((bsz, t_len, BRANCH_W), F32),
                   jax.ShapeDtypeStruct((bsz, BRANCH_W, BRANCH_W), F32)],
        scratch_shapes=[pltpu.VMEM((BRANCH_W, BRANCH_W), F32)],
        compiler_params=_params(("arbitrary", "arbitrary")),
        name="retention",
    )(proj3, proj3, proj3, proj3, s0, cos_t, sin_t, dmask, xi, zeta, g_chunk)
    return y, _diag_blocks(s_bd)


def _topk_rows(gs_t, n_valid):
    nblk = gs_t.shape[0]
    blk = lax.broadcasted_iota(jnp.int32, gs_t.shape, 0)
    valid = blk < n_valid
    gsm = jnp.where(valid, gs_t, NEG_INF)
    cnt = jnp.zeros(gs_t.shape, F32)
    for m in range(nblk):
        row = gsm[m:m + 1, :]
        cnt = cnt + jnp.where(row > gsm, 1.0, jnp.where((row == gsm) & (blk > m), 1.0, 0.0))
    return jnp.where(valid & (cnt < MOBA_TOPK), 1.0, 0.0)


def _dot_nt3(a, b):
    ah = a.astype(BF16)
    al = (a - ah.astype(F32)).astype(BF16)
    bh = b.astype(BF16)
    bl = (b - bh.astype(F32)).astype(BF16)
    d = lambda x, y: lax.dot_general(x, y, (((1,), (1,)), ((), ())), preferred_element_type=F32)
    return d(ah, bh) + d(ah, bl) + d(al, bh)


def _moba_kernel(q_ref, k_ref, v_ref, o_ref, km_scr, kb_scr, vb_scr, *, nblk):
    qi = pl.program_id(1)
    bs = MOBA_BLOCK
    masks = _head_masks()

    @pl.when(qi == 0)
    def _():
        for n in range(nblk):
            rows = slice(n * bs, (n + 1) * bs)
            kblk = k_ref[0, rows, :]
            kb_scr[rows, :] = kblk.astype(BF16)
            vb_scr[rows, :] = v_ref[0, rows, :].astype(BF16)
            km = jnp.mean(kblk, axis=0, keepdims=True)
            for h, mh in enumerate(masks):
                km_scr[h * nblk + n:h * nblk + n + 1, :] = km * mh

    q = q_ref[0]
    scale = HEAD_DIM ** -0.5
    ri = lax.broadcasted_iota(jnp.int32, (bs, bs), 0)
    ci = lax.broadcasted_iota(jnp.int32, (bs, bs), 1)
    eye_b = (lax.broadcasted_iota(jnp.int32, (nblk, 128), 0)
             == lax.broadcasted_iota(jnp.int32, (nblk, 128), 1)).astype(F32)
    blk = lax.broadcasted_iota(jnp.int32, (1, 128), 1)
    own = pl.ds(pl.multiple_of(qi * bs, bs), bs)
    k_own = kb_scr[own, :]
    v_own = vb_scr[own, :]
    gs_all = _dot_nt3(km_scr[...], q)
    heads = range(N_HEADS)
    nt = lambda x, y: lax.dot_general(x, y, (((1,), (1,)), ((), ())), preferred_element_type=F32)
    pv = lambda p, vblk: jnp.dot(p.astype(BF16), vblk, preferred_element_type=F32)
    spread = lambda cols: sum(masks[h] * cols[h] for h in heads)
    qh = [(q * masks[h]).astype(BF16) for h in heads]
    sel = [lax.dot_general(_topk_rows(gs_all[h * nblk:(h + 1) * nblk], qi), eye_b, (((0,), (0,)), ((), ())),
                           preferred_element_type=F32) for h in heads]
    s = [jnp.where(ci <= ri, nt(qh[h], k_own) * scale, NEG_INF) for h in heads]
    m0 = [jnp.max(s[h], axis=1, keepdims=True) for h in heads]
    p = [jnp.exp(s[h] - m0[h]) for h in heads]
    l0 = [jnp.sum(p[h], axis=1, keepdims=True) for h in heads]
    acc0 = sum(masks[h] * pv(p[h], v_own) for h in heads)

    def body(n, carry):
        m, l, acc = carry
        rows = pl.ds(pl.multiple_of(n * bs, bs), bs)
        kblk = kb_scr[rows, :]
        vblk = vb_scr[rows, :]
        seln = [jnp.sum(jnp.where(blk == n, sel[h], 0.0), axis=1, keepdims=True) for h in heads]
        s = [jnp.where(seln[h] > 0.0, nt(qh[h], kblk) * scale, NEG_INF) for h in heads]
        m_new = [jnp.maximum(m[h], jnp.max(s[h], axis=1, keepdims=True)) for h in heads]
        alpha = [jnp.exp(m[h] - m_new[h]) for h in heads]
        p = [jnp.exp(s[h] - m_new[h]) for h in heads]
        l = [alpha[h] * l[h] + jnp.sum(p[h], axis=1, keepdims=True) for h in heads]
        acc = spread(alpha) * acc + sum(masks[h] * pv(p[h], vblk) for h in heads)
        return tuple(m_new), tuple(l), acc

    _, l, acc = lax.fori_loop(0, qi, body, (tuple(m0), tuple(l0), acc0))
    o_ref[0] = acc / spread(l)


def _moba_prompt(proj3):
    bsz, t_len, _ = proj3.shape
    nblk = t_len // MOBA_BLOCK
    full = lambda k: pl.BlockSpec((1, t_len, BRANCH_W), lambda b, i: (b, 0, COL_MOBA + k))
    return pl.pallas_call(
        functools.partial(_moba_kernel, nblk=nblk),
        grid=(bsz, nblk),
        in_specs=[pl.BlockSpec((1, MOBA_BLOCK, BRANCH_W), lambda b, i: (b, i, COL_MOBA)), full(1), full(2)],
        out_specs=pl.BlockSpec((1, MOBA_BLOCK, BRANCH_W), lambda b, i: (b, i, 0)),
        out_shape=jax.ShapeDtypeStruct((bsz, t_len, BRANCH_W), F32),
        scratch_shapes=[pltpu.VMEM((N_HEADS * nblk, BRANCH_W), F32), pltpu.VMEM((t_len, BRANCH_W), BF16),
                        pltpu.VMEM((t_len, BRANCH_W), BF16)],
        compiler_params=_params(("arbitrary", "arbitrary")),
        name="moba_prompt",
    )(proj3, proj3, proj3)


def _merge_kernel(x_ref, y0_ref, y1_ref, y2_ref, y3_ref, g0_ref, g1_ref, g2_ref, g3_ref, gt_ref,
                  wb_ref, wo_ref, o_ref, wb_scr, wo_scr):
    @pl.when(pl.program_id(0) == 0)
    def _():
        wb_scr[...] = wb_ref[0].astype(BF16)
        wo_scr[...] = wo_ref[0].astype(BF16)

    mixed = None
    for g, (y_ref, g_ref) in enumerate(((y0_ref, g0_ref), (y1_ref, g1_ref), (y2_ref, g2_ref), (y3_ref, g3_ref))):
        y = y_ref[...] if len(y_ref.shape) == 2 else y_ref[0]
        up = jnp.dot(y.astype(BF16), wb_scr[g], preferred_element_type=F32)
        term = jax.nn.sigmoid(g_ref[...].astype(F32)) * up
        mixed = term if mixed is None else mixed + term
    o_ref[...] = x_ref[...] + gt_ref[0] * jnp.dot(mixed.astype(BF16), wo_scr[...], preferred_element_type=F32)


def _merge(x2d, y_s5, s5_spec, y_rwkv, y_ret, y_moba, gates, mod3, w_branch, w_out, layer, *, tm,
           tiles_per_group):
    rows = x2d.shape[0]
    m = mod3.shape[1]
    tpg = tiles_per_group
    tiles_per_seq = y_rwkv.shape[1] // tm
    ysp = pl.BlockSpec((1, tm, BRANCH_W), lambda i: (i // tiles_per_seq, i % tiles_per_seq, 0))
    gate = lambda g: pl.BlockSpec((tm, D_MODEL), lambda i: (i, g))
    return pl.pallas_call(
        _merge_kernel,
        grid=(rows // tm,),
        in_specs=[pl.BlockSpec((tm, D_MODEL), lambda i: (i, 0)), s5_spec, ysp, ysp, ysp,
                  gate(0), gate(1), gate(2), gate(3),
                  pl.BlockSpec((1, m, D_MODEL), lambda i: (i // tpg, 0, 2)),
                  pl.BlockSpec((1, N_BRANCH, BRANCH_W, D_MODEL), lambda i: (layer, 0, 0, 0)),
                  pl.BlockSpec((1, D_MODEL, D_MODEL), lambda i: (layer, 0, 0))],
        out_specs=pl.BlockSpec((tm, D_MODEL), lambda i: (i, 0)),
        out_shape=jax.ShapeDtypeStruct((rows, D_MODEL), F32),
        scratch_shapes=[pltpu.VMEM((N_BRANCH, BRANCH_W, D_MODEL), BF16), pltpu.VMEM((D_MODEL, D_MODEL), BF16)],
        compiler_params=_params(("arbitrary",)),
        name="merge",
    )(x2d, y_s5, y_rwkv, y_ret, y_moba, gates, gates, gates, gates, mod3, w_branch, w_out)


def _router_combine(h, router):
    logits = _dot_hi(h, router)
    lane = lax.broadcasted_iota(jnp.int32, logits.shape, 1)
    logits = jnp.where(lane < N_EXPERTS, logits, NEG_INF)
    m1 = jnp.max(logits, axis=1, keepdims=True)
    i1 = jnp.min(jnp.where(logits == m1, lane, ROUTER_PAD), axis=1, keepdims=True)
    rest = jnp.where(lane == i1, NEG_INF, logits)
    m2 = jnp.max(rest, axis=1, keepdims=True)
    i2 = jnp.min(jnp.where(rest == m2, lane, ROUTER_PAD), axis=1, keepdims=True)
    e2 = jnp.exp(m2 - m1)
    den = 1.0 + e2
    comb = jnp.where(lane == i1, 1.0 / den, 0.0) + jnp.where(lane == i2, e2 / den, 0.0)
    return comb, jnp.where((lane == i1) | (lane == i2), 1.0, 0.0)


def _swiglu(hb, w1, w3, w2):
    a = jnp.dot(hb, w1, preferred_element_type=F32)
    b = jnp.dot(hb, w3, preferred_element_type=F32)
    act = (a * jax.nn.sigmoid(a)) * b
    return jnp.dot(act.astype(BF16), w2, preferred_element_type=F32)


def _ffn_kernel(x_ref, g_ref, sc_ref, sh_ref, gt_ref, w1_ref, w3_ref, w2_ref, o_ref, h_scr, acc_scr, *, n_j):
    j = pl.program_id(1)

    @pl.when(j == 0)
    def _():
        h_scr[...] = _modulated_norm(x_ref[...], g_ref[...], sc_ref[0], sh_ref[0]).astype(BF16)
        acc_scr[...] = jnp.zeros(acc_scr.shape, F32)

    acc_scr[...] += _swiglu(h_scr[...], w1_ref[0].astype(BF16), w3_ref[0].astype(BF16),
                            w2_ref[0].astype(BF16))

    @pl.when(j == n_j - 1)
    def _():
        o_ref[...] = x_ref[...] + gt_ref[0] * acc_scr[...]


def _ffn(x2d, g, mod3, weights, idx, *, tm, tiles_per_group, tf):
    rows = x2d.shape[0]
    m = mod3.shape[1]
    tpg = tiles_per_group
    w1, w3, w2 = weights
    n_j = w1.shape[2] // tf
    modspec = lambda k: pl.BlockSpec((1, m, D_MODEL), lambda i, j: (i // tpg, 0, k))
    return pl.pallas_call(
        functools.partial(_ffn_kernel, n_j=n_j),
        grid=(rows // tm, n_j),
        in_specs=[pl.BlockSpec((tm, D_MODEL), lambda i, j: (i, 0)),
                  pl.BlockSpec((1, D_MODEL), lambda i, j: (0, 0)),
                  modspec(4), modspec(3), modspec(5),
                  pl.BlockSpec((1, D_MODEL, tf), lambda i, j: (idx, 0, j)),
                  pl.BlockSpec((1, D_MODEL, tf), lambda i, j: (idx, 0, j)),
                  pl.BlockSpec((1, tf, D_MODEL), lambda i, j: (idx, j, 0))],
        out_specs=pl.BlockSpec((tm, D_MODEL), lambda i, j: (i, 0)),
        out_shape=jax.ShapeDtypeStruct((rows, D_MODEL), F32),
        scratch_shapes=[pltpu.VMEM((tm, D_MODEL), BF16), pltpu.VMEM((tm, D_MODEL), F32)],
        compiler_params=_params(("arbitrary", "arbitrary")),
        name="dense_ffn",
    )(x2d, g.reshape(1, D_MODEL), mod3, mod3, mod3, w1, w3, w2)


def _moe_kernel(x_ref, g_ref, sc_ref, sh_ref, gt_ref, rt_ref, w1_ref, w3_ref, w2_ref, o_ref,
                h_scr, comb_scr, asg_scr, rank_scr, *, n_e, cap, sub):
    e = pl.program_id(1)
    tm = x_ref.shape[0]
    subs = [slice(s * sub, (s + 1) * sub) for s in range(tm // sub)]

    @pl.when(e == 0)
    def _():
        o_ref[...] = jnp.zeros(o_ref.shape, F32)
        below = (lax.broadcasted_iota(jnp.int32, (sub, sub), 0)
                 > lax.broadcasted_iota(jnp.int32, (sub, sub), 1)).astype(BF16)
        for rows in subs:
            per_row = sc_ref.shape[1] > 1
            sc = sc_ref[0, rows, :] if per_row else sc_ref[0]
            sh = sh_ref[0, rows, :] if per_row else sh_ref[0]
            h = _modulated_norm(x_ref[rows, :], g_ref[...], sc, sh)
            h_scr[rows, :] = h.astype(BF16)
            comb, asg = _router_combine(h, rt_ref[...])
            comb_scr[rows, :] = comb
            asg_scr[rows, :] = asg
            rank_scr[rows, :] = jnp.dot(below, asg.astype(BF16), preferred_element_type=F32)

    lane = lax.broadcasted_iota(jnp.int32, (sub, ROUTER_PAD), 1)
    slot = lax.broadcasted_iota(jnp.int32, (1, cap), 1).astype(F32)
    w1 = w1_ref[0, 0]
    w3 = w3_ref[0, 0]
    w2 = w2_ref[0, 0]
    pick = lambda ref, rows: jnp.sum(jnp.where(lane == e, ref[rows, :], 0.0), axis=1, keepdims=True)
    cols = [(pick(asg_scr, rows), pick(rank_scr, rows), pick(comb_scr, rows)) for rows in subs]
    select = lambda a_col, r_col, base: jnp.where((r_col - base == slot) & (a_col > 0.0), 1.0, 0.0).astype(BF16)
    pack = lambda sel, rows: lax.dot_general(sel, h_scr[rows, :], (((0,), (0,)), ((), ())),
                                             preferred_element_type=F32).astype(BF16)

    def spread(sel, y, c_col, rows):
        y_hi = y.astype(BF16)
        y_lo = (y - y_hi.astype(F32)).astype(BF16)
        o_ref[rows, :] += c_col * (jnp.dot(sel, y_hi, preferred_element_type=F32)
                                   + jnp.dot(sel, y_lo, preferred_element_type=F32))

    sels = [select(a_col, r_col, 0.0) for a_col, r_col, _ in cols]
    y_all = _swiglu(jnp.concatenate([pack(sel, rows) for sel, rows in zip(sels, subs)], axis=0), w1, w3, w2)
    for s, (sel, rows) in enumerate(zip(sels, subs)):
        spread(sel, y_all[s * cap:(s + 1) * cap], cols[s][2], rows)

    for rows, (a_col, r_col, c_col) in zip(subs, cols):
        count = jnp.sum(a_col).astype(jnp.int32)

        def one_pass(pi, carry, rows=rows, a_col=a_col, r_col=r_col, c_col=c_col):
            sel = select(a_col, r_col, (pi * cap).astype(F32))
            spread(sel, _swiglu(pack(sel, rows), w1, w3, w2), c_col, rows)
            return carry

        lax.fori_loop(1, (count + cap - 1) // cap, one_pass, 0)

    @pl.when(e == n_e - 1)
    def _():
        o_ref[...] = x_ref[...] + gt_ref[0] * o_ref[...]


def _moe(x2d, g, mod3, weights, idx, *, tm, tiles_per_group, sub):
    rows = x2d.shape[0]
    m = mod3.shape[1]
    tpg = tiles_per_group
    cap = _moe_cap(sub)
    router, w1, w3, w2 = weights
    router = jnp.zeros((D_MODEL, ROUTER_PAD), F32).at[:, :N_EXPERTS].set(router[idx])
    _, n_e, _, d_ff = w1.shape
    modspec = lambda k: pl.BlockSpec((1, m, D_MODEL), lambda i, e: (i // tpg, 0, k))
    return pl.pallas_call(
        functools.partial(_moe_kernel, n_e=n_e, cap=cap, sub=sub),
        grid=(rows // tm, n_e),
        in_specs=[pl.BlockSpec((tm, D_MODEL), lambda i, e: (i, 0)),
                  pl.BlockSpec((1, D_MODEL), lambda i, e: (0, 0)),
                  modspec(4), modspec(3), modspec(5),
                  pl.BlockSpec((D_MODEL, ROUTER_PAD), lambda i, e: (0, 0)),
                  pl.BlockSpec((1, 1, D_MODEL, d_ff), lambda i, e: (idx, e, 0, 0)),
                  pl.BlockSpec((1, 1, D_MODEL, d_ff), lambda i, e: (idx, e, 0, 0)),
                  pl.BlockSpec((1, 1, d_ff, D_MODEL), lambda i, e: (idx, e, 0, 0))],
        out_specs=pl.BlockSpec((tm, D_MODEL), lambda i, e: (i, 0)),
        out_shape=jax.ShapeDtypeStruct((rows, D_MODEL), F32),
        scratch_shapes=[pltpu.VMEM((tm, D_MODEL), BF16),
                        pltpu.VMEM((tm, ROUTER_PAD), F32), pltpu.VMEM((tm, ROUTER_PAD), F32),
                        pltpu.VMEM((tm, ROUTER_PAD), F32)],
        compiler_params=_params(("arbitrary", "arbitrary"), VMEM_LIMIT_MOE),
        name="moe_ffn",
    )(x2d, g.reshape(1, D_MODEL), mod3, mod3, mod3, router, w1, w3, w2)


N_PRE = 9


FT_R, FT_KMOD, FT_V, FT_DECAY, FT_AA, FT_BB, FT_QRET, FT_KRET, FT_VRET, FT_MQ, FT_MK, FT_MV = range(12)
N_FT = 12


def _dec_pre_kernel(r_ref, k_ref, v_ref, l_ref, q_ref, kr_ref, vr_ref, mq_ref, mk_ref, mv_ref, shift_ref,
                    mu_ref, w0_ref, w2_ref, a0_ref, a2_ref, g2_ref, kk_ref, ka_ref, cos_ref, sin_ref,
                    o_ref, ft_ref):
    pieces = []
    for idx, ref in enumerate((r_ref, k_ref, v_ref, l_ref)):
        lanes = slice(idx * BRANCH_W, (idx + 1) * BRANCH_W)
        x = ref[...]
        pieces.append(x + (shift_ref[:, lanes] - x) * mu_ref[:, lanes])
    ones_bd = _head_ones()
    prm = (w0_ref[...], w2_ref[...], a0_ref[...], a2_ref[...], g2_ref[...], kk_ref[...], ka_ref[...])
    r, k_mod, v, lw, aa, bb, g = _rwkv_features(*pieces, prm, ones_bd)
    q_r = _rope(q_ref[...], cos_ref[...], sin_ref[...])
    k_r = _rope(kr_ref[...], cos_ref[...], sin_ref[...]) * (HEAD_DIM ** -0.5)
    for idx, val in enumerate((r, k_mod, v, jnp.exp(lw), aa, bb, g, q_r, k_r)):
        o_ref[:, idx * BRANCH_W:(idx + 1) * BRANCH_W] = val
    for idx, val in enumerate((r, k_mod, v, jnp.exp(lw), aa, bb, q_r, k_r, vr_ref[...], mq_ref[...],
                               mk_ref[...], mv_ref[...])):
        ft_ref[idx * BRANCH_W:(idx + 1) * BRANCH_W, :] = val.T


def _dec_pre(proj, shift0, rp, pos0):
    rows = proj.shape[0]
    cos_t, sin_t = _rope_tables(1, pos0)
    col = lambda k: pl.BlockSpec((rows, BRANCH_W), lambda i: (0, k))
    const = lambda shape: pl.BlockSpec(shape, lambda i: (0,) * len(shape))
    vec = const((1, BRANCH_W))
    mat = const((BRANCH_W, BRANCH_W))
    mu, w0, w2p, a0, a2p, g2p, k_k, k_a = rp[:8]
    return pl.pallas_call(
        _dec_pre_kernel,
        grid=(1,),
        in_specs=[col(COL_RWKV), col(COL_RWKV + 1), col(COL_RWKV + 2), col(COL_RWKV + 3),
                  col(COL_RET), col(COL_RET + 1), col(COL_RET + 2),
                  col(COL_MOBA), col(COL_MOBA + 1), col(COL_MOBA + 2),
                  const((rows, RWKV_IN_W)), const((1, RWKV_IN_W)),
                  vec, mat, vec, mat, mat, vec, vec, vec, vec],
        out_specs=[const((rows, N_PRE * BRANCH_W)), const((N_FT * BRANCH_W, rows))],
        out_shape=[jax.ShapeDtypeStruct((rows, N_PRE * BRANCH_W), F32),
                   jax.ShapeDtypeStruct((N_FT * BRANCH_W, rows), F32)],
        compiler_params=_params(("arbitrary",)),
        name="dec_pre",
    )(*([proj] * 10), shift0, mu, w0, w2p, a0, a2p, g2p, k_k, k_a, cos_t, sin_t)


def _dec_state_kernel(sw_ref, sr_ref, aa_ref, w_ref, bb_ref, km_ref, r_ref, vv_ref, q_ref, kc_ref, vr_ref,
                      gm_ref, sw_out, sr_out, y_out, o_out):
    aa = aa_ref[...]
    w = w_ref[...]
    bb = bb_ref[...]
    km = km_ref[...]
    r = r_ref[...]

    def rwkv_row(v, carry):
        s = sw_ref[0, 0, v]
        sa = jnp.sum(s * aa, axis=0, keepdims=True)
        s = s * w + sa * bb + vv_ref[pl.ds(v, 1), :] * km
        sw_out[0, v] = s
        y_out[pl.ds(v, 1), :] = jnp.sum(s * r, axis=0, keepdims=True)
        return carry

    lax.fori_loop(0, HEAD_DIM, rwkv_row, 0)

    gm = gm_ref[0, 0:1, :]
    vr = vr_ref[...]

    def ret_row(d, inter):
        t = sr_ref[0, 0, d]
        inter = inter + q_ref[pl.ds(d, 1), :] * t
        sr_out[0, d] = t * gm + kc_ref[pl.ds(d, 1), :] * vr
        return inter

    inter = lax.fori_loop(0, HEAD_DIM, ret_row, jnp.zeros(vr.shape, F32))
    att = jnp.sum(q_ref[...] * kc_ref[...], axis=0, keepdims=True)
    o_out[...] = att * vr + inter * gm


def _dec_state(s_rwkv_t, s_ret_t, feat_t, layer):
    bsz = feat_t.shape[1]
    log_gamma = jnp.log(1.0 - jnp.exp2(-5.0 - jnp.arange(N_HEADS, dtype=F32)))
    gamma = jnp.broadcast_to(jnp.exp(log_gamma)[:, None, None], (N_HEADS, 8, bsz))
    state = pl.BlockSpec((1, 1, HEAD_DIM, HEAD_DIM, bsz), lambda h: (layer, h, 0, 0, 0))
    feat = lambda piece: pl.BlockSpec((HEAD_DIM, bsz), lambda h: (piece * N_HEADS + h, 0))
    new_state = pl.BlockSpec((1, HEAD_DIM, HEAD_DIM, bsz), lambda h: (h, 0, 0, 0))
    head_rows = pl.BlockSpec((HEAD_DIM, bsz), lambda h: (h, 0))
    state_shape = jax.ShapeDtypeStruct((N_HEADS, HEAD_DIM, HEAD_DIM, bsz), F32)
    rows_shape = jax.ShapeDtypeStruct((BRANCH_W, bsz), F32)
    return pl.pallas_call(
        _dec_state_kernel,
        grid=(N_HEADS,),
        in_specs=[state, state, feat(FT_AA), feat(FT_DECAY), feat(FT_BB), feat(FT_KMOD), feat(FT_R),
                  feat(FT_V), feat(FT_QRET), feat(FT_KRET), feat(FT_VRET),
                  pl.BlockSpec((1, 8, bsz), lambda h: (h, 0, 0))],
        out_specs=[new_state, new_state, head_rows, head_rows],
        out_shape=[state_shape, state_shape, rows_shape, rows_shape],
        compiler_params=_params(("arbitrary",)),
        name="dec_state",
    )(s_rwkv_t, s_ret_t, *([feat_t] * 9), gamma)


def _dec_post_kernel(y_ref, r_ref, km_ref, v_ref, g_ref, o_ref, gr_ref, rk_ref, lng_ref, lnb_ref,
                     yw_out, yr_out):
    ones_bd = _head_ones()
    yw_out[...] = _rwkv_post(y_ref[...].T, r_ref[...], km_ref[...], v_ref[...], g_ref[...],
                             rk_ref[...], lng_ref[...], lnb_ref[...], ones_bd)
    o = o_ref[...].T
    o = o * lax.rsqrt(_head_sum(o * o, ones_bd) * (1.0 / HEAD_DIM) + RMS_EPS)
    g = gr_ref[...]
    yr_out[...] = o * (g * jax.nn.sigmoid(g))


def _dec_post(y_rwkv_t, pre, o_ret_t, proj, rp):
    rows = proj.shape[0]
    blk = pl.BlockSpec((rows, BRANCH_W), lambda i: (0, 0))
    blk_t = pl.BlockSpec((BRANCH_W, rows), lambda i: (0, 0))
    col = lambda k: pl.BlockSpec((rows, BRANCH_W), lambda i: (0, k))
    vec = pl.BlockSpec((1, BRANCH_W), lambda i: (0, 0))
    r_k, ln_g, ln_b = rp[8:]
    return pl.pallas_call(
        _dec_post_kernel,
        grid=(1,),
        in_specs=[blk_t, col(0), col(1), col(2), col(6), blk_t, col(COL_RET + 3), vec, vec, vec],
        out_specs=[blk, blk],
        out_shape=[jax.ShapeDtypeStruct((rows, BRANCH_W), F32)] * 2,
        compiler_params=_params(("arbitrary",)),
        name="dec_post",
    )(y_rwkv_t, pre, pre, pre, pre, o_ret_t, proj, r_k, ln_g, ln_b)


def _moba_dec_kernel(pt_ref, q_ref, kn_ref, vn_ref, *refs, n_pages):
    del pt_ref
    k_refs = refs[:n_pages]
    v_refs = refs[n_pages:2 * n_pages]
    o_ref, sc_scr = refs[2 * n_pages:]
    page = k_refs[0].shape[-1]
    per = MOBA_BLOCK // page
    nblk = n_pages // per
    half = N_HEADS * nblk
    scale = HEAD_DIM ** -0.5
    b = pl.program_id(0)
    mine = lax.broadcasted_iota(jnp.int32, (1, q_ref.shape[1]), 1) == b
    column = lambda ref, h: jnp.sum(jnp.where(mine, ref[h * HEAD_DIM:(h + 1) * HEAD_DIM, :], 0.0),
                                    axis=1, keepdims=True)

    @pl.when(b == 0)
    def _():
        o_ref[...] = jnp.zeros(o_ref.shape, F32)

    q_cols = [column(q_ref, h) for h in range(N_HEADS)]
    for h in range(N_HEADS):
        q_c = q_cols[h]
        for pg in range(n_pages):
            row = (pg % per) * half + h * nblk + pg // per
            sc_scr[row:row + 1, :] = jnp.sum(k_refs[pg][0, 0, h] * q_c, axis=0, keepdims=True)
    raw = sc_scr[...]
    rs = jnp.sum(raw, axis=1, keepdims=True)
    gate = (rs[:half] + rs[half:]) * (1.0 / MOBA_BLOCK)
    ri = lax.broadcasted_iota(jnp.int32, (half, half), 0)
    ci = lax.broadcasted_iota(jnp.int32, (half, half), 1)
    g_self = jnp.broadcast_to(gate, (half, half))
    g_other = _dot_hi(jnp.ones((half, half), F32), jnp.where(ri == ci, g_self, 0.0))
    beats = jnp.where(g_other > g_self, 1.0, jnp.where((g_other == g_self) & (ci < ri), 1.0, 0.0))
    beats = jnp.where(ri // nblk == ci // nblk, beats, 0.0)
    sel = jnp.sum(beats, axis=1, keepdims=True) < MOBA_TOPK
    sel2 = jnp.concatenate([sel.astype(F32)] * per, axis=0) > 0.0
    masked = jnp.where(sel2, raw * scale, NEG_INF)
    for h in range(N_HEADS):
        q_c = q_cols[h]
        s_own = jnp.sum(q_c * column(kn_ref, h), axis=0, keepdims=True) * scale
        parts = [masked[par * half + h * nblk:par * half + (h + 1) * nblk] for par in range(per)]
        m = s_own
        for part in parts:
            m = jnp.maximum(m, jnp.max(jnp.max(part, axis=1, keepdims=True), axis=0, keepdims=True))
        p_own = jnp.exp(s_own - m)
        l = p_own
        acc = jnp.zeros((HEAD_DIM, page), F32)
        for par, part in enumerate(parts):
            p = jnp.exp(part - m)
            l = l + jnp.sum(jnp.sum(p, axis=1, keepdims=True), axis=0, keepdims=True)
            for n in range(nblk):
                acc = acc + p[n:n + 1, :] * v_refs[n * per + par][0, 0, h]
        o = (jnp.sum(acc, axis=1, keepdims=True) + p_own * column(vn_ref, h)) / l
        rows = slice(h * HEAD_DIM, (h + 1) * HEAD_DIM)
        o_ref[rows, :] = jnp.where(mine, o, o_ref[rows, :])


def _moba_dec(feat_t, k_t, v_t, page_table, layer):
    bsz, n_pages = page_table.shape
    page = k_t.shape[-1]
    fspec = lambda piece: pl.BlockSpec((BRANCH_W, bsz), lambda b, pt: (piece, 0))
    pspec = lambda pg: pl.BlockSpec((1, 1, N_HEADS, HEAD_DIM, page), lambda b, pt: (layer, pt[b, pg], 0, 0, 0))
    return pl.pallas_call(
        functools.partial(_moba_dec_kernel, n_pages=n_pages),
        grid_spec=pltpu.PrefetchScalarGridSpec(
            num_scalar_prefetch=1,
            grid=(bsz,),
            in_specs=[fspec(FT_MQ), fspec(FT_MK), fspec(FT_MV)] + [pspec(pg) for pg in range(n_pages)] * 2,
            out_specs=pl.BlockSpec((BRANCH_W, bsz), lambda b, pt: (0, 0)),
            scratch_shapes=[pltpu.VMEM((N_HEADS * n_pages, page), F32)]),
        out_shape=jax.ShapeDtypeStruct((BRANCH_W, bsz), F32),
        compiler_params=_params(("arbitrary",)),
        name="moba_decode",
    )(page_table, feat_t, feat_t, feat_t, *([k_t] * n_pages), *([v_t] * n_pages))


def _final_norm_kernel(x_ref, g_ref, o_ref):
    x = x_ref[...]
    o_ref[...] = x * lax.rsqrt(jnp.mean(x * x, axis=-1, keepdims=True) + RMS_EPS) * g_ref[...]


def _final_norm(x2d, g, *, tm):
    rows = x2d.shape[0]
    return pl.pallas_call(
        _final_norm_kernel,
        grid=(rows // tm,),
        in_specs=[pl.BlockSpec((tm, D_MODEL), lambda i: (i, 0)), pl.BlockSpec((1, D_MODEL), lambda i: (0, 0))],
        out_specs=pl.BlockSpec((tm, D_MODEL), lambda i: (i, 0)),
        out_shape=jax.ShapeDtypeStruct((rows, D_MODEL), F32),
        compiler_params=_params(("arbitrary",)),
        name="final_norm",
    )(x2d, g.reshape(1, D_MODEL))


RWKV_BATCHES_PER_STEP = 8
RET_CHUNK = 256
S5_TIME_CHUNK = 128


MOE_TILE = 1024
MOE_SUB = 512
BF16_ROWS = 16


def _moe_cap(tm):
    return -(-(tm * 5 // 16) // BF16_ROWS) * BF16_ROWS


def _ffn_any(x2d, g, mod3, ffn, idx, *, rows_per_group):
    if len(ffn) == 3:
        tm = min(1024, rows_per_group)
        return _ffn(x2d, g, mod3, ffn, idx, tm=tm, tiles_per_group=rows_per_group // tm, tf=256)
    tm = min(MOE_TILE, rows_per_group)
    return _moe(x2d, g, mod3, ffn, idx, tm=tm, tiles_per_group=rows_per_group // tm, sub=min(MOE_SUB, tm))


def _prompt_layer(x2d, bsz, t_len, mod_l, lp, layer):
    mod3 = mod_l.reshape(bsz, 1, -1)
    tm = min(2048, t_len)
    tpg = t_len // tm
    proj, gates, u_tb = _inproj(x2d, lp['norm_mix'], mod3, lp['w_in'], layer, tm=tm, tiles_per_group=tpg,
                                tb_shape=(t_len, bsz * BRANCH_W))
    proj3 = proj.reshape(bsz, t_len, MIX_W)
    z_state = jnp.zeros((bsz, S5_W), F32)
    y_s5, s5_re, s5_im = _s5(u_tb.reshape(t_len * bsz, BRANCH_W), z_state, z_state, lp['s5p'], lp['s5_d'],
                             lp['s5_w_glu'], nb=bsz, t_len=t_len, tc=min(S5_TIME_CHUNK, t_len))
    z_bd = jnp.zeros((bsz, BRANCH_W, BRANCH_W), F32)
    y_rwkv, s_rwkv, shift_n = _rwkv(proj3, jnp.zeros((bsz, RWKV_IN_W), F32), z_bd, lp['rwkv'],
                                    nb=RWKV_BATCHES_PER_STEP)
    y_ret, s_ret = _ret(proj3, z_bd, 0, chunk=min(RET_CHUNK, t_len))
    y_moba = _moba_prompt(proj3)
    kv = lambda k: proj3[:, :, (COL_MOBA + k) * BRANCH_W:(COL_MOBA + k + 1) * BRANCH_W].reshape(
        bsz, t_len, N_HEADS, HEAD_DIM)
    tmm = min(512, t_len)
    tpm = t_len // tmm
    s5_spec = pl.BlockSpec((tmm, BRANCH_W), lambda i: (i % tpm, i // tpm))
    x2d = _merge(x2d, y_s5.reshape(t_len, bsz * BRANCH_W), s5_spec, y_rwkv, y_ret, y_moba,
                 gates, mod3, lp['w_branch'], lp['w_out'], layer, tm=tmm, tiles_per_group=tpm)
    x2d = _ffn_any(x2d, lp['norm_ffn'], mod3, lp['ffn'], layer // 2, rows_per_group=t_len)
    g16 = (bsz, S5_GROUPS, S5_STATE)
    return x2d, (s5_re.reshape(g16), s5_im.reshape(g16), s_rwkv, shift_n, s_ret, kv(1), kv(2))


def _decode_layer(x2d, mod_l, lp, layer, pos0, s5_re0, s5_im0, s_rwkv_t, shift0, s_ret_t, cache_k, cache_v,
                  page_table):
    bsz = x2d.shape[0]
    mod3 = mod_l.reshape(1, bsz, -1)
    proj, gates = _inproj(x2d, lp['norm_mix'], mod3, lp['w_in'], layer, tm=bsz, tiles_per_group=1)
    piece = lambda k: proj[:, k * BRANCH_W:(k + 1) * BRANCH_W]
    y_s5, s5_re, s5_im = _s5(piece(COL_S5), s5_re0.reshape(bsz, S5_W), s5_im0.reshape(bsz, S5_W), lp['s5p'],
                             lp['s5_d'], lp['s5_w_glu'], nb=bsz, t_len=1, tc=1)
    pre, feat_t = _dec_pre(proj, shift0, lp['rwkv'], pos0)
    s_rwkv, s_ret, y_raw_t, o_raw_t = _dec_state(s_rwkv_t, s_ret_t, feat_t, layer)
    y_rwkv, y_ret = _dec_post(y_raw_t, pre, o_raw_t, proj, lp['rwkv'])
    k_new = piece(COL_MOBA + 1)
    v_new = piece(COL_MOBA + 2)
    y_moba = _moba_dec(feat_t, cache_k, cache_v, page_table, layer).T
    s5_spec = pl.BlockSpec((bsz, BRANCH_W), lambda i: (i, 0))
    x2d = _merge(x2d, y_s5, s5_spec, y_rwkv[None], y_ret[None], y_moba[None], gates, mod3, lp['w_branch'],
                 lp['w_out'], layer, tm=bsz, tiles_per_group=1)
    x2d = _ffn_any(x2d, lp['norm_ffn'], mod3, lp['ffn'], layer // 2, rows_per_group=bsz)
    g16 = (bsz, S5_GROUPS, S5_STATE)
    kv4 = lambda t: t.reshape(bsz, 1, N_HEADS, HEAD_DIM)
    shift_n = proj[:, COL_RWKV * BRANCH_W:COL_RWKV * BRANCH_W + RWKV_IN_W]
    return x2d, (s5_re.reshape(g16), s5_im.reshape(g16), s_rwkv, shift_n, s_ret, kv4(k_new), kv4(v_new))


def kernel(x_prompt, x_sample, c_prompt, c_sample, state_s5_re, state_s5_im, state_rwkv, state_rwkv_shift, state_ret, cache_moba_k, cache_moba_v, page_table, norm_mix_g, norm_ffn_g, norm_final_g, w_ada, b_ada, w_in, s5_lambda_re, s5_lambda_im, s5_log_dt, s5_b_re, s5_b_im, s5_c_re, s5_c_im, s5_d, s5_w_glu, rwkv_mu, rwkv_w0, rwkv_w2, rwkv_a0, rwkv_a2, rwkv_g2, rwkv_k_k, rwkv_k_a, rwkv_r_k, rwkv_ln_g, rwkv_ln_b, w_branch, w_out, ffn_w1, ffn_w3, ffn_w2, moe_router, moe_w1, moe_w3, moe_w2):
    bp, t_len, _ = x_prompt.shape
    bs = x_sample.shape[0]
    depth = w_in.shape[0]
    past_len = page_table.shape[1] * cache_moba_k.shape[2]
    cache_kt = jnp.transpose(cache_moba_k, (0, 1, 3, 4, 2))
    cache_vt = jnp.transpose(cache_moba_v, (0, 1, 3, 4, 2))
    s_rwkv_t = jnp.transpose(state_rwkv, (0, 2, 3, 4, 1))
    s_ret_t = jnp.transpose(state_ret, (0, 2, 3, 4, 1))
    mod_all = _ada(jnp.concatenate([c_prompt, c_sample], axis=0), w_ada, b_ada)
    xp = x_prompt.reshape(bp * t_len, D_MODEL)
    xs = x_sample.reshape(bs, D_MODEL)
    outs_p = [[] for _ in range(7)]
    outs_s = [[] for _ in range(7)]
    dense = (ffn_w1, ffn_w3, ffn_w2)
    experts = (moe_router, moe_w1.astype(BF16), moe_w3.astype(BF16), moe_w2.astype(BF16))
    for l in range(depth):
        lp = {
            'norm_mix': norm_mix_g[l], 'norm_ffn': norm_ffn_g[l], 'w_in': w_in,
            's5p': _s5_params(s5_lambda_re[l], s5_lambda_im[l], s5_log_dt[l], s5_b_re[l], s5_b_im[l],
                              s5_c_re[l], s5_c_im[l]),
            's5_d': s5_d[l], 's5_w_glu': s5_w_glu[l],
            'rwkv': _rwkv_params(rwkv_mu[l], rwkv_w0[l], rwkv_w2[l], rwkv_a0[l], rwkv_a2[l], rwkv_g2[l],
                                 rwkv_k_k[l], rwkv_k_a[l], rwkv_r_k[l], rwkv_ln_g[l], rwkv_ln_b[l]),
            'w_branch': w_branch, 'w_out': w_out, 'ffn': dense if l % 2 == 0 else experts,
        }
        xp, st_p = _prompt_layer(xp, bp, t_len, mod_all[l, :bp], lp, l)
        xs, st_s = _decode_layer(xs, mod_all[l, bp:], lp, l, past_len, state_s5_re[l], state_s5_im[l],
                                 s_rwkv_t, state_rwkv_shift[l], s_ret_t, cache_kt, cache_vt, page_table)
        for j in range(7):
            outs_p[j].append(st_p[j])
            outs_s[j].append(st_s[j])
    y_prompt = _final_norm(xp, norm_final_g, tm=1024).reshape(bp, t_len, D_MODEL)
    y_sample = _final_norm(xs, norm_final_g, tm=bs).reshape(bs, 1, D_MODEL)
    stack = lambda outs: [jnp.stack(o, axis=0) for o in outs]
    dec = stack(outs_s)
    for j in (2, 4):
        dec[j] = jnp.transpose(dec[j], (0, 4, 1, 2, 3))
    return (y_prompt, y_sample, *stack(outs_p), *dec)
```
